```python
import math
import jax, jax.numpy as jnp
from jax import lax
import numpy as np

D_MODEL = 2048
BATCH = 2
SEQ = 4096
DEPTH = 1

DN_HEADS = 8
DN_DK = 128
DN_DV = 128
CONV_W = 4
RET_HEADS = 8
RET_DK = 64
RET_DV = 128
CHUNK = 64
ROPE_BASE = 10000.0
N_KEYS = 128
N_EXPERTS = N_KEYS * N_KEYS
PK_HEADS = 8
PK_DQ = 256
PK_TOPK = 16
TOK_BLOCK = 128
PLE_DIM = 256
EPS = 1e-6

DN_QK = DN_HEADS * DN_DK
DN_V = DN_HEADS * DN_DV
CONV_CH = 2 * DN_QK + DN_V
RET_QK = RET_HEADS * RET_DK
RET_V = RET_HEADS * RET_DV
D_MIX = DN_V + RET_V
SPLIT_SIZES = (CONV_CH, DN_V, DN_HEADS, DN_HEADS, RET_QK, RET_QK, RET_V, RET_V)
IN_COLS = CONV_CH + DN_V + 2 * DN_HEADS + 2 * RET_QK + 2 * RET_V

kernel_name = "hybrid_deltanet_retention_peer_block"


def rms_norm(x, w):
    xf = x.astype(jnp.float32)
    y = xf * lax.rsqrt(jnp.mean(xf * xf, axis=-1, keepdims=True) + EPS)
    return (y * w.astype(jnp.float32)).astype(x.dtype)


def l2norm(x):
    return x * lax.rsqrt(jnp.sum(x * x, axis=-1, keepdims=True) + EPS)


def causal_conv_silu(x, w):
    s = x.shape[1]
    xp = jnp.pad(x, ((0, 0), (CONV_W - 1, 0), (0, 0)))
    y = sum(xp[:, j:j + s, :] * w[:, j] for j in range(CONV_W))
    return jax.nn.silu(y)


def rotary(x, positions):
    half = x.shape[-1] // 2
    inv_freq = ROPE_BASE ** (-jnp.arange(half, dtype=jnp.float32) / half)
    ang = positions.astype(jnp.float32)[..., None] * inv_freq
    cos = jnp.cos(ang)[:, :, None, :]
    sin = jnp.sin(ang)[:, :, None, :]
    x1, x2 = x[..., :half], x[..., half:]
    return jnp.concatenate([x1 * cos - x2 * sin, x1 * sin + x2 * cos], axis=-1)


def _chunks(t, n):
    b, _, h = t.shape[:3]
    t = t.reshape((b, n, CHUNK, h) + t.shape[3:])
    return jnp.moveaxis(t, 3, 1)


def _unchunk(t):
    b, h, n, c = t.shape[:4]
    return jnp.moveaxis(t, 1, 3).reshape((b, n * c, h) + t.shape[4:])


def gated_delta_rule(q, k, v, g, beta):
    b, s, h, dk = q.shape
    dv = v.shape[-1]
    n = s // CHUNK
    q, k, v, g, beta = (_chunks(t, n) for t in (q, k, v, g, beta))
    g = jnp.cumsum(g, axis=-1)
    causal = jnp.tril(jnp.ones((CHUNK, CHUNK), dtype=bool))
    strict = jnp.tril(jnp.ones((CHUNK, CHUNK), dtype=bool), -1)
    decay = jnp.exp(jnp.where(causal, g[..., :, None] - g[..., None, :], -jnp.inf))
    k_beta = k * beta[..., None]
    kk = jnp.einsum('bhncd,bhnmd->bhncm', k_beta, k) * decay
    a_mat = jnp.where(strict, kk, 0.0) + jnp.eye(CHUNK, dtype=q.dtype)
    rhs = jnp.concatenate([v * beta[..., None], k_beta * jnp.exp(g)[..., None]], axis=-1)
    sol = lax.linalg.triangular_solve(a_mat, rhs, left_side=True, lower=True, unit_diagonal=True)
    u, w = sol[..., :dv], sol[..., dv:]
    attn = jnp.einsum('bhncd,bhnmd->bhncm', q, k) * decay

    def step(state, xs):
        q_c, k_c, u_c, w_c, g_c, attn_c = xs
        v_new = u_c - jnp.einsum('bhck,bhkv->bhcv', w_c, state)
        o_c = (jnp.einsum('bhck,bhkv->bhcv', q_c * jnp.exp(g_c)[..., None], state)
               + jnp.einsum('bhcm,bhmv->bhcv', attn_c, v_new))
        g_end = g_c[..., -1:]
        state = (state * jnp.exp(g_end)[..., None]
                 + jnp.einsum('bhck,bhcv->bhkv', k_c * jnp.exp(g_end - g_c)[..., None], v_new))
        return state, o_c

    state0 = jnp.zeros((b, h, dk, dv), q.dtype)
    xs = tuple(jnp.moveaxis(t, 2, 0) for t in (q, k, u, w, g, attn))
    _, out = lax.scan(step, state0, xs)
    return _unchunk(jnp.moveaxis(out, 0, 2))


def retention_chunked(q, k, v):
    b, s, h, dk = q.shape
    dv = v.shape[-1]
    n = s // CHUNK
    q, k, v = (_chunks(t, n) for t in (q, k, v))
    log_gamma = jnp.log1p(-jnp.exp2(-5.0 - jnp.arange(RET_HEADS, dtype=jnp.float32)))
    pos = jnp.arange(CHUNK, dtype=jnp.float32)
    rel = pos[:, None] - pos[None, :]
    d_mat = jnp.exp(jnp.where(rel >= 0, rel * log_gamma[:, None, None], -jnp.inf))
    attn = jnp.einsum('bhncd,bhnmd->bhncm', q, k) * d_mat[None, :, None]
    inner = jnp.einsum('bhncm,bhnmv->bhncv', attn, v)
    xi = jnp.exp((pos + 1.0) * log_gamma[:, None])[None, :, :, None]
    zeta = jnp.exp((CHUNK - 1.0 - pos) * log_gamma[:, None])[None, :, :, None]
    chunk_decay = jnp.exp(CHUNK * log_gamma)[None, :, None, None]

    def step(state, xs):
        q_c, k_c, v_c = xs
        cross = jnp.einsum('bhck,bhkv->bhcv', q_c, state) * xi
        state = state * chunk_decay + jnp.einsum('bhck,bhcv->bhkv', k_c * zeta, v_c)
        return state, cross

    state0 = jnp.zeros((b, h, dk, dv), q.dtype)
    xs = tuple(jnp.moveaxis(t, 2, 0) for t in (q, k, v))
    _, cross = lax.scan(step, state0, xs)
    return _unchunk(inner + jnp.moveaxis(cross, 0, 2))


def peer(xn, w_pq, sub_keys, expert_u, expert_v):
    b, s, d = xn.shape
    t = b * s
    xt = xn.reshape(t, d)
    q = (xt @ w_pq).reshape(t, PK_HEADS, 2, PK_DQ // 2)
    scores = jnp.einsum('thpc,hpnc->thpn', q, sub_keys).astype(jnp.float32)
    s1, i1 = lax.top_k(scores[:, :, 0], PK_TOPK)
    s2, i2 = lax.top_k(scores[:, :, 1], PK_TOPK)
    cand_s = (s1[..., :, None] + s2[..., None, :]).reshape(t, PK_HEADS, PK_TOPK * PK_TOPK)
    cand_i = (i1[..., :, None] * N_KEYS + i2[..., None, :]).reshape(t, PK_HEADS, PK_TOPK * PK_TOPK)
    top_s, sel = lax.top_k(cand_s, PK_TOPK)
    idx = jnp.take_along_axis(cand_i, sel, axis=-1)
    gates = jax.nn.softmax(top_s, axis=-1)
    nb = t // TOK_BLOCK
    idx = idx.reshape(nb, TOK_BLOCK, PK_HEADS * PK_TOPK)
    gates = gates.reshape(nb, TOK_BLOCK, PK_HEADS * PK_TOPK)
    xb = xt.reshape(nb, TOK_BLOCK, d)

    def block(args):
        x_blk, i_blk, g_blk = args
        pre = jnp.einsum('tkd,td->tk', expert_u[i_blk], x_blk).astype(jnp.float32)
        act = jax.nn.gelu(pre, approximate=False) * g_blk
        return jnp.einsum('tk,tkd->td', act.astype(x_blk.dtype), expert_v[i_blk])

    out = lax.map(block, (xb, idx, gates))
    return out.reshape(b, s, d)


def setup_inputs(seed: int = 0) -> dict:
    key = jax.random.key(seed)
    ks = jax.random.split(key, 24)
    f32 = jnp.float32

    def nrm(k, shape, scale):
        return jax.random.normal(k, shape, f32) * scale

    def gain(k, shape):
        return 1.0 + 0.02 * jax.random.normal(k, shape, f32)

    L = DEPTH
    x = nrm(ks[0], (BATCH, SEQ, D_MODEL), 1.0)
    p = nrm(ks[1], (L, BATCH, SEQ, PLE_DIM), 1.0)
    positions = (jax.random.randint(ks[2], (BATCH, 1), 0, 1024, dtype=jnp.int32)
                 + jnp.arange(SEQ, dtype=jnp.int32)[None, :])
    norm_mix = gain(ks[3], (L, D_MODEL))
    w_in = nrm(ks[4], (L, D_MODEL, IN_COLS), D_MODEL ** -0.5)
    conv_w = nrm(ks[5], (L, CONV_CH, CONV_W), CONV_W ** -0.5)
    a_log = jnp.log(jax.random.uniform(ks[6], (L, DN_HEADS), f32, 1.0, 16.0))
    dt = jnp.exp(jax.random.uniform(ks[7], (L, DN_HEADS), f32, math.log(1e-3), math.log(1e-1)))
    dt_bias = dt + jnp.log(-jnp.expm1(-dt))
    dn_norm = gain(ks[8], (L, DN_DV))
    ret_gn_w = gain(ks[9], (L, RET_V))
    ret_gn_b = nrm(ks[10], (L, RET_V), 0.02)
    w_out = nrm(ks[11], (L, D_MIX, D_MODEL), D_MIX ** -0.5)
    norm_ffn = gain(ks[12], (L, D_MODEL))
    w_pq = nrm(ks[13], (L, D_MODEL, PK_HEADS * PK_DQ), D_MODEL ** -0.5)
    sub_keys = nrm(ks[14], (L, PK_HEADS, 2, N_KEYS, PK_DQ // 2), (PK_DQ // 2) ** -0.5)
    expert_u = nrm(ks[15], (L, N_EXPERTS, D_MODEL), D_MODEL ** -0.5)
    expert_v = nrm(ks[16], (L, N_EXPERTS, D_MODEL), PK_HEADS ** -0.5)
    norm_ple = gain(ks[17], (L, D_MODEL))
    w_ple_gate = nrm(ks[18], (L, D_MODEL, D_MODEL), D_MODEL ** -0.5)
    w_ple_proj = nrm(ks[19], (L, PLE_DIM, D_MODEL), PLE_DIM ** -0.5)
    norm_final = gain(ks[20], (D_MODEL,))
    return {"x": x, "p": p, "positions": positions, "norm_mix": norm_mix, "w_in": w_in,
            "conv_w": conv_w, "a_log": a_log, "dt_bias": dt_bias, "dn_norm": dn_norm,
            "ret_gn_w": ret_gn_w, "ret_gn_b": ret_gn_b, "w_out": w_out, "norm_ffn": norm_ffn,
            "w_pq": w_pq, "sub_keys": sub_keys, "expert_u": expert_u, "expert_v": expert_v,
            "norm_ple": norm_ple, "w_ple_gate": w_ple_gate, "w_ple_proj": w_ple_proj,
            "norm_final": norm_final}


def reference(x, p, positions, norm_mix, w_in, conv_w, a_log, dt_bias, dn_norm, ret_gn_w, ret_gn_b,
              w_out, norm_ffn, w_pq, sub_keys, expert_u, expert_v, norm_ple, w_ple_gate, w_ple_proj,
              norm_final):
    f32 = jnp.float32
    b, s, _ = x.shape
    splits = [int(c) for c in np.cumsum(SPLIT_SIZES)[:-1]]
    h = x
    for i in range(DEPTH):
        hn = rms_norm(h, norm_mix[i])
        proj = jnp.einsum('bsd,dc->bsc', hn, w_in[i]).astype(f32)
        conv_in, dn_z, dn_a, dn_b, r_q, r_k, r_v, r_g = jnp.split(proj, splits, axis=-1)

        qkv = causal_conv_silu(conv_in, conv_w[i].astype(f32))
        dq, dk, dv = jnp.split(qkv, [DN_QK, 2 * DN_QK], axis=-1)
        dq = l2norm(dq.reshape(b, s, DN_HEADS, DN_DK)) * (DN_DK ** -0.5)
        dk = l2norm(dk.reshape(b, s, DN_HEADS, DN_DK))
        dv = dv.reshape(b, s, DN_HEADS, DN_DV)
        beta = jax.nn.sigmoid(dn_b)
        g = -jnp.exp(a_log[i].astype(f32)) * jax.nn.softplus(dn_a + dt_bias[i].astype(f32))
        o_dn = gated_delta_rule(dq, dk, dv, g, beta)
        o_dn = rms_norm(o_dn, dn_norm[i]) * jax.nn.silu(dn_z.reshape(b, s, DN_HEADS, DN_DV))
        o_dn = o_dn.reshape(b, s, DN_V)

        rq = rotary(r_q.reshape(b, s, RET_HEADS, RET_DK), positions)
        rk = rotary(r_k.reshape(b, s, RET_HEADS, RET_DK), positions) * (RET_DK ** -0.5)
        rv = r_v.reshape(b, s, RET_HEADS, RET_DV)
        o_r = retention_chunked(rq, rk, rv)
        mu = jnp.mean(o_r, axis=-1, keepdims=True)
        var = jnp.mean(jnp.square(o_r - mu), axis=-1, keepdims=True)
        o_r = ((o_r - mu) * lax.rsqrt(var + EPS)).reshape(b, s, RET_V)
        o_r = (o_r * ret_gn_w[i].astype(f32) + ret_gn_b[i].astype(f32)) * jax.nn.silu(r_g)

        mixed = jnp.concatenate([o_dn, o_r], axis=-1).astype(h.dtype)
        h = h + jnp.einsum('bsc,cd->bsd', mixed, w_out[i])

        hn = rms_norm(h, norm_ffn[i])
        h = h + peer(hn, w_pq[i], sub_keys[i], expert_u[i], expert_v[i])

        gate = jax.nn.sigmoid((rms_norm(h, norm_ple[i]) @ w_ple_gate[i]).astype(f32))
        ple = (p[i] @ w_ple_proj[i]).astype(f32)
        h = h + (gate * ple).astype(h.dtype)
    return rms_norm(h, norm_final)
```

```python
import functools
import math

import numpy as np
import jax
import jax.numpy as jnp
from jax import lax
from jax.experimental import pallas as pl
from jax.experimental.pallas import tpu as pltpu

F32 = jnp.float32
BF16 = jnp.bfloat16
HIGHEST = lax.Precision.HIGHEST
NEG_INF = float("-inf")

EPS = 1e-6
D_MODEL = 2048
DN_HEADS = 8
DN_DK = 128
DN_DV = 128
CONV_W = 4
RET_HEADS = 8
RET_DK = 64
RET_DV = 128
CHUNK = 64
ROPE_BASE = 10000.0
N_KEYS = 128
PK_HEADS = 8
PK_DQ = 256
PK_TOPK = 16
PLE_DIM = 256

DN_QK = DN_HEADS * DN_DK
DN_V = DN_HEADS * DN_DV
CONV_CH = 2 * DN_QK + DN_V
RET_QK = RET_HEADS * RET_DK
RET_V = RET_HEADS * RET_DV
AB_OFF = CONV_CH + DN_V
MAIN_COLS = CONV_CH + DN_V + 2 * RET_QK + 2 * RET_V
LANE = 128
HALO = 8

TM_INPROJ = 512
TN_INPROJ = 1024
TT_SEQ = 256
TM_DENSE = 256
TL_ROUTER = 256
TM_PEER = 512
NI_PEER = 8
TE_PEER = NI_PEER * N_KEYS
VMEM_LIMIT = 56 * 1024 * 1024


def _cparams(sem):
    return pltpu.CompilerParams(dimension_semantics=sem, vmem_limit_bytes=VMEM_LIMIT)


def _rms(x, w):
    return x * lax.rsqrt(jnp.mean(x * x, axis=-1, keepdims=True) + EPS) * w


def _silu(x):
    return x * jax.nn.sigmoid(x)


def _softplus(x):
    return jnp.maximum(x, 0.0) + jnp.log1p(jnp.exp(-jnp.abs(x)))


def _gelu(x):
    return 0.5 * x * (1.0 + lax.erf(x * (2.0 ** -0.5)))


def _mm(a, b):
    return jnp.dot(a.astype(BF16), b.astype(BF16), preferred_element_type=F32)


def _mm_nt(a, b):
    return lax.dot_general(a.astype(BF16), b.astype(BF16), (((1,), (1,)), ((), ())),
                           preferred_element_type=F32)


def _mm_tn(a, b):
    return lax.dot_general(a.astype(BF16), b.astype(BF16), (((0,), (0,)), ((), ())),
                           preferred_element_type=F32)


def _mm_exact(a, b):
    return jnp.dot(a, b, precision=HIGHEST, preferred_element_type=F32)


def _inproj_body(x_ref, nw_ref, w_ref, wab_ref, o_ref, oab_ref, hn_ref):
    @pl.when(pl.program_id(1) == 0)
    def _():
        hn_ref[...] = _rms(x_ref[...], nw_ref[...]).astype(BF16)
        oab_ref[...] = jnp.dot(hn_ref[...], wab_ref[...], preferred_element_type=F32)

    o_ref[...] = jnp.dot(hn_ref[...], w_ref[...], preferred_element_type=F32)


def _inproj(x, nw, w_main, w_ab):
    t, d = x.shape
    n = w_main.shape[1]
    return pl.pallas_call(
        _inproj_body,
        grid=(t // TM_INPROJ, n // TN_INPROJ),
        in_specs=[
            pl.BlockSpec((TM_INPROJ, d), lambda i, j: (i, 0)),
            pl.BlockSpec((1, d), lambda i, j: (0, 0)),
            pl.BlockSpec((d, TN_INPROJ), lambda i, j: (0, j)),
            pl.BlockSpec((d, LANE), lambda i, j: (0, 0)),
        ],
        out_specs=[
            pl.BlockSpec((TM_INPROJ, TN_INPROJ), lambda i, j: (i, j)),
            pl.BlockSpec((TM_INPROJ, LANE), lambda i, j: (i, 0)),
        ],
        out_shape=[jax.ShapeDtypeStruct((t, n), F32), jax.ShapeDtypeStruct((t, LANE), F32)],
        scratch_shapes=[pltpu.VMEM((TM_INPROJ, d), BF16)],
        compiler_params=_cparams(("parallel", "arbitrary")),
        name="inproj",
    )(x, nw, w_main, w_ab)


def _dn_local_body(q_ref, k_ref, v_ref, qh_ref, kh_ref, vh_ref, ab_ref, cw_ref, alog_ref, dtb_ref,
                   selg_ref, selb_ref, qg_ref, kd_ref, u_ref, w_ref, attn_ref, gend_ref, buf_ref):
    tt = TT_SEQ
    first_tile = pl.program_id(1) == 0

    def conv_silu(main_ref, halo_ref, c0):
        halo = halo_ref[...]
        buf_ref[0:HALO, :] = jnp.where(first_tile, jnp.zeros_like(halo), halo)
        buf_ref[HALO:HALO + tt, :] = main_ref[...]
        y = None
        for j in range(CONV_W):
            off = HALO - (CONV_W - 1) + j
            term = buf_ref[off:off + tt, :] * cw_ref[j:j + 1, c0:c0 + DN_QK]
            y = term if y is None else y + term
        return _silu(y)

    q = conv_silu(q_ref, qh_ref, 0)
    k = conv_silu(k_ref, kh_ref, DN_QK)
    v = conv_silu(v_ref, vh_ref, 2 * DN_QK)

    row = lax.broadcasted_iota(jnp.int32, (tt, tt), 0)
    col = lax.broadcasted_iota(jnp.int32, (tt, tt), 1)

    def same_block(bits):
        return (row >> bits) == (col >> bits)

    blk8, blk16, blk32, blk64 = same_block(3), same_block(4), same_block(5), same_block(6)
    causal = blk64 & (col <= row)
    strict = blk64 & (col < row)
    eye = (row == col).astype(F32)

    ab = ab_ref[...]
    g = -jnp.exp(alog_ref[...]) * _softplus(ab + dtb_ref[...])
    beta = jax.nn.sigmoid(ab)
    gc = _mm_exact(causal.astype(F32), g)
    gtot = _mm_exact(blk64.astype(F32), g)
    e_gc = _mm_exact(gc, selg_ref[...])
    e_gt = _mm_exact(gtot, selg_ref[...])
    e_beta = _mm_exact(beta, selb_ref[...])
    r128 = lax.broadcasted_iota(jnp.int32, (LANE, LANE), 0)
    c128 = lax.broadcasted_iota(jnp.int32, (LANE, LANE), 1)
    gc_t = lax.dot_general((r128 == c128).astype(F32), gc, (((1,), (1,)), ((), ())),
                           precision=HIGHEST, preferred_element_type=F32)
    eg = jnp.exp(e_gc)
    kdec = jnp.exp(e_gt - e_gc)

    for c in range(tt // CHUNK):
        gend_ref[0, c:c + 1, :] = jnp.exp(e_gt[c * CHUNK:c * CHUNK + 1, :])
    gend_ref[0, tt // CHUNK:, :] = jnp.zeros((HALO - tt // CHUNK, DN_V), F32)

    for h in range(DN_HEADS):
        hs = slice(h * DN_DK, (h + 1) * DN_DK)
        qh = q[:, hs]
        qh = qh * lax.rsqrt(jnp.sum(qh * qh, axis=-1, keepdims=True) + EPS) * (DN_DK ** -0.5)
        kh = k[:, hs]
        kh = kh * lax.rsqrt(jnp.sum(kh * kh, axis=-1, keepdims=True) + EPS)
        vh = v[:, hs]
        beta_h = e_beta[:, hs]
        kb = kh * beta_h
        gcol = e_gc[:, hs]
        diff = jnp.concatenate([gcol] * (tt // LANE), axis=1) - gc_t[h:h + 1, :]
        dmat = jnp.exp(jnp.where(causal, diff, NEG_INF))
        khb = kh.astype(BF16)
        nmat = jnp.where(strict, _mm_nt(kb, khb) * dmat, 0.0)
        attn_ref[0, h] = (_mm_nt(qh, khb) * dmat).astype(BF16)

        n0 = jnp.where(blk8, nmat, 0.0)
        n2 = _mm(n0, n0)
        n4 = _mm(n2, n2)
        inv = eye - n0
        inv = inv + _mm(inv, n2)
        inv = inv + _mm(inv, n4)
        for inner, outer in ((blk8, blk16), (blk16, blk32), (blk32, blk64)):
            off_diag = jnp.where(outer & jnp.logical_not(inner), nmat, 0.0)
            inv = inv - _mm(_mm(inv, off_diag), inv)

        rhs = jnp.concatenate([vh * beta_h, kb * eg[:, hs]], axis=1)
        sol = _mm(inv, rhs)
        u_ref[:, hs] = sol[:, :DN_DV]
        w_ref[:, hs] = sol[:, DN_DV:].astype(BF16)
        qg_ref[:, hs] = (qh * eg[:, hs]).astype(BF16)
        kd_ref[:, hs] = (kh * kdec[:, hs]).astype(BF16)


def _dn_local(proj, proj_ab, cw, alog_row, dtb_row, selg, selb, b, s):
    t = b * s
    nt = s // TT_SEQ
    rows_per_halo = TT_SEQ // HALO

    def main_spec(cb):
        return pl.BlockSpec((TT_SEQ, DN_QK), lambda bi, i: (bi * nt + i, cb))

    def halo_spec(cb):
        return pl.BlockSpec(
            (HALO, DN_QK), lambda bi, i: (jnp.maximum((bi * nt + i) * rows_per_halo - 1, 0), cb))

    def full(shape):
        return pl.BlockSpec(shape, lambda bi, i: (0,) * len(shape))

    tok_spec = pl.BlockSpec((TT_SEQ, DN_V), lambda bi, i: (bi * nt + i, 0))
    return pl.pallas_call(
        _dn_local_body,
        grid=(b, nt),
        in_specs=[main_spec(0), main_spec(1), main_spec(2), halo_spec(0), halo_spec(1), halo_spec(2),
                  pl.BlockSpec((TT_SEQ, LANE), lambda bi, i: (bi * nt + i, 0)),
                  full((CONV_W, CONV_CH)), full((1, LANE)), full((1, LANE)),
                  full((LANE, DN_V)), full((LANE, DN_V))],
        out_specs=[tok_spec, tok_spec, tok_spec, tok_spec,
                   pl.BlockSpec((1, DN_HEADS, TT_SEQ, TT_SEQ), lambda bi, i: (bi * nt + i, 0, 0, 0)),
                   pl.BlockSpec((1, HALO, DN_V), lambda bi, i: (bi * nt + i, 0, 0))],
        out_shape=[jax.ShapeDtypeStruct((t, DN_V), BF16),
                   jax.ShapeDtypeStruct((t, DN_V), BF16),
                   jax.ShapeDtypeStruct((t, DN_V), F32),
                   jax.ShapeDtypeStruct((t, DN_V), BF16),
                   jax.ShapeDtypeStruct((t // TT_SEQ, DN_HEADS, TT_SEQ, TT_SEQ), BF16),
                   jax.ShapeDtypeStruct((t // TT_SEQ, HALO, DN_V), F32)],
        scratch_shapes=[pltpu.VMEM((HALO + TT_SEQ, DN_QK), F32)],
        compiler_params=_cparams(("parallel", "parallel")),
        name="dn_local",
    )(proj, proj, proj, proj, proj, proj, proj_ab, cw, alog_row, dtb_row, selg, selb)


def _dn_scan_body(qg_ref, kd_ref, u_ref, w_ref, attn_ref, gend_ref, o_ref, s_ref, vn_ref):
    nb = qg_ref.shape[0]

    @pl.when(pl.program_id(0) == 0)
    def _():
        s_ref[...] = jnp.zeros(s_ref.shape, F32)

    for b in range(nb):
        for h in range(DN_HEADS):
            ch = b * DN_HEADS + h
            hs = slice(h * DN_DK, (h + 1) * DN_DK)
            state = s_ref[ch]
            vn_ref[ch] = jnp.zeros((TT_SEQ, DN_DV), BF16)
            for c in range(TT_SEQ // CHUNK):
                rs = slice(c * CHUNK, (c + 1) * CHUNK)
                sb = state.astype(BF16)
                v_new = u_ref[b, rs, hs] - jnp.dot(w_ref[b, rs, hs], sb, preferred_element_type=F32)
                vnb = v_new.astype(BF16)
                vn_ref[ch, rs, :] = vnb
                o_ref[b, rs, hs] = (
                    jnp.dot(qg_ref[b, rs, hs], sb, preferred_element_type=F32)
                    + jnp.dot(attn_ref[b, 0, h, rs, :], vn_ref[ch], preferred_element_type=F32))
                state = state * gend_ref[b, 0, c:c + 1, hs] + _mm_tn(kd_ref[b, rs, hs], vnb)
            s_ref[ch] = state


def _dn_scan(qg, kd, u, w, attn, gend, b, s):
    nt = s // TT_SEQ
    tok = lambda a: a.reshape(b, s, DN_V)
    tok_spec = pl.BlockSpec((b, TT_SEQ, DN_V), lambda i: (0, i, 0))
    out = pl.pallas_call(
        _dn_scan_body,
        grid=(nt,),
        in_specs=[tok_spec, tok_spec, tok_spec, tok_spec,
                  pl.BlockSpec((b, 1, DN_HEADS, TT_SEQ, TT_SEQ), lambda i: (0, i, 0, 0, 0)),
                  pl.BlockSpec((b, 1, HALO, DN_V), lambda i: (0, i, 0, 0))],
        out_specs=tok_spec,
        out_shape=jax.ShapeDtypeStruct((b, s, DN_V), F32),
        scratch_shapes=[pltpu.VMEM((b * DN_HEADS, DN_DK, DN_DV), F32),
                        pltpu.VMEM((b * DN_HEADS, TT_SEQ, DN_DV), BF16)],
        compiler_params=_cparams(("arbitrary",)),
        name="dn_scan",
    )(tok(qg), tok(kd), tok(u), tok(w),
      attn.reshape(b, nt, DN_HEADS, TT_SEQ, TT_SEQ), gend.reshape(b, nt, HALO, DN_V))
    return out.reshape(b * s, DN_V)


def _log_gamma(h):
    return math.log1p(-(2.0 ** (-5.0 - h)))


def _ret_body(qk_ref, v_ref, pos_ref, freq_ref, o_ref, s_ref):
    tt = TT_SEQ

    @pl.when(pl.program_id(1) == 0)
    def _():
        s_ref[...] = jnp.zeros(s_ref.shape, F32)

    ang = pos_ref[...].astype(F32) * freq_ref[...]
    lane = lax.broadcasted_iota(jnp.int32, (tt, LANE), 1)
    first_half = (lane & (RET_DK // 2)) == 0
    cos = jnp.cos(ang)
    sin = jnp.sin(ang)
    ssin = jnp.where(first_half, -sin, sin)

    def rotary(x):
        swapped = jnp.where(first_half, pltpu.roll(x, LANE - RET_DK // 2, axis=1),
                            pltpu.roll(x, RET_DK // 2, axis=1))
        return x * cos + swapped * ssin

    row = lax.broadcasted_iota(jnp.int32, (tt, tt), 0)
    col = lax.broadcasted_iota(jnp.int32, (tt, tt), 1)
    rel = (row - col).astype(F32)
    causal = row >= col
    trow = lax.broadcasted_iota(jnp.int32, (tt, LANE), 0).astype(F32)
    srow = lax.broadcasted_iota(jnp.int32, (LANE, LANE), 0)

    for pr in range(RET_HEADS // 2):
        ps = slice(pr * LANE, (pr + 1) * LANE)
        qp = rotary(qk_ref[:, ps])
        kp = rotary(qk_ref[:, RET_QK + pr * LANE:RET_QK + (pr + 1) * LANE]) * (RET_DK ** -0.5)
        kpb = kp.astype(BF16)
        state = s_ref[pr]
        sb = state.astype(BF16)
        update = jnp.zeros((LANE, RET_DV), F32)
        for hh in range(2):
            h = 2 * pr + hh
            lg = _log_gamma(h)
            mine = (lane < RET_DK) if hh == 0 else (lane >= RET_DK)
            qm = jnp.where(mine, qp, 0.0).astype(BF16)
            vh = v_ref[:, h * RET_DV:(h + 1) * RET_DV].astype(BF16)
            att = _mm_nt(qm, kpb) * jnp.exp(jnp.where(causal, rel * lg, NEG_INF))
            inner = jnp.dot(att.astype(BF16), vh, preferred_element_type=F32)
            cross = jnp.dot(qm, sb, preferred_element_type=F32) * jnp.exp((trow + 1.0) * lg)
            o_ref[:, h * RET_DV:(h + 1) * RET_DV] = inner + cross
            kz = jnp.where(mine, kp, 0.0) * jnp.exp((tt - 1.0 - trow) * lg)
            update = update + _mm_tn(kz, vh)
        decay = jnp.where(srow < RET_DK, math.exp(tt * _log_gamma(2 * pr)),
                          math.exp(tt * _log_gamma(2 * pr + 1)))
        s_ref[pr] = state * decay + update


def _ret(proj, pos, freq_row, b, s):
    t = b * s
    nt = s // TT_SEQ
    return pl.pallas_call(
        _ret_body,
        grid=(b, nt),
        in_specs=[pl.BlockSpec((TT_SEQ, 2 * RET_QK), lambda bi, i: (bi * nt + i, 4)),
                  pl.BlockSpec((TT_SEQ, RET_V), lambda bi, i: (bi * nt + i, 5)),
                  pl.BlockSpec((TT_SEQ, 1), lambda bi, i: (bi * nt + i, 0)),
                  pl.BlockSpec((1, LANE), lambda bi, i: (0, 0))],
        out_specs=pl.BlockSpec((TT_SEQ, RET_V), lambda bi, i: (bi * nt + i, 0)),
        out_shape=jax.ShapeDtypeStruct((t, RET_V), F32),
        scratch_shapes=[pltpu.VMEM((RET_HEADS // 2, LANE, RET_DV), F32)],
        compiler_params=_cparams(("parallel", "arbitrary")),
        name="ret",
    )(proj, proj, pos, freq_row)


def _outproj_body(odn_ref, z_ref, or_ref, rg_ref, x_ref, dnw_ref, gnw_ref, gnb_ref, w_ref, o_ref,
                  mix_ref):
    for h in range(DN_HEADS):
        hs = slice(h * DN_DV, (h + 1) * DN_DV)
        mix_ref[:, hs] = (_rms(odn_ref[:, hs], dnw_ref[...]) * _silu(z_ref[:, hs])).astype(BF16)
    for h in range(RET_HEADS):
        hs = slice(h * RET_DV, (h + 1) * RET_DV)
        o = or_ref[:, hs]
        cen = o - jnp.mean(o, axis=-1, keepdims=True)
        y = cen * lax.rsqrt(jnp.mean(cen * cen, axis=-1, keepdims=True) + EPS)
        y = (y * gnw_ref[:, hs] + gnb_ref[:, hs]) * _silu(rg_ref[:, hs])
        mix_ref[:, DN_V + h * RET_DV:DN_V + (h + 1) * RET_DV] = y.astype(BF16)
    o_ref[...] = x_ref[...] + jnp.dot(mix_ref[...], w_ref[...], preferred_element_type=F32)


def _outproj(o_dn, proj, o_r, x, dnw, gnw, gnb, w_out):
    t, d = x.shape
    tm = TM_DENSE
    full = lambda shape: pl.BlockSpec(shape, lambda i: (0, 0))
    return pl.pallas_call(
        _outproj_body,
        grid=(t // tm,),
        in_specs=[pl.BlockSpec((tm, DN_V), lambda i: (i, 0)),
                  pl.BlockSpec((tm, DN_V), lambda i: (i, 3)),
                  pl.BlockSpec((tm, RET_V), lambda i: (i, 0)),
                  pl.BlockSpec((tm, RET_V), lambda i: (i, 6)),
                  pl.BlockSpec((tm, d), lambda i: (i, 0)),
                  full((1, DN_DV)), full((1, RET_V)), full((1, RET_V)), full((DN_V + RET_V, d))],
        out_specs=pl.BlockSpec((tm, d), lambda i: (i, 0)),
        out_shape=jax.ShapeDtypeStruct((t, d), F32),
        scratch_shapes=[pltpu.VMEM((tm, DN_V + RET_V), BF16)],
        compiler_params=_cparams(("parallel",)),
        name="outproj",
    )(o_dn, proj, o_r, proj, x, dnw, gnw, gnb, w_out)


def _pq_body(h_ref, nw_ref, w_ref, keys_ref, hnt_ref, sc_ref):
    hn = _rms(h_ref[...], nw_ref[...])
    hnt_ref[...] = hn.T.astype(BF16)
    q = jnp.dot(hn.astype(BF16), w_ref[...], preferred_element_type=F32)
    half = PK_DQ // 2
    for hp in range(2 * PK_HEADS):
        sc_ref[hp] = _mm_nt(keys_ref[hp], q[:, hp * half:(hp + 1) * half])


def _pq(h1, nw, w_pq, keys):
    t, d = h1.shape
    tm = TM_DENSE
    nk = 2 * PK_HEADS
    return pl.pallas_call(
        _pq_body,
        grid=(t // tm,),
        in_specs=[pl.BlockSpec((tm, d), lambda i: (i, 0)),
                  pl.BlockSpec((1, d), lambda i: (0, 0)),
                  pl.BlockSpec((d, PK_HEADS * PK_DQ), lambda i: (0, 0)),
                  pl.BlockSpec((nk, N_KEYS, PK_DQ // 2), lambda i: (0, 0, 0))],
        out_specs=[pl.BlockSpec((d, tm), lambda i: (0, i)),
                   pl.BlockSpec((nk, N_KEYS, tm), lambda i: (0, 0, i))],
        out_shape=[jax.ShapeDtypeStruct((d, t), BF16),
                   jax.ShapeDtypeStruct((nk, N_KEYS, t), F32)],
        compiler_params=_cparams(("parallel",)),
        name="pq",
    )(h1, nw, w_pq, keys)


def _router_body(sc_ref, rankb_ref, eb_ref, cnt_ref, ea_ref):
    tl = TL_ROUTER
    k = PK_TOPK
    keyid = lax.broadcasted_iota(jnp.int32, (N_KEYS, tl), 0)
    slot = lax.broadcasted_iota(jnp.int32, (k, tl), 0)

    def top_k(s):
        rank = jnp.full((N_KEYS, tl), k, jnp.int32)
        vals = jnp.zeros((k, tl), F32)
        for r in range(k):
            m = jnp.max(s, axis=0, keepdims=True)
            pick = jnp.min(jnp.where(s == m, keyid, N_KEYS), axis=0, keepdims=True)
            hit = keyid == pick
            rank = jnp.where(hit, r, rank)
            s = jnp.where(hit, NEG_INF, s)
            vals = jnp.where(slot == r, m, vals)
        return vals, rank

    a = sc_ref[0]
    bsc = sc_ref[1]
    av, rank_a = top_k(a)
    bv, rank_b = top_k(bsc)

    cand = jnp.concatenate([av[r:r + 1, :] + bv for r in range(k)], axis=0)
    cid = lax.broadcasted_iota(jnp.int32, (k * k, tl), 0)
    work = cand
    sel = jnp.zeros((k * k, tl), F32)
    for _ in range(k):
        m = jnp.max(work, axis=0, keepdims=True)
        pick = jnp.min(jnp.where(work == m, cid, k * k), axis=0, keepdims=True)
        hit = cid == pick
        sel = jnp.where(hit, 1.0, sel)
        work = jnp.where(hit, NEG_INF, work)
    zsum = jnp.sum(sel * jnp.exp(cand - cand[0:1, :]), axis=0, keepdims=True)

    cnt = jnp.zeros((N_KEYS, tl), F32)
    for r in range(k):
        cnt_r = jnp.sum(sel[r * k:(r + 1) * k, :], axis=0, keepdims=True)
        cnt = jnp.where(rank_a == r, cnt_r, cnt)
    rankb_ref[0] = rank_b.astype(F32)
    eb_ref[0] = jnp.exp(bsc - bv[0:1, :])
    cnt_ref[0] = cnt
    ea_ref[0] = jnp.exp(a - av[0:1, :]) / zsum


def _router(scores):
    nk, n, t = scores.shape
    tl = TL_ROUTER
    spec = pl.BlockSpec((1, n, tl), lambda h, j: (h, 0, j))
    shape = jax.ShapeDtypeStruct((PK_HEADS, n, t), F32)
    return pl.pallas_call(
        _router_body,
        grid=(PK_HEADS, t // tl),
        in_specs=[pl.BlockSpec((2, n, tl), lambda h, j: (h, 0, j))],
        out_specs=[spec, spec, spec, spec],
        out_shape=[shape, shape, shape, shape],
        compiler_params=_cparams(("parallel", "parallel")),
        name="router",
    )(scores)


def _peer_body(hnt_ref, u_ref, vt_ref, rankb_ref, eb_ref, cnt_ref, ea_ref, o_ref, acc_ref, act_ref):
    j = pl.program_id(1)

    @pl.when(j == 0)
    def _():
        acc_ref[...] = jnp.zeros(acc_ref.shape, F32)

    pre = jnp.dot(u_ref[...], hnt_ref[...], preferred_element_type=F32)
    for il in range(NI_PEER):
        rs = slice(il * N_KEYS, (il + 1) * N_KEYS)
        gate = jnp.zeros((N_KEYS, TM_PEER), F32)
        for h in range(PK_HEADS):
            picked = rankb_ref[h] < cnt_ref[h, il:il + 1, :]
            gate = gate + jnp.where(picked, eb_ref[h], 0.0) * ea_ref[h, il:il + 1, :]
        act_ref[rs, :] = (_gelu(pre[rs, :]) * gate).astype(BF16)
    acc_ref[...] += jnp.dot(vt_ref[...], act_ref[...], preferred_element_type=F32)

    @pl.when(j == pl.num_programs(1) - 1)
    def _():
        o_ref[...] = acc_ref[...].T


def _peer(hnt, u_bf, vt_bf, rankb, eb, cnt, ea):
    d, t = hnt.shape
    ne = u_bf.shape[0]
    tm, te = TM_PEER, TE_PEER
    key_spec = pl.BlockSpec((PK_HEADS, N_KEYS, tm), lambda i, j: (0, 0, i))
    blk_spec = pl.BlockSpec((PK_HEADS, NI_PEER, tm), lambda i, j: (0, j, i))
    return pl.pallas_call(
        _peer_body,
        grid=(t // tm, ne // te),
        in_specs=[pl.BlockSpec((d, tm), lambda i, j: (0, i)),
                  pl.BlockSpec((te, d), lambda i, j: (j, 0)),
                  pl.BlockSpec((d, te), lambda i, j: (0, j)),
                  key_spec, key_spec, blk_spec, blk_spec],
        out_specs=pl.BlockSpec((tm, d), lambda i, j: (i, 0)),
        out_shape=jax.ShapeDtypeStruct((t, d), F32),
        scratch_shapes=[pltpu.VMEM((d, tm), F32), pltpu.VMEM((te, tm), BF16)],
        compiler_params=_cparams(("parallel", "arbitrary")),
        name="peer",
    )(hnt, u_bf, vt_bf, rankb, eb, cnt, ea)


def _ple_body(h_ref, po_ref, p_ref, nple_ref, wg_ref, wp_ref, nfin_ref, o_ref):
    h2 = h_ref[...] + po_ref[...]
    gate = jax.nn.sigmoid(
        jnp.dot(_rms(h2, nple_ref[...]).astype(BF16), wg_ref[...], preferred_element_type=F32))
    ple = jnp.dot(p_ref[...].astype(BF16), wp_ref[...], preferred_element_type=F32)
    o_ref[...] = _rms(h2 + gate * ple, nfin_ref[...])


def _ple(h1, peer_out, p, nple, wg, wp, nfin):
    t, d = h1.shape
    tm = TM_DENSE
    full = lambda shape: pl.BlockSpec(shape, lambda i: (0, 0))
    return pl.pallas_call(
        _ple_body,
        grid=(t // tm,),
        in_specs=[pl.BlockSpec((tm, d), lambda i: (i, 0)),
                  pl.BlockSpec((tm, d), lambda i: (i, 0)),
                  pl.BlockSpec((tm, PLE_DIM), lambda i: (i, 0)),
                  full((1, d)), full((d, d)), full((PLE_DIM, d)), full((1, d))],
        out_specs=pl.BlockSpec((tm, d), lambda i: (i, 0)),
        out_shape=jax.ShapeDtypeStruct((t, d), F32),
        compiler_params=_cparams(("parallel",)),
        name="ple",
    )(h1, peer_out, p, nple, wg, wp, nfin)


def _lane_row(vec, offset):
    return jnp.zeros((1, LANE), F32).at[0, offset:offset + vec.shape[0]].set(vec.astype(F32))


def _head_selector(offset):
    sel = np.zeros((LANE, DN_V), np.float32)
    for h in range(DN_HEADS):
        sel[offset + h, h * DN_DV:(h + 1) * DN_DV] = 1.0
    return jnp.asarray(sel)


def kernel(x, p, positions, norm_mix, w_in, conv_w, a_log, dt_bias, dn_norm, ret_gn_w, ret_gn_b,
           w_out, norm_ffn, w_pq, sub_keys, expert_u, expert_v, norm_ple, w_ple_gate, w_ple_proj,
           norm_final):
    b, s, d = x.shape
    t = b * s
    depth = w_in.shape[0]
    assert depth == 1, "the final rms_norm is fused into the single layer's ple kernel"
    half = RET_DK // 2
    inv_freq = ROPE_BASE ** (-jnp.arange(half, dtype=F32) / half)
    freq_row = jnp.tile(inv_freq, LANE // half).reshape(1, LANE)
    pos = positions.reshape(t, 1)
    selg, selb = _head_selector(0), _head_selector(DN_HEADS)

    h = x.reshape(t, d)
    for i in range(depth):
        w = w_in[i]
        w_main = jnp.concatenate([w[:, :AB_OFF], w[:, AB_OFF + 2 * DN_HEADS:]], axis=1).astype(BF16)
        w_ab = jnp.pad(w[:, AB_OFF:AB_OFF + 2 * DN_HEADS],
                       ((0, 0), (0, LANE - 2 * DN_HEADS))).astype(BF16)
        proj, proj_ab = _inproj(h, norm_mix[i].reshape(1, d), w_main, w_ab)

        qg, kd, u, wv, attn, gend = _dn_local(
            proj, proj_ab, conv_w[i].astype(F32).T, _lane_row(a_log[i], 0), _lane_row(dt_bias[i], 0),
            selg, selb, b, s)
        o_dn = _dn_scan(qg, kd, u, wv, attn, gend, b, s)
        o_r = _ret(proj, pos, freq_row, b, s)
        h1 = _outproj(o_dn, proj, o_r, h, dn_norm[i].reshape(1, DN_DV), ret_gn_w[i].reshape(1, RET_V),
                      ret_gn_b[i].reshape(1, RET_V), w_out[i].astype(BF16))

        keys = sub_keys[i].reshape(2 * PK_HEADS, N_KEYS, PK_DQ // 2).astype(BF16)
        hnt, scores = _pq(h1, norm_ffn[i].reshape(1, d), w_pq[i].astype(BF16), keys)
        rankb, eb, cnt, ea = _router(scores)
        peer_out = _peer(hnt, expert_u[i].astype(BF16), expert_v[i].T.astype(BF16), rankb, eb, cnt, ea)

        h = _ple(h1, peer_out, p[i].reshape(t, PLE_DIM), norm_ple[i].reshape(1, d),
                 w_ple_gate[i].astype(BF16), w_ple_proj[i].astype(BF16), norm_final.reshape(1, d))
    return h.reshape(b, s, d)
```

```python
import functools
import math

import numpy as np
import jax
import jax.numpy as jnp
from jax import lax
from jax.experimental import pallas as pl
from jax.experimental.pallas import tpu as pltpu

F32 = jnp.float32
BF16 = jnp.bfloat16
HIGHEST = lax.Precision.HIGHEST
NEG_INF = float("-inf")

EPS = 1e-6
D_MODEL = 2048
DN_HEADS = 8
DN_DK = 128
DN_DV = 128
CONV_W = 4
RET_HEADS = 8
RET_DK = 64
RET_DV = 128
CHUNK = 64
ROPE_BASE = 10000.0
N_KEYS = 128
PK_HEADS = 8
PK_DQ = 256
PK_TOPK = 16
PLE_DIM = 256

DN_QK = DN_HEADS * DN_DK
DN_V = DN_HEADS * DN_DV
CONV_CH = 2 * DN_QK + DN_V
RET_QK = RET_HEADS * RET_DK
RET_V = RET_HEADS * RET_DV
AB_OFF = CONV_CH + DN_V
MAIN_COLS = CONV_CH + DN_V + 2 * RET_QK + 2 * RET_V
LANE = 128
HALO = 8

TM_INPROJ = 512
TN_INPROJ = 1024
TT_SEQ = 256
TM_DENSE = 256
TL_ROUTER = 256
TM_PEER = 512
NI_PEER = 8
TE_PEER = NI_PEER * N_KEYS
VMEM_LIMIT = 56 * 1024 * 1024


def _cparams(sem, flags=None):
    return pltpu.CompilerParams(dimension_semantics=sem, vmem_limit_bytes=VMEM_LIMIT, flags=flags)


def _rms(x, w):
    return x * lax.rsqrt(jnp.mean(x * x, axis=-1, keepdims=True) + EPS) * w


def _silu(x):
    return x * jax.nn.sigmoid(x)


def _softplus(x):
    return jnp.maximum(x, 0.0) + jnp.log1p(jnp.exp(-jnp.abs(x)))


def _gelu(x):
    return 0.5 * x * (1.0 + lax.erf(x * (2.0 ** -0.5)))


def _mm(a, b):
    return jnp.dot(a.astype(BF16), b.astype(BF16), preferred_element_type=F32)


def _mm_nt(a, b):
    return lax.dot_general(a.astype(BF16), b.astype(BF16), (((1,), (1,)), ((), ())),
                           preferred_element_type=F32)


def _mm_tn(a, b):
    return lax.dot_general(a.astype(BF16), b.astype(BF16), (((0,), (0,)), ((), ())),
                           preferred_element_type=F32)


def _mm_exact(a, b):
    return jnp.dot(a, b, precision=HIGHEST, preferred_element_type=F32)


def _inproj_body(x_ref, nw_ref, w_ref, wab_ref, o_ref, oab_ref, hn_ref):
    @pl.when(pl.program_id(1) == 0)
    def _():
        hn_ref[...] = _rms(x_ref[...], nw_ref[...]).astype(BF16)
        oab_ref[...] = jnp.dot(hn_ref[...], wab_ref[...], preferred_element_type=F32)

    o_ref[...] = jnp.dot(hn_ref[...], w_ref[...], preferred_element_type=F32)


def _inproj(x, nw, w_main, w_ab):
    t, d = x.shape
    n = w_main.shape[1]
    return pl.pallas_call(
        _inproj_body,
        grid=(t // TM_INPROJ, n // TN_INPROJ),
        in_specs=[
            pl.BlockSpec((TM_INPROJ, d), lambda i, j: (i, 0)),
            pl.BlockSpec((1, d), lambda i, j: (0, 0)),
            pl.BlockSpec((d, TN_INPROJ), lambda i, j: (0, j)),
            pl.BlockSpec((d, LANE), lambda i, j: (0, 0)),
        ],
        out_specs=[
            pl.BlockSpec((TM_INPROJ, TN_INPROJ), lambda i, j: (i, j)),
            pl.BlockSpec((TM_INPROJ, LANE), lambda i, j: (i, 0)),
        ],
        out_shape=[jax.ShapeDtypeStruct((t, n), F32), jax.ShapeDtypeStruct((t, LANE), F32)],
        scratch_shapes=[pltpu.VMEM((TM_INPROJ, d), BF16)],
        compiler_params=_cparams(("parallel", "arbitrary")),
        name="inproj",
    )(x, nw, w_main, w_ab)


def _dn_local_body(q_ref, k_ref, v_ref, qh_ref, kh_ref, vh_ref, ab_ref, cw_ref, alog_ref, dtb_ref,
                   selg_ref, selb_ref, qg_ref, kd_ref, u_ref, w_ref, attn_ref, gend_ref, buf_ref):
    tt = TT_SEQ
    first_tile = pl.program_id(1) == 0

    def conv_silu(main_ref, halo_ref, c0):
        halo = halo_ref[...]
        buf_ref[0:HALO, :] = jnp.where(first_tile, jnp.zeros_like(halo), halo)
        buf_ref[HALO:HALO + tt, :] = main_ref[...]
        y = None
        for j in range(CONV_W):
            off = HALO - (CONV_W - 1) + j
            term = buf_ref[off:off + tt, :] * cw_ref[j:j + 1, c0:c0 + DN_QK]
            y = term if y is None else y + term
        return _silu(y)

    q = conv_silu(q_ref, qh_ref, 0)
    k = conv_silu(k_ref, kh_ref, DN_QK)
    v = conv_silu(v_ref, vh_ref, 2 * DN_QK)

    row = lax.broadcasted_iota(jnp.int32, (tt, tt), 0)
    col = lax.broadcasted_iota(jnp.int32, (tt, tt), 1)

    def same_block(bits):
        return (row >> bits) == (col >> bits)

    blk8, blk16, blk32, blk64 = same_block(3), same_block(4), same_block(5), same_block(6)
    causal = blk64 & (col <= row)
    strict = blk64 & (col < row)
    eye = (row == col).astype(F32)

    ab = ab_ref[...]
    g = -jnp.exp(alog_ref[...]) * _softplus(ab + dtb_ref[...])
    beta = jax.nn.sigmoid(ab)
    gc = _mm_exact(causal.astype(F32), g)
    gtot = _mm_exact(blk64.astype(F32), g)
    e_gc = _mm_exact(gc, selg_ref[...])
    e_gt = _mm_exact(gtot, selg_ref[...])
    e_beta = _mm_exact(beta, selb_ref[...])
    r128 = lax.broadcasted_iota(jnp.int32, (LANE, LANE), 0)
    c128 = lax.broadcasted_iota(jnp.int32, (LANE, LANE), 1)
    gc_t = lax.dot_general((r128 == c128).astype(F32), gc, (((1,), (1,)), ((), ())),
                           precision=HIGHEST, preferred_element_type=F32)
    eg = jnp.exp(e_gc)
    kdec = jnp.exp(e_gt - e_gc)

    for c in range(tt // CHUNK):
        gend_ref[0, c:c + 1, :] = jnp.exp(e_gt[c * CHUNK:c * CHUNK + 1, :])
    gend_ref[0, tt // CHUNK:, :] = jnp.zeros((HALO - tt // CHUNK, DN_V), F32)

    for h in range(DN_HEADS):
        hs = slice(h * DN_DK, (h + 1) * DN_DK)
        qh = q[:, hs]
        qh = qh * lax.rsqrt(jnp.sum(qh * qh, axis=-1, keepdims=True) + EPS) * (DN_DK ** -0.5)
        kh = k[:, hs]
        kh = kh * lax.rsqrt(jnp.sum(kh * kh, axis=-1, keepdims=True) + EPS)
        vh = v[:, hs]
        beta_h = e_beta[:, hs]
        kb = kh * beta_h
        gcol = e_gc[:, hs]
        diff = jnp.concatenate([gcol] * (tt // LANE), axis=1) - gc_t[h:h + 1, :]
        dmat = jnp.exp(jnp.where(causal, diff, NEG_INF))
        khb = kh.astype(BF16)
        nmat = jnp.where(strict, _mm_nt(kb, khb) * dmat, 0.0)
        attn_ref[0, h] = (_mm_nt(qh, khb) * dmat).astype(BF16)

        n0 = jnp.where(blk8, nmat, 0.0)
        n2 = _mm(n0, n0)
        n4 = _mm(n2, n2)
        inv = eye - n0
        inv = inv + _mm(inv, n2)
        inv = inv + _mm(inv, n4)
        for inner, outer in ((blk8, blk16), (blk16, blk32), (blk32, blk64)):
            off_diag = jnp.where(outer & jnp.logical_not(inner), nmat, 0.0)
            inv = inv - _mm(_mm(inv, off_diag), inv)

        rhs = jnp.concatenate([vh * beta_h, kb * eg[:, hs]], axis=1)
        sol = _mm(inv, rhs)
        u_ref[:, hs] = sol[:, :DN_DV]
        w_ref[:, hs] = sol[:, DN_DV:].astype(BF16)
        qg_ref[:, hs] = (qh * eg[:, hs]).astype(BF16)
        kd_ref[:, hs] = (kh * kdec[:, hs]).astype(BF16)


def _dn_local(proj, proj_ab, cw, alog_row, dtb_row, selg, selb, b, s):
    t = b * s
    nt = s // TT_SEQ
    rows_per_halo = TT_SEQ // HALO

    def main_spec(cb):
        return pl.BlockSpec((TT_SEQ, DN_QK), lambda bi, i: (bi * nt + i, cb))

    def halo_spec(cb):
        return pl.BlockSpec(
            (HALO, DN_QK), lambda bi, i: (jnp.maximum((bi * nt + i) * rows_per_halo - 1, 0), cb))

    def full(shape):
        return pl.BlockSpec(shape, lambda bi, i: (0,) * len(shape))

    tok_spec = pl.BlockSpec((TT_SEQ, DN_V), lambda bi, i: (bi * nt + i, 0))
    return pl.pallas_call(
        _dn_local_body,
        grid=(b, nt),
        in_specs=[main_spec(0), main_spec(1), main_spec(2), halo_spec(0), halo_spec(1), halo_spec(2),
                  pl.BlockSpec((TT_SEQ, LANE), lambda bi, i: (bi * nt + i, 0)),
                  full((CONV_W, CONV_CH)), full((1, LANE)), full((1, LANE)),
                  full((LANE, DN_V)), full((LANE, DN_V))],
        out_specs=[tok_spec, tok_spec, tok_spec, tok_spec,
                   pl.BlockSpec((1, DN_HEADS, TT_SEQ, TT_SEQ), lambda bi, i: (bi * nt + i, 0, 0, 0)),
                   pl.BlockSpec((1, HALO, DN_V), lambda bi, i: (bi * nt + i, 0, 0))],
        out_shape=[jax.ShapeDtypeStruct((t, DN_V), BF16),
                   jax.ShapeDtypeStruct((t, DN_V), BF16),
                   jax.ShapeDtypeStruct((t, DN_V), F32),
                   jax.ShapeDtypeStruct((t, DN_V), BF16),
                   jax.ShapeDtypeStruct((t // TT_SEQ, DN_HEADS, TT_SEQ, TT_SEQ), BF16),
                   jax.ShapeDtypeStruct((t // TT_SEQ, HALO, DN_V), F32)],
        scratch_shapes=[pltpu.VMEM((HALO + TT_SEQ, DN_QK), F32)],
        compiler_params=_cparams(("parallel", "parallel")),
        name="dn_local",
    )(proj, proj, proj, proj, proj, proj, proj_ab, cw, alog_row, dtb_row, selg, selb)


def _dn_scan_body(qg_ref, kd_ref, u_ref, w_ref, attn_ref, gend_ref, o_ref, s_ref, vn_ref):
    nb = qg_ref.shape[0]

    @pl.when(pl.program_id(0) == 0)
    def _():
        s_ref[...] = jnp.zeros(s_ref.shape, F32)

    for b in range(nb):
        for h in range(DN_HEADS):
            ch = b * DN_HEADS + h
            hs = slice(h * DN_DK, (h + 1) * DN_DK)
            state = s_ref[ch]
            vn_ref[ch] = jnp.zeros((TT_SEQ, DN_DV), BF16)
            for c in range(TT_SEQ // CHUNK):
                rs = slice(c * CHUNK, (c + 1) * CHUNK)
                sb = state.astype(BF16)
                v_new = u_ref[b, rs, hs] - jnp.dot(w_ref[b, rs, hs], sb, preferred_element_type=F32)
                vnb = v_new.astype(BF16)
                vn_ref[ch, rs, :] = vnb
                o_ref[b, rs, hs] = (
                    jnp.dot(qg_ref[b, rs, hs], sb, preferred_element_type=F32)
                    + jnp.dot(attn_ref[b, 0, h, rs, :], vn_ref[ch], preferred_element_type=F32))
                state = state * gend_ref[b, 0, c:c + 1, hs] + _mm_tn(kd_ref[b, rs, hs], vnb)
            s_ref[ch] = state


def _dn_scan(qg, kd, u, w, attn, gend, b, s):
    nt = s // TT_SEQ
    tok = lambda a: a.reshape(b, s, DN_V)
    tok_spec = pl.BlockSpec((b, TT_SEQ, DN_V), lambda i: (0, i, 0))
    out = pl.pallas_call(
        _dn_scan_body,
        grid=(nt,),
        in_specs=[tok_spec, tok_spec, tok_spec, tok_spec,
                  pl.BlockSpec((b, 1, DN_HEADS, TT_SEQ, TT_SEQ), lambda i: (0, i, 0, 0, 0)),
                  pl.BlockSpec((b, 1, HALO, DN_V), lambda i: (0, i, 0, 0))],
        out_specs=tok_spec,
        out_shape=jax.ShapeDtypeStruct((b, s, DN_V), F32),
        scratch_shapes=[pltpu.VMEM((b * DN_HEADS, DN_DK, DN_DV), F32),
                        pltpu.VMEM((b * DN_HEADS, TT_SEQ, DN_DV), BF16)],
        compiler_params=_cparams(("arbitrary",)),
        name="dn_scan",
    )(tok(qg), tok(kd), tok(u), tok(w),
      attn.reshape(b, nt, DN_HEADS, TT_SEQ, TT_SEQ), gend.reshape(b, nt, HALO, DN_V))
    return out.reshape(b * s, DN_V)


def _log_gamma(h):
    return math.log1p(-(2.0 ** (-5.0 - h)))


def _ret_body(qk_ref, v_ref, pos_ref, freq_ref, o_ref, s_ref):
    tt = TT_SEQ

    @pl.when(pl.program_id(1) == 0)
    def _():
        s_ref[...] = jnp.zeros(s_ref.shape, F32)

    ang = pos_ref[...].astype(F32) * freq_ref[...]
    lane = lax.broadcasted_iota(jnp.int32, (tt, LANE), 1)
    first_half = (lane & (RET_DK // 2)) == 0
    cos = jnp.cos(ang)
    sin = jnp.sin(ang)
    ssin = jnp.where(first_half, -sin, sin)

    def rotary(x):
        swapped = jnp.where(first_half, pltpu.roll(x, LANE - RET_DK // 2, axis=1),
                            pltpu.roll(x, RET_DK // 2, axis=1))
        return x * cos + swapped * ssin

    row = lax.broadcasted_iota(jnp.int32, (tt, tt), 0)
    col = lax.broadcasted_iota(jnp.int32, (tt, tt), 1)
    rel = (row - col).astype(F32)
    causal = row >= col
    trow = lax.broadcasted_iota(jnp.int32, (tt, LANE), 0).astype(F32)
    srow = lax.broadcasted_iota(jnp.int32, (LANE, LANE), 0)

    for pr in range(RET_HEADS // 2):
        ps = slice(pr * LANE, (pr + 1) * LANE)
        qp = rotary(qk_ref[:, ps])
        kp = rotary(qk_ref[:, RET_QK + pr * LANE:RET_QK + (pr + 1) * LANE]) * (RET_DK ** -0.5)
        kpb = kp.astype(BF16)
        state = s_ref[pr]
        sb = state.astype(BF16)
        update = jnp.zeros((LANE, RET_DV), F32)
        for hh in range(2):
            h = 2 * pr + hh
            lg = _log_gamma(h)
            mine = (lane < RET_DK) if hh == 0 else (lane >= RET_DK)
            qm = jnp.where(mine, qp, 0.0).astype(BF16)
            vh = v_ref[:, h * RET_DV:(h + 1) * RET_DV].astype(BF16)
            att = _mm_nt(qm, kpb) * jnp.exp(jnp.where(causal, rel * lg, NEG_INF))
            inner = jnp.dot(att.astype(BF16), vh, preferred_element_type=F32)
            cross = jnp.dot(qm, sb, preferred_element_type=F32) * jnp.exp((trow + 1.0) * lg)
            o_ref[:, h * RET_DV:(h + 1) * RET_DV] = inner + cross
            kz = jnp.where(mine, kp, 0.0) * jnp.exp((tt - 1.0 - trow) * lg)
            update = update + _mm_tn(kz, vh)
        decay = jnp.where(srow < RET_DK, math.exp(tt * _log_gamma(2 * pr)),
                          math.exp(tt * _log_gamma(2 * pr + 1)))
        s_ref[pr] = state * decay + update


def _ret(proj, pos, freq_row, b, s):
    t = b * s
    nt = s // TT_SEQ
    return pl.pallas_call(
        _ret_body,
        grid=(b, nt),
        in_specs=[pl.BlockSpec((TT_SEQ, 2 * RET_QK), lambda bi, i: (bi * nt + i, 4)),
                  pl.BlockSpec((TT_SEQ, RET_V), lambda bi, i: (bi * nt + i, 5)),
                  pl.BlockSpec((TT_SEQ, 1), lambda bi, i: (bi * nt + i, 0)),
                  pl.BlockSpec((1, LANE), lambda bi, i: (0, 0))],
        out_specs=pl.BlockSpec((TT_SEQ, RET_V), lambda bi, i: (bi * nt + i, 0)),
        out_shape=jax.ShapeDtypeStruct((t, RET_V), F32),
        scratch_shapes=[pltpu.VMEM((RET_HEADS // 2, LANE, RET_DV), F32)],
        compiler_params=_cparams(("parallel", "arbitrary")),
        name="ret",
    )(proj, proj, pos, freq_row)


def _outproj_body(odn_ref, z_ref, or_ref, rg_ref, x_ref, dnw_ref, gnw_ref, gnb_ref, w_ref, o_ref,
                  mix_ref):
    for h in range(DN_HEADS):
        hs = slice(h * DN_DV, (h + 1) * DN_DV)
        mix_ref[:, hs] = (_rms(odn_ref[:, hs], dnw_ref[...]) * _silu(z_ref[:, hs])).astype(BF16)
    for h in range(RET_HEADS):
        hs = slice(h * RET_DV, (h + 1) * RET_DV)
        o = or_ref[:, hs]
        cen = o - jnp.mean(o, axis=-1, keepdims=True)
        y = cen * lax.rsqrt(jnp.mean(cen * cen, axis=-1, keepdims=True) + EPS)
        y = (y * gnw_ref[:, hs] + gnb_ref[:, hs]) * _silu(rg_ref[:, hs])
        mix_ref[:, DN_V + h * RET_DV:DN_V + (h + 1) * RET_DV] = y.astype(BF16)
    o_ref[...] = x_ref[...] + jnp.dot(mix_ref[...], w_ref[...], preferred_element_type=F32)


def _outproj(o_dn, proj, o_r, x, dnw, gnw, gnb, w_out):
    t, d = x.shape
    tm = TM_DENSE
    full = lambda shape: pl.BlockSpec(shape, lambda i: (0, 0))
    return pl.pallas_call(
        _outproj_body,
        grid=(t // tm,),
        in_specs=[pl.BlockSpec((tm, DN_V), lambda i: (i, 0)),
                  pl.BlockSpec((tm, DN_V), lambda i: (i, 3)),
                  pl.BlockSpec((tm, RET_V), lambda i: (i, 0)),
                  pl.BlockSpec((tm, RET_V), lambda i: (i, 6)),
                  pl.BlockSpec((tm, d), lambda i: (i, 0)),
                  full((1, DN_DV)), full((1, RET_V)), full((1, RET_V)), full((DN_V + RET_V, d))],
        out_specs=pl.BlockSpec((tm, d), lambda i: (i, 0)),
        out_shape=jax.ShapeDtypeStruct((t, d), F32),
        scratch_shapes=[pltpu.VMEM((tm, DN_V + RET_V), BF16)],
        compiler_params=_cparams(("parallel",)),
        name="outproj",
    )(o_dn, proj, o_r, proj, x, dnw, gnw, gnb, w_out)


def _pq_body(h_ref, nw_ref, w_ref, keys_ref, hnt_ref, sc_ref):
    hn = _rms(h_ref[...], nw_ref[...])
    hnt_ref[...] = hn.T.astype(BF16)
    q = jnp.dot(hn.astype(BF16), w_ref[...], preferred_element_type=F32)
    half = PK_DQ // 2
    for hp in range(2 * PK_HEADS):
        sc_ref[hp] = _mm_nt(keys_ref[hp], q[:, hp * half:(hp + 1) * half])


def _pq(h1, nw, w_pq, keys):
    t, d = h1.shape
    tm = TM_DENSE
    nk = 2 * PK_HEADS
    return pl.pallas_call(
        _pq_body,
        grid=(t // tm,),
        in_specs=[pl.BlockSpec((tm, d), lambda i: (i, 0)),
                  pl.BlockSpec((1, d), lambda i: (0, 0)),
                  pl.BlockSpec((d, PK_HEADS * PK_DQ), lambda i: (0, 0)),
                  pl.BlockSpec((nk, N_KEYS, PK_DQ // 2), lambda i: (0, 0, 0))],
        out_specs=[pl.BlockSpec((d, tm), lambda i: (0, i)),
                   pl.BlockSpec((nk, N_KEYS, tm), lambda i: (0, 0, i))],
        out_shape=[jax.ShapeDtypeStruct((d, t), BF16),
                   jax.ShapeDtypeStruct((nk, N_KEYS, t), F32)],
        compiler_params=_cparams(("parallel",)),
        name="pq",
    )(h1, nw, w_pq, keys)


def _router_body(sc_ref, rankb_ref, eb_ref, cnt_ref, ea_ref):
    tl = TL_ROUTER
    k = PK_TOPK
    keyid = lax.broadcasted_iota(jnp.int32, (N_KEYS, tl), 0)
    slot = lax.broadcasted_iota(jnp.int32, (k, tl), 0)

    def top_k(s):
        rank = jnp.full((N_KEYS, tl), k, jnp.int32)
        vals = jnp.zeros((k, tl), F32)
        for r in range(k):
            m = jnp.max(s, axis=0, keepdims=True)
            pick = jnp.min(jnp.where(s == m, keyid, N_KEYS), axis=0, keepdims=True)
            hit = keyid == pick
            rank = jnp.where(hit, r, rank)
            s = jnp.where(hit, NEG_INF, s)
            vals = jnp.where(slot == r, m, vals)
        return vals, rank

    a = sc_ref[0]
    bsc = sc_ref[1]
    av, rank_a = top_k(a)
    bv, rank_b = top_k(bsc)

    k2 = k // 2
    n_cand = k + (k - 1) * k2
    cand = jnp.concatenate([av[0:1, :] + bv] + [av[r:r + 1, :] + bv[0:k2, :] for r in range(1, k)],
                           axis=0)
    row = lax.broadcasted_iota(jnp.int32, (n_cand, tl), 0)
    tail = row - k
    assert k2 & (k2 - 1) == 0
    cid = jnp.where(row < k, row,
                    (1 + (tail >> (k2.bit_length() - 1))) * k + (tail & (k2 - 1)))
    work = cand
    sel = jnp.zeros((n_cand, tl), F32)
    for _ in range(k):
        m = jnp.max(work, axis=0, keepdims=True)
        pick = jnp.min(jnp.where(work == m, cid, k * k), axis=0, keepdims=True)
        hit = cid == pick
        sel = jnp.where(hit, 1.0, sel)
        work = jnp.where(hit, NEG_INF, work)
    zsum = jnp.sum(sel * jnp.exp(cand - cand[0:1, :]), axis=0, keepdims=True)

    cnt = jnp.zeros((N_KEYS, tl), F32)
    for r in range(k):
        lo, hi = (0, k) if r == 0 else (k + (r - 1) * k2, k + r * k2)
        cnt_r = jnp.sum(sel[lo:hi, :], axis=0, keepdims=True)
        cnt = jnp.where(rank_a == r, cnt_r, cnt)
    rankb_ref[0] = rank_b.astype(BF16)
    eb_ref[0] = jnp.exp(bsc - bv[0:1, :]).astype(BF16)
    cnt_ref[0] = cnt
    ea_ref[0] = jnp.exp(a - av[0:1, :]) / zsum


def _router(scores):
    nk, n, t = scores.shape
    tl = TL_ROUTER
    spec = pl.BlockSpec((1, n, tl), lambda h, j: (h, 0, j))
    shape = lambda dt: jax.ShapeDtypeStruct((PK_HEADS, n, t), dt)
    return pl.pallas_call(
        _router_body,
        grid=(PK_HEADS, t // tl),
        in_specs=[pl.BlockSpec((2, n, tl), lambda h, j: (h, 0, j))],
        out_specs=[spec, spec, spec, spec],
        out_shape=[shape(BF16), shape(BF16), shape(F32), shape(F32)],
        compiler_params=_cparams(("parallel", "parallel")),
        name="router",
    )(scores)


def _peer_body(hnt_ref, u_ref, vt_ref, rankb_ref, eb_ref, cnt_ref, ea_ref, o_ref, acc_ref, act_ref):
    j = pl.program_id(1)

    @pl.when(j == 0)
    def _():
        acc_ref[...] = jnp.zeros(acc_ref.shape, F32)

    pre = jnp.dot(u_ref[...], hnt_ref[...], preferred_element_type=F32)
    for il in range(NI_PEER):
        rs = slice(il * N_KEYS, (il + 1) * N_KEYS)
        gate = jnp.zeros((N_KEYS, TM_PEER), BF16)
        for h in range(PK_HEADS):
            picked = rankb_ref[h] < cnt_ref[h, il:il + 1, :].astype(BF16)
            gate = gate + jnp.where(picked, eb_ref[h], 0.0) * ea_ref[h, il:il + 1, :].astype(BF16)
        act_ref[rs, :] = _gelu(pre[rs, :]).astype(BF16) * gate
    acc_ref[...] += jnp.dot(vt_ref[...], act_ref[...], preferred_element_type=F32)

    @pl.when(j == pl.num_programs(1) - 1)
    def _():
        o_ref[...] = acc_ref[...].T


def _peer(hnt, u_bf, vt_bf, rankb, eb, cnt, ea):
    d, t = hnt.shape
    ne = u_bf.shape[0]
    tm, te = TM_PEER, TE_PEER
    key_spec = pl.BlockSpec((PK_HEADS, N_KEYS, tm), lambda i, j: (0, 0, i))
    blk_spec = pl.BlockSpec((PK_HEADS, NI_PEER, tm), lambda i, j: (0, j, i))
    return pl.pallas_call(
        _peer_body,
        grid=(t // tm, ne // te),
        in_specs=[pl.BlockSpec((d, tm), lambda i, j: (0, i)),
                  pl.BlockSpec((te, d), lambda i, j: (j, 0)),
                  pl.BlockSpec((d, te), lambda i, j: (0, j)),
                  key_spec, key_spec, blk_spec, blk_spec],
        out_specs=pl.BlockSpec((tm, d), lambda i, j: (i, 0)),
        out_shape=jax.ShapeDtypeStruct((t, d), F32),
        scratch_shapes=[pltpu.VMEM((d, tm), F32), pltpu.VMEM((te, tm), BF16)],
        compiler_params=_cparams(("parallel", "arbitrary")),
        name="peer",
    )(hnt, u_bf, vt_bf, rankb, eb, cnt, ea)


def _ple_body(h_ref, po_ref, p_ref, nple_ref, wg_ref, wp_ref, nfin_ref, o_ref):
    h2 = h_ref[...] + po_ref[...]
    gate = jax.nn.sigmoid(
        jnp.dot(_rms(h2, nple_ref[...]).astype(BF16), wg_ref[...], preferred_element_type=F32))
    ple = jnp.dot(p_ref[...].astype(BF16), wp_ref[...], preferred_element_type=F32)
    o_ref[...] = _rms(h2 + gate * ple, nfin_ref[...])


def _ple(h1, peer_out, p, nple, wg, wp, nfin):
    t, d = h1.shape
    tm = TM_DENSE
    full = lambda shape: pl.BlockSpec(shape, lambda i: (0, 0))
    return pl.pallas_call(
        _ple_body,
        grid=(t // tm,),
        in_specs=[pl.BlockSpec((tm, d), lambda i: (i, 0)),
                  pl.BlockSpec((tm, d), lambda i: (i, 0)),
                  pl.BlockSpec((tm, PLE_DIM), lambda i: (i, 0)),
                  full((1, d)), full((d, d)), full((PLE_DIM, d)), full((1, d))],
        out_specs=pl.BlockSpec((tm, d), lambda i: (i, 0)),
        out_shape=jax.ShapeDtypeStruct((t, d), F32),
        compiler_params=_cparams(("parallel",)),
        name="ple",
    )(h1, peer_out, p, nple, wg, wp, nfin)


def _lane_row(vec, offset):
    return jnp.zeros((1, LANE), F32).at[0, offset:offset + vec.shape[0]].set(vec.astype(F32))


def _head_selector(offset):
    sel = np.zeros((LANE, DN_V), np.float32)
    for h in range(DN_HEADS):
        sel[offset + h, h * DN_DV:(h + 1) * DN_DV] = 1.0
    return jnp.asarray(sel)


def kernel(x, p, positions, norm_mix, w_in, conv_w, a_log, dt_bias, dn_norm, ret_gn_w, ret_gn_b,
           w_out, norm_ffn, w_pq, sub_keys, expert_u, expert_v, norm_ple, w_ple_gate, w_ple_proj,
           norm_final):
    b, s, d = x.shape
    t = b * s
    depth = w_in.shape[0]
    assert depth == 1, "the final rms_norm is fused into the single layer's ple kernel"
    half = RET_DK // 2
    inv_freq = ROPE_BASE ** (-jnp.arange(half, dtype=F32) / half)
    freq_row = jnp.tile(inv_freq, LANE // half).reshape(1, LANE)
    pos = positions.reshape(t, 1)
    selg, selb = _head_selector(0), _head_selector(DN_HEADS)

    h = x.reshape(t, d)
    for i in range(depth):
        w = w_in[i]
        w_main = jnp.concatenate([w[:, :AB_OFF], w[:, AB_OFF + 2 * DN_HEADS:]], axis=1).astype(BF16)
        w_ab = jnp.pad(w[:, AB_OFF:AB_OFF + 2 * DN_HEADS],
                       ((0, 0), (0, LANE - 2 * DN_HEADS))).astype(BF16)
        proj, proj_ab = _inproj(h, norm_mix[i].reshape(1, d), w_main, w_ab)

        qg, kd, u, wv, attn, gend = _dn_local(
            proj, proj_ab, conv_w[i].astype(F32).T, _lane_row(a_log[i], 0), _lane_row(dt_bias[i], 0),
            selg, selb, b, s)
        o_dn = _dn_scan(qg, kd, u, wv, attn, gend, b, s)
        o_r = _ret(proj, pos, freq_row, b, s)
        h1 = _outproj(o_dn, proj, o_r, h, dn_norm[i].reshape(1, DN_DV), ret_gn_w[i].reshape(1, RET_V),
                      ret_gn_b[i].reshape(1, RET_V), w_out[i].astype(BF16))

        keys = sub_keys[i].reshape(2 * PK_HEADS, N_KEYS, PK_DQ // 2).astype(BF16)
        hnt, scores = _pq(h1, norm_ffn[i].reshape(1, d), w_pq[i].astype(BF16), keys)
        rankb, eb, cnt, ea = _router(scores)
        peer_out = _peer(hnt, expert_u[i].astype(BF16), expert_v[i].T.astype(BF16), rankb, eb, cnt, ea)

        h = _ple(h1, peer_out, p[i].reshape(t, PLE_DIM), norm_ple[i].reshape(1, d),
                 w_ple_gate[i].astype(BF16), w_ple_proj[i].astype(BF16), norm_final.reshape(1, d))
    return h.reshape(b, s, d)
```

```python
import functools
import math

import numpy as np
import jax
import jax.numpy as jnp
from jax import lax
from jax.experimental import pallas as pl
from jax.experimental.pallas import tpu as pltpu

F32 = jnp.float32
BF16 = jnp.bfloat16
NEG_INF = float("-inf")

EPS = 1e-6
D_MODEL = 2048
DN_HEADS = 8
DN_DK = 128
DN_DV = 128
CONV_W = 4
RET_HEADS = 8
RET_DK = 64
RET_DV = 128
CHUNK = 64
ROPE_BASE = 10000.0
N_KEYS = 128
PK_HEADS = 8
PK_DQ = 256
PK_TOPK = 16
PLE_DIM = 256

DN_QK = DN_HEADS * DN_DK
DN_V = DN_HEADS * DN_DV
CONV_CH = 2 * DN_QK + DN_V
RET_QK = RET_HEADS * RET_DK
RET_V = RET_HEADS * RET_DV
AB_OFF = CONV_CH + DN_V
MAIN_COLS = CONV_CH + DN_V + 2 * RET_QK + 2 * RET_V
LANE = 128
HALO = 8

TM_INPROJ = 1024
TN_INPROJ = 512
TT_SEQ = 256
TM_DENSE = 256
TL_ROUTER = 256
TM_PEER = 512
NI_PEER = 8
TE_PEER = NI_PEER * N_KEYS
VMEM_LIMIT = 56 * 1024 * 1024


def _cparams(sem, flags=None):
    return pltpu.CompilerParams(dimension_semantics=sem, vmem_limit_bytes=VMEM_LIMIT, flags=flags)


def _rms(x, w):
    return x * lax.rsqrt(jnp.mean(x * x, axis=-1, keepdims=True) + EPS) * w


def _silu(x):
    return x * jax.nn.sigmoid(x)


def _softplus(x):
    return jnp.maximum(x, 0.0) + jnp.log1p(jnp.exp(-jnp.abs(x)))


def _gelu(x):
    return 0.5 * x * (1.0 + lax.erf(x * (2.0 ** -0.5)))


def _mm(a, b):
    return jnp.dot(a.astype(BF16), b.astype(BF16), preferred_element_type=F32)


def _mm_nt(a, b):
    return lax.dot_general(a.astype(BF16), b.astype(BF16), (((1,), (1,)), ((), ())),
                           preferred_element_type=F32)


def _mm_tn(a, b):
    return lax.dot_general(a.astype(BF16), b.astype(BF16), (((0,), (0,)), ((), ())),
                           preferred_element_type=F32)


def _split3(x):
    hi = x.astype(BF16)
    rest = x - hi.astype(F32)
    mid = rest.astype(BF16)
    lo = (rest - mid.astype(F32)).astype(BF16)
    return hi, mid, lo


def _select_mm(a, b, dims=(((1,), (0,)), ((), ()))):
    if a.dtype == BF16:
        parts = [lax.dot_general(a, p, dims, preferred_element_type=F32) for p in _split3(b)]
    else:
        parts = [lax.dot_general(p, b, dims, preferred_element_type=F32) for p in _split3(a)]
    return parts[0] + parts[1] + parts[2]


def _inproj_body(x_ref, nw_ref, w_ref, wab_ref, o_ref, oab_ref, hn_ref):
    @pl.when(pl.program_id(1) == 0)
    def _():
        hn_ref[...] = _rms(x_ref[...], nw_ref[...]).astype(BF16)
        oab_ref[...] = jnp.dot(hn_ref[...], wab_ref[...], preferred_element_type=F32)

    o_ref[...] = jnp.dot(hn_ref[...], w_ref[...], preferred_element_type=F32)


def _inproj(x, nw, w_main, w_ab):
    t, d = x.shape
    n = w_main.shape[1]
    return pl.pallas_call(
        _inproj_body,
        grid=(t // TM_INPROJ, n // TN_INPROJ),
        in_specs=[
            pl.BlockSpec((TM_INPROJ, d), lambda i, j: (i, 0)),
            pl.BlockSpec((1, d), lambda i, j: (0, 0)),
            pl.BlockSpec((d, TN_INPROJ), lambda i, j: (0, j)),
            pl.BlockSpec((d, LANE), lambda i, j: (0, 0)),
        ],
        out_specs=[
            pl.BlockSpec((TM_INPROJ, TN_INPROJ), lambda i, j: (i, j)),
            pl.BlockSpec((TM_INPROJ, LANE), lambda i, j: (i, 0)),
        ],
        out_shape=[jax.ShapeDtypeStruct((t, n), F32), jax.ShapeDtypeStruct((t, LANE), F32)],
        scratch_shapes=[pltpu.VMEM((TM_INPROJ, d), BF16)],
        compiler_params=_cparams(("parallel", "arbitrary")),
        name="inproj",
    )(x, nw, w_main, w_ab)


def _dn_local_body(q_ref, k_ref, v_ref, qh_ref, kh_ref, vh_ref, ab_ref, cw_ref, alog_ref, dtb_ref,
                   selg_ref, selb_ref, qg_ref, kd_ref, u_ref, w_ref, attn_ref, gend_ref, buf_ref):
    tt = TT_SEQ
    first_tile = pl.program_id(1) == 0

    def conv_silu(main_ref, halo_ref, c0):
        halo = halo_ref[...]
        buf_ref[0:HALO, :] = jnp.where(first_tile, jnp.zeros_like(halo), halo)
        buf_ref[HALO:HALO + tt, :] = main_ref[...]
        y = None
        for j in range(CONV_W):
            off = HALO - (CONV_W - 1) + j
            term = buf_ref[off:off + tt, :] * cw_ref[j:j + 1, c0:c0 + DN_QK]
            y = term if y is None else y + term
        return _silu(y)

    q = conv_silu(q_ref, qh_ref, 0)
    k = conv_silu(k_ref, kh_ref, DN_QK)
    v = conv_silu(v_ref, vh_ref, 2 * DN_QK)

    row = lax.broadcasted_iota(jnp.int32, (tt, tt), 0)
    col = lax.broadcasted_iota(jnp.int32, (tt, tt), 1)

    def same_block(bits):
        return (row >> bits) == (col >> bits)

    blk8, blk16, blk32, blk64 = same_block(3), same_block(4), same_block(5), same_block(6)
    causal = blk64 & (col <= row)
    strict = blk64 & (col < row)
    eye = (row == col).astype(F32)

    ab = ab_ref[...]
    g = -jnp.exp(alog_ref[...]) * _softplus(ab + dtb_ref[...])
    beta = jax.nn.sigmoid(ab)
    gc = _select_mm(causal.astype(BF16), g)
    gtot = _select_mm(blk64.astype(BF16), g)
    e_gc = _select_mm(gc, selg_ref[...])
    e_gt = _select_mm(gtot, selg_ref[...])
    e_beta = _select_mm(beta, selb_ref[...])
    r128 = lax.broadcasted_iota(jnp.int32, (LANE, LANE), 0)
    c128 = lax.broadcasted_iota(jnp.int32, (LANE, LANE), 1)
    gc_t = _select_mm((r128 == c128).astype(BF16), gc, (((1,), (1,)), ((), ())))
    eg = jnp.exp(e_gc)
    kdec = jnp.exp(e_gt - e_gc)

    for c in range(tt // CHUNK):
        gend_ref[0, c:c + 1, :] = jnp.exp(e_gt[c * CHUNK:c * CHUNK + 1, :])
    gend_ref[0, tt // CHUNK:, :] = jnp.zeros((HALO - tt // CHUNK, DN_V), F32)

    for h in range(DN_HEADS):
        hs = slice(h * DN_DK, (h + 1) * DN_DK)
        qh = q[:, hs]
        qh = qh * lax.rsqrt(jnp.sum(qh * qh, axis=-1, keepdims=True) + EPS) * (DN_DK ** -0.5)
        kh = k[:, hs]
        kh = kh * lax.rsqrt(jnp.sum(kh * kh, axis=-1, keepdims=True) + EPS)
        vh = v[:, hs]
        beta_h = e_beta[:, hs]
        kb = kh * beta_h
        gcol = e_gc[:, hs]
        diff = jnp.concatenate([gcol] * (tt // LANE), axis=1) - gc_t[h:h + 1, :]
        dmat = jnp.exp(jnp.where(causal, diff, NEG_INF))
        khb = kh.astype(BF16)
        nmat = jnp.where(strict, _mm_nt(kb, khb) * dmat, 0.0)
        attn_ref[0, h] = (_mm_nt(qh, khb) * dmat).astype(BF16)

        n0 = jnp.where(blk8, nmat, 0.0)
        n2 = _mm(n0, n0)
        n4 = _mm(n2, n2)
        inv = eye - n0
        inv = inv + _mm(inv, n2)
        inv = inv + _mm(inv, n4)
        for inner, outer in ((blk8, blk16), (blk16, blk32), (blk32, blk64)):
            off_diag = jnp.where(outer & jnp.logical_not(inner), nmat, 0.0)
            inv = inv - _mm(_mm(inv, off_diag), inv)

        rhs = jnp.concatenate([vh * beta_h, kb * eg[:, hs]], axis=1)
        sol = _mm(inv, rhs)
        u_ref[:, hs] = sol[:, :DN_DV]
        w_ref[:, hs] = sol[:, DN_DV:].astype(BF16)
        qg_ref[:, hs] = (qh * eg[:, hs]).astype(BF16)
        kd_ref[:, hs] = (kh * kdec[:, hs]).astype(BF16)


def _dn_local(proj, proj_ab, cw, alog_row, dtb_row, selg, selb, b, s):
    t = b * s
    nt = s // TT_SEQ
    rows_per_halo = TT_SEQ // HALO

    def main_spec(cb):
        return pl.BlockSpec((TT_SEQ, DN_QK), lambda bi, i: (bi * nt + i, cb))

    def halo_spec(cb):
        return pl.BlockSpec(
            (HALO, DN_QK), lambda bi, i: (jnp.maximum((bi * nt + i) * rows_per_halo - 1, 0), cb))

    def full(shape):
        return pl.BlockSpec(shape, lambda bi, i: (0,) * len(shape))

    tok_spec = pl.BlockSpec((TT_SEQ, DN_V), lambda bi, i: (bi * nt + i, 0))
    return pl.pallas_call(
        _dn_local_body,
        grid=(b, nt),
        in_specs=[main_spec(0), main_spec(1), main_spec(2), halo_spec(0), halo_spec(1), halo_spec(2),
                  pl.BlockSpec((TT_SEQ, LANE), lambda bi, i: (bi * nt + i, 0)),
                  full((CONV_W, CONV_CH)), full((1, LANE)), full((1, LANE)),
                  full((LANE, DN_V)), full((LANE, DN_V))],
        out_specs=[tok_spec, tok_spec, tok_spec, tok_spec,
                   pl.BlockSpec((1, DN_HEADS, TT_SEQ, TT_SEQ), lambda bi, i: (bi * nt + i, 0, 0, 0)),
                   pl.BlockSpec((1, HALO, DN_V), lambda bi, i: (bi * nt + i, 0, 0))],
        out_shape=[jax.ShapeDtypeStruct((t, DN_V), BF16),
                   jax.ShapeDtypeStruct((t, DN_V), BF16),
                   jax.ShapeDtypeStruct((t, DN_V), F32),
                   jax.ShapeDtypeStruct((t, DN_V), BF16),
                   jax.ShapeDtypeStruct((t // TT_SEQ, DN_HEADS, TT_SEQ, TT_SEQ), BF16),
                   jax.ShapeDtypeStruct((t // TT_SEQ, HALO, DN_V), F32)],
        scratch_shapes=[pltpu.VMEM((HALO + TT_SEQ, DN_QK), F32)],
        compiler_params=_cparams(("parallel", "parallel")),
        name="dn_local",
    )(proj, proj, proj, proj, proj, proj, proj_ab, cw, alog_row, dtb_row, selg, selb)


def _dn_scan_body(qg_ref, kd_ref, u_ref, w_ref, attn_ref, gend_ref, o_ref, s_ref, vn_ref):
    nb = qg_ref.shape[0]

    @pl.when(pl.program_id(0) == 0)
    def _():
        s_ref[...] = jnp.zeros(s_ref.shape, F32)

    for b in range(nb):
        for h in range(DN_HEADS):
            ch = b * DN_HEADS + h
            hs = slice(h * DN_DK, (h + 1) * DN_DK)
            state = s_ref[ch]
            vn_ref[ch] = jnp.zeros((TT_SEQ, DN_DV), BF16)
            for c in range(TT_SEQ // CHUNK):
                rs = slice(c * CHUNK, (c + 1) * CHUNK)
                sb = state.astype(BF16)
                v_new = u_ref[b, rs, hs] - jnp.dot(w_ref[b, rs, hs], sb, preferred_element_type=F32)
                vnb = v_new.astype(BF16)
                vn_ref[ch, rs, :] = vnb
                o_ref[b, rs, hs] = (
                    jnp.dot(qg_ref[b, rs, hs], sb, preferred_element_type=F32)
                    + jnp.dot(attn_ref[b, 0, h, rs, :], vn_ref[ch], preferred_element_type=F32))
                state = state * gend_ref[b, 0, c:c + 1, hs] + _mm_tn(kd_ref[b, rs, hs], vnb)
            s_ref[ch] = state


def _dn_scan(qg, kd, u, w, attn, gend, b, s):
    nt = s // TT_SEQ
    tok = lambda a: a.reshape(b, s, DN_V)
    tok_spec = pl.BlockSpec((b, TT_SEQ, DN_V), lambda i: (0, i, 0))
    out = pl.pallas_call(
        _dn_scan_body,
        grid=(nt,),
        in_specs=[tok_spec, tok_spec, tok_spec, tok_spec,
                  pl.BlockSpec((b, 1, DN_HEADS, TT_SEQ, TT_SEQ), lambda i: (0, i, 0, 0, 0)),
                  pl.BlockSpec((b, 1, HALO, DN_V), lambda i: (0, i, 0, 0))],
        out_specs=tok_spec,
        out_shape=jax.ShapeDtypeStruct((b, s, DN_V), F32),
        scratch_shapes=[pltpu.VMEM((b * DN_HEADS, DN_DK, DN_DV), F32),
                        pltpu.VMEM((b * DN_HEADS, TT_SEQ, DN_DV), BF16)],
        compiler_params=_cparams(("arbitrary",)),
        name="dn_scan",
    )(tok(qg), tok(kd), tok(u), tok(w),
      attn.reshape(b, nt, DN_HEADS, TT_SEQ, TT_SEQ), gend.reshape(b, nt, HALO, DN_V))
    return out.reshape(b * s, DN_V)


def _log_gamma(h):
    return math.log1p(-(2.0 ** (-5.0 - h)))


def _ret_body(qk_ref, v_ref, pos_ref, freq_ref, o_ref, s_ref):
    tt = TT_SEQ

    @pl.when(pl.program_id(1) == 0)
    def _():
        s_ref[...] = jnp.zeros(s_ref.shape, F32)

    ang = pos_ref[...].astype(F32) * freq_ref[...]
    lane = lax.broadcasted_iota(jnp.int32, (tt, LANE), 1)
    first_half = (lane & (RET_DK // 2)) == 0
    cos = jnp.cos(ang)
    sin = jnp.sin(ang)
    ssin = jnp.where(first_half, -sin, sin)

    def rotary(x):
        swapped = jnp.where(first_half, pltpu.roll(x, LANE - RET_DK // 2, axis=1),
                            pltpu.roll(x, RET_DK // 2, axis=1))
        return x * cos + swapped * ssin

    row = lax.broadcasted_iota(jnp.int32, (tt, tt), 0)
    col = lax.broadcasted_iota(jnp.int32, (tt, tt), 1)
    rel = (row - col).astype(F32)
    causal = row >= col
    trow = lax.broadcasted_iota(jnp.int32, (tt, LANE), 0).astype(F32)
    srow = lax.broadcasted_iota(jnp.int32, (LANE, LANE), 0)

    for pr in range(RET_HEADS // 2):
        ps = slice(pr * LANE, (pr + 1) * LANE)
        qp = rotary(qk_ref[:, ps])
        kp = rotary(qk_ref[:, RET_QK + pr * LANE:RET_QK + (pr + 1) * LANE]) * (RET_DK ** -0.5)
        kpb = kp.astype(BF16)
        state = s_ref[pr]
        sb = state.astype(BF16)
        update = jnp.zeros((LANE, RET_DV), F32)
        for hh in range(2):
            h = 2 * pr + hh
            lg = _log_gamma(h)
            mine = (lane < RET_DK) if hh == 0 else (lane >= RET_DK)
            qm = jnp.where(mine, qp, 0.0).astype(BF16)
            vh = v_ref[:, h * RET_DV:(h + 1) * RET_DV].astype(BF16)
            att = _mm_nt(qm, kpb) * jnp.exp(jnp.where(causal, rel * lg, NEG_INF))
            inner = jnp.dot(att.astype(BF16), vh, preferred_element_type=F32)
            cross = jnp.dot(qm, sb, preferred_element_type=F32) * jnp.exp((trow + 1.0) * lg)
            o_ref[:, h * RET_DV:(h + 1) * RET_DV] = inner + cross
            kz = jnp.where(mine, kp, 0.0) * jnp.exp((tt - 1.0 - trow) * lg)
            update = update + _mm_tn(kz, vh)
        decay = jnp.where(srow < RET_DK, math.exp(tt * _log_gamma(2 * pr)),
                          math.exp(tt * _log_gamma(2 * pr + 1)))
        s_ref[pr] = state * decay + update


def _ret(proj, pos, freq_row, b, s):
    t = b * s
    nt = s // TT_SEQ
    return pl.pallas_call(
        _ret_body,
        grid=(b, nt),
        in_specs=[pl.BlockSpec((TT_SEQ, 2 * RET_QK), lambda bi, i: (bi * nt + i, 4)),
                  pl.BlockSpec((TT_SEQ, RET_V), lambda bi, i: (bi * nt + i, 5)),
                  pl.BlockSpec((TT_SEQ, 1), lambda bi, i: (bi * nt + i, 0)),
                  pl.BlockSpec((1, LANE), lambda bi, i: (0, 0))],
        out_specs=pl.BlockSpec((TT_SEQ, RET_V), lambda bi, i: (bi * nt + i, 0)),
        out_shape=jax.ShapeDtypeStruct((t, RET_V), F32),
        scratch_shapes=[pltpu.VMEM((RET_HEADS // 2, LANE, RET_DV), F32)],
        compiler_params=_cparams(("parallel", "arbitrary")),
        name="ret",
    )(proj, proj, pos, freq_row)


def _outproj_body(odn_ref, z_ref, or_ref, rg_ref, x_ref, dnw_ref, gnw_ref, gnb_ref, w_ref, o_ref,
                  mix_ref):
    for h in range(DN_HEADS):
        hs = slice(h * DN_DV, (h + 1) * DN_DV)
        mix_ref[:, hs] = (_rms(odn_ref[:, hs], dnw_ref[...]) * _silu(z_ref[:, hs])).astype(BF16)
    for h in range(RET_HEADS):
        hs = slice(h * RET_DV, (h + 1) * RET_DV)
        o = or_ref[:, hs]
        cen = o - jnp.mean(o, axis=-1, keepdims=True)
        y = cen * lax.rsqrt(jnp.mean(cen * cen, axis=-1, keepdims=True) + EPS)
        y = (y * gnw_ref[:, hs] + gnb_ref[:, hs]) * _silu(rg_ref[:, hs])
        mix_ref[:, DN_V + h * RET_DV:DN_V + (h + 1) * RET_DV] = y.astype(BF16)
    o_ref[...] = x_ref[...] + jnp.dot(mix_ref[...], w_ref[...], preferred_element_type=F32)


def _outproj(o_dn, proj, o_r, x, dnw, gnw, gnb, w_out):
    t, d = x.shape
    tm = TM_DENSE
    full = lambda shape: pl.BlockSpec(shape, lambda i: (0, 0))
    return pl.pallas_call(
        _outproj_body,
        grid=(t // tm,),
        in_specs=[pl.BlockSpec((tm, DN_V), lambda i: (i, 0)),
                  pl.BlockSpec((tm, DN_V), lambda i: (i, 3)),
                  pl.BlockSpec((tm, RET_V), lambda i: (i, 0)),
                  pl.BlockSpec((tm, RET_V), lambda i: (i, 6)),
                  pl.BlockSpec((tm, d), lambda i: (i, 0)),
                  full((1, DN_DV)), full((1, RET_V)), full((1, RET_V)), full((DN_V + RET_V, d))],
        out_specs=pl.BlockSpec((tm, d), lambda i: (i, 0)),
        out_shape=jax.ShapeDtypeStruct((t, d), F32),
        scratch_shapes=[pltpu.VMEM((tm, DN_V + RET_V), BF16)],
        compiler_params=_cparams(("parallel",)),
        name="outproj",
    )(o_dn, proj, o_r, proj, x, dnw, gnw, gnb, w_out)


def _pq_body(h_ref, nw_ref, w_ref, keys_ref, hnt_ref, sc_ref):
    hn = _rms(h_ref[...], nw_ref[...])
    hnt_ref[...] = hn.T.astype(BF16)
    q = jnp.dot(hn.astype(BF16), w_ref[...], preferred_element_type=F32)
    half = PK_DQ // 2
    for hp in range(2 * PK_HEADS):
        sc_ref[hp] = _mm_nt(keys_ref[hp], q[:, hp * half:(hp + 1) * half])


def _pq(h1, nw, w_pq, keys):
    t, d = h1.shape
    tm = TM_DENSE
    nk = 2 * PK_HEADS
    return pl.pallas_call(
        _pq_body,
        grid=(t // tm,),
        in_specs=[pl.BlockSpec((tm, d), lambda i: (i, 0)),
                  pl.BlockSpec((1, d), lambda i: (0, 0)),
                  pl.BlockSpec((d, PK_HEADS * PK_DQ), lambda i: (0, 0)),
                  pl.BlockSpec((nk, N_KEYS, PK_DQ // 2), lambda i: (0, 0, 0))],
        out_specs=[pl.BlockSpec((d, tm), lambda i: (0, i)),
                   pl.BlockSpec((nk, N_KEYS, tm), lambda i: (0, 0, i))],
        out_shape=[jax.ShapeDtypeStruct((d, t), BF16),
                   jax.ShapeDtypeStruct((nk, N_KEYS, t), F32)],
        compiler_params=_cparams(("parallel",)),
        name="pq",
    )(h1, nw, w_pq, keys)


def _route(a, bsc, exact):
    tl = a.shape[1]
    k = PK_TOPK
    keyid = lax.broadcasted_iota(jnp.int32, (N_KEYS, tl), 0)
    slot = lax.broadcasted_iota(jnp.int32, (k, tl), 0)

    def extract(s, ids, n_ids):
        m = jnp.max(s, axis=0, keepdims=True)
        hit = s == m
        if exact:
            hit = ids == jnp.min(jnp.where(hit, ids, n_ids), axis=0, keepdims=True)
        return m, hit

    def top_k(s):
        rank = jnp.full((N_KEYS, tl), float(k), F32)
        vals = jnp.zeros((k, tl), F32)
        for r in range(k):
            m, hit = extract(s, keyid, N_KEYS)
            rank = jnp.where(hit, float(r), rank)
            s = jnp.where(hit, NEG_INF, s)
            vals = jnp.where(slot == r, m, vals)
        return vals, rank

    av, rank_a = top_k(a)
    bv, rank_b = top_k(bsc)

    k2 = k // 2
    assert k2 & (k2 - 1) == 0
    n_cand = k + (k - 1) * k2
    cand = jnp.concatenate([av[0:1, :] + bv] + [av[r:r + 1, :] + bv[0:k2, :] for r in range(1, k)],
                           axis=0)
    row = lax.broadcasted_iota(jnp.int32, (n_cand, tl), 0)
    tail = row - k
    cid = jnp.where(row < k, row,
                    (1 + (tail >> (k2.bit_length() - 1))) * k + (tail & (k2 - 1)))
    work = cand
    for _ in range(k):
        _, hit = extract(work, cid, k * k)
        work = jnp.where(hit, NEG_INF, work)
    sel = (work == NEG_INF).astype(F32)
    zsum = jnp.sum(sel * jnp.exp(cand - cand[0:1, :]), axis=0, keepdims=True)

    cnt = jnp.zeros((N_KEYS, tl), F32)
    for r in range(k):
        lo, hi = (0, k) if r == 0 else (k + (r - 1) * k2, k + r * k2)
        cnt_r = jnp.sum(sel[lo:hi, :], axis=0, keepdims=True)
        cnt = jnp.where(rank_a == float(r), cnt_r, cnt)

    def full_count(x):
        return jnp.sum(x, axis=0, keepdims=True) == float(k)

    ok = (full_count((rank_a < float(k)).astype(F32)) & full_count((rank_b < float(k)).astype(F32))
          & full_count(sel))
    eb = jnp.exp(bsc - bv[0:1, :])
    ea = jnp.exp(a - av[0:1, :]) / zsum
    return rank_b, eb, cnt, ea, ok


def _router_body(sc_ref, rankb_ref, eb_ref, cnt_ref, ea_ref):
    def run(exact):
        rank_b, eb, cnt, ea, ok = _route(sc_ref[0], sc_ref[1], exact)
        rankb_ref[0] = rank_b.astype(BF16)
        eb_ref[0] = eb.astype(BF16)
        cnt_ref[0] = cnt
        ea_ref[0] = ea
        return ok

    ok = run(exact=False)
    n_bad = jnp.sum(jnp.where(ok, 0.0, 1.0), axis=1, keepdims=True)

    @pl.when(n_bad[0, 0] > 0.0)
    def _():
        run(exact=True)


def _router(scores):
    nk, n, t = scores.shape
    tl = TL_ROUTER
    spec = pl.BlockSpec((1, n, tl), lambda h, j: (h, 0, j))
    shape = lambda dt: jax.ShapeDtypeStruct((PK_HEADS, n, t), dt)
    return pl.pallas_call(
        _router_body,
        grid=(PK_HEADS, t // tl),
        in_specs=[pl.BlockSpec((2, n, tl), lambda h, j: (h, 0, j))],
        out_specs=[spec, spec, spec, spec],
        out_shape=[shape(BF16), shape(BF16), shape(F32), shape(F32)],
        compiler_params=_cparams(("parallel", "parallel")),
        name="router",
    )(scores)


def _peer_pre_body(u_ref, hnt_ref, o_ref):
    o_ref[...] = jnp.dot(u_ref[...], hnt_ref[...], preferred_element_type=F32)


def _peer_step(pre_ref, pre_next_ref, hnt_ref, u_ref, vt_ref, rankb_ref, eb_ref, cnt_ref, ea_ref,
               acc_ref, act_ref):
    pre_next_ref[...] = jnp.dot(u_ref[...], hnt_ref[...], preferred_element_type=F32)
    for il in range(NI_PEER):
        rs = slice(il * N_KEYS, (il + 1) * N_KEYS)
        gate = jnp.zeros((N_KEYS, TM_PEER), BF16)
        for h in range(PK_HEADS):
            picked = rankb_ref[h] < cnt_ref[h, il:il + 1, :].astype(BF16)
            gate = gate + jnp.where(picked, eb_ref[h], 0.0) * ea_ref[h, il:il + 1, :].astype(BF16)
        act_ref[rs, :] = _gelu(pre_ref[rs, :]).astype(BF16) * gate
    acc_ref[...] += jnp.dot(vt_ref[...], act_ref[...], preferred_element_type=F32)


def _peer_body(pre0_ref, hnt_ref, u_ref, vt_ref, rankb_ref, eb_ref, cnt_ref, ea_ref, o_ref,
               acc_ref, act_ref, pre_a_ref, pre_b_ref):
    i = pl.program_id(0)
    j = pl.program_id(1)
    args = (hnt_ref, u_ref, vt_ref, rankb_ref, eb_ref, cnt_ref, ea_ref, acc_ref, act_ref)

    @pl.when((i == 0) & (j == 0))
    def _():
        pre_a_ref[...] = pre0_ref[...]

    @pl.when(j == 0)
    def _():
        acc_ref[...] = jnp.zeros(acc_ref.shape, F32)

    @pl.when(lax.rem(j, 2) == 0)
    def _():
        _peer_step(pre_a_ref, pre_b_ref, *args)

    @pl.when(lax.rem(j, 2) == 1)
    def _():
        _peer_step(pre_b_ref, pre_a_ref, *args)

    @pl.when(j == pl.num_programs(1) - 1)
    def _():
        o_ref[...] = acc_ref[...].T


def _peer(hnt, u_bf, vt_bf, rankb, eb, cnt, ea):
    d, t = hnt.shape
    ne = u_bf.shape[0]
    tm, te = TM_PEER, TE_PEER
    ni, nj = t // tm, ne // te
    assert nj % 2 == 0, "pre-activation buffers alternate with the expert-tile index"
    pre0 = pl.pallas_call(
        _peer_pre_body,
        grid=(1,),
        in_specs=[pl.BlockSpec((te, d), lambda i: (0, 0)), pl.BlockSpec((d, tm), lambda i: (0, 0))],
        out_specs=pl.BlockSpec((te, tm), lambda i: (0, 0)),
        out_shape=jax.ShapeDtypeStruct((te, tm), F32),
        compiler_params=_cparams(("arbitrary",)),
        name="peer_pre",
    )(u_bf, hnt)

    def next_i(i, j):
        return jnp.minimum(i + (j + 1) // nj, ni - 1)

    key_spec = pl.BlockSpec((PK_HEADS, N_KEYS, tm), lambda i, j: (0, 0, i))
    blk_spec = pl.BlockSpec((PK_HEADS, NI_PEER, tm), lambda i, j: (0, j, i))
    return pl.pallas_call(
        _peer_body,
        grid=(ni, nj),
        in_specs=[pl.BlockSpec((te, tm), lambda i, j: (0, 0)),
                  pl.BlockSpec((d, tm), lambda i, j: (0, next_i(i, j))),
                  pl.BlockSpec((te, d), lambda i, j: ((j + 1) % nj, 0)),
                  pl.BlockSpec((d, te), lambda i, j: (0, j)),
                  key_spec, key_spec, blk_spec, blk_spec],
        out_specs=pl.BlockSpec((tm, d), lambda i, j: (i, 0)),
        out_shape=jax.ShapeDtypeStruct((t, d), F32),
        scratch_shapes=[pltpu.VMEM((d, tm), F32), pltpu.VMEM((te, tm), BF16),
                        pltpu.VMEM((te, tm), F32), pltpu.VMEM((te, tm), F32)],
        compiler_params=_cparams(("arbitrary", "arbitrary")),
        name="peer",
    )(pre0, hnt, u_bf, vt_bf, rankb, eb, cnt, ea)


def _ple_body(h_ref, po_ref, p_ref, nple_ref, wg_ref, wp_ref, nfin_ref, o_ref):
    h2 = h_ref[...] + po_ref[...]
    gate = jax.nn.sigmoid(
        jnp.dot(_rms(h2, nple_ref[...]).astype(BF16), wg_ref[...], preferred_element_type=F32))
    ple = jnp.dot(p_ref[...].astype(BF16), wp_ref[...], preferred_element_type=F32)
    o_ref[...] = _rms(h2 + gate * ple, nfin_ref[...])


def _ple(h1, peer_out, p, nple, wg, wp, nfin):
    t, d = h1.shape
    tm = TM_DENSE
    full = lambda shape: pl.BlockSpec(shape, lambda i: (0, 0))
    return pl.pallas_call(
        _ple_body,
        grid=(t // tm,),
        in_specs=[pl.BlockSpec((tm, d), lambda i: (i, 0)),
                  pl.BlockSpec((tm, d), lambda i: (i, 0)),
                  pl.BlockSpec((tm, PLE_DIM), lambda i: (i, 0)),
                  full((1, d)), full((d, d)), full((PLE_DIM, d)), full((1, d))],
        out_specs=pl.BlockSpec((tm, d), lambda i: (i, 0)),
        out_shape=jax.ShapeDtypeStruct((t, d), F32),
        compiler_params=_cparams(("parallel",)),
        name="ple",
    )(h1, peer_out, p, nple, wg, wp, nfin)


def _lane_row(vec, offset):
    return jnp.zeros((1, LANE), F32).at[0, offset:offset + vec.shape[0]].set(vec.astype(F32))


def _head_selector(offset):
    sel = np.zeros((LANE, DN_V), np.float32)
    for h in range(DN_HEADS):
        sel[offset + h, h * DN_DV:(h + 1) * DN_DV] = 1.0
    return jnp.asarray(sel, dtype=BF16)


def kernel(x, p, positions, norm_mix, w_in, conv_w, a_log, dt_bias, dn_norm, ret_gn_w, ret_gn_b,
           w_out, norm_ffn, w_pq, sub_keys, expert_u, expert_v, norm_ple, w_ple_gate, w_ple_proj,
           norm_final):
    b, s, d = x.shape
    t = b * s
    depth = w_in.shape[0]
    assert depth == 1, "the final rms_norm is fused into the single layer's ple kernel"
    half = RET_DK // 2
    inv_freq = ROPE_BASE ** (-jnp.arange(half, dtype=F32) / half)
    freq_row = jnp.tile(inv_freq, LANE // half).reshape(1, LANE)
    pos = positions.reshape(t, 1)
    selg, selb = _head_selector(0), _head_selector(DN_HEADS)

    h = x.reshape(t, d)
    for i in range(depth):
        w = w_in[i]
        w_main = jnp.concatenate([w[:, :AB_OFF], w[:, AB_OFF + 2 * DN_HEADS:]], axis=1).astype(BF16)
        w_ab = jnp.pad(w[:, AB_OFF:AB_OFF + 2 * DN_HEADS],
                       ((0, 0), (0, LANE - 2 * DN_HEADS))).astype(BF16)
        proj, proj_ab = _inproj(h, norm_mix[i].reshape(1, d), w_main, w_ab)

        qg, kd, u, wv, attn, gend = _dn_local(
            proj, proj_ab, conv_w[i].astype(F32).T, _lane_row(a_log[i], 0), _lane_row(dt_bias[i], 0),
            selg, selb, b, s)
        o_dn = _dn_scan(qg, kd, u, wv, attn, gend, b, s)
        o_r = _ret(proj, pos, freq_row, b, s)
        h1 = _outproj(o_dn, proj, o_r, h, dn_norm[i].reshape(1, DN_DV), ret_gn_w[i].reshape(1, RET_V),
                      ret_gn_b[i].reshape(1, RET_V), w_out[i].astype(BF16))

        keys = sub_keys[i].reshape(2 * PK_HEADS, N_KEYS, PK_DQ // 2).astype(BF16)
        hnt, scores = _pq(h1, norm_ffn[i].reshape(1, d), w_pq[i].astype(BF16), keys)
        rankb, eb, cnt, ea = _router(scores)
        peer_out = _peer(hnt, expert_u[i].astype(BF16), expert_v[i].T.astype(BF16), rankb, eb, cnt, ea)

        h = _ple(h1, peer_out, p[i].reshape(t, PLE_DIM), norm_ple[i].reshape(1, d),
                 w_ple_gate[i].astype(BF16), w_ple_proj[i].astype(BF16), norm_final.reshape(1, d))
    return h.reshape(b, s, d)
```

```python
import functools
import math

import numpy as np
import jax
import jax.numpy as jnp
from jax import lax
from jax.experimental import pallas as pl
from jax.experimental.pallas import tpu as pltpu

F32 = jnp.float32
BF16 = jnp.bfloat16
NEG_INF = float("-inf")

EPS = 1e-6
D_MODEL = 2048
DN_HEADS = 8
DN_DK = 128
DN_DV = 128
CONV_W = 4
RET_HEADS = 8
RET_DK = 64
RET_DV = 128
CHUNK = 64
ROPE_BASE = 10000.0
N_KEYS = 128
PK_HEADS = 8
PK_DQ = 256
PK_TOPK = 16
PLE_DIM = 256

DN_QK = DN_HEADS * DN_DK
DN_V = DN_HEADS * DN_DV
CONV_CH = 2 * DN_QK + DN_V
RET_QK = RET_HEADS * RET_DK
RET_V = RET_HEADS * RET_DV
AB_OFF = CONV_CH + DN_V
MAIN_COLS = CONV_CH + DN_V + 2 * RET_QK + 2 * RET_V
LANE = 128
HALO = 8

TM_INPROJ = 1024
TN_INPROJ = 512
TT_SEQ = 256
TM_DENSE = 256
TL_ROUTER = 256
TM_PEER = 512
NI_PEER = 8
TE_PEER = NI_PEER * N_KEYS
VMEM_LIMIT = 56 * 1024 * 1024


def _cparams(sem, flags=None):
    return pltpu.CompilerParams(dimension_semantics=sem, vmem_limit_bytes=VMEM_LIMIT, flags=flags)


def _rms(x, w):
    return x * lax.rsqrt(jnp.mean(x * x, axis=-1, keepdims=True) + EPS) * w


def _silu(x):
    return x * jax.nn.sigmoid(x)


def _softplus(x):
    return jnp.maximum(x, 0.0) + jnp.log1p(jnp.exp(-jnp.abs(x)))


def _gelu(x):
    return 0.5 * x * (1.0 + lax.erf(x * (2.0 ** -0.5)))


def _mm(a, b):
    return jnp.dot(a.astype(BF16), b.astype(BF16), preferred_element_type=F32)


def _mm_nt(a, b):
    return lax.dot_general(a.astype(BF16), b.astype(BF16), (((1,), (1,)), ((), ())),
                           preferred_element_type=F32)


def _mm_tn(a, b):
    return lax.dot_general(a.astype(BF16), b.astype(BF16), (((0,), (0,)), ((), ())),
                           preferred_element_type=F32)


def _split3(x):
    hi = x.astype(BF16)
    rest = x - hi.astype(F32)
    mid = rest.astype(BF16)
    lo = (rest - mid.astype(F32)).astype(BF16)
    return hi, mid, lo


def _select_mm(a, b, dims=(((1,), (0,)), ((), ()))):
    if a.dtype == BF16:
        parts = [lax.dot_general(a, p, dims, preferred_element_type=F32) for p in _split3(b)]
    else:
        parts = [lax.dot_general(p, b, dims, preferred_element_type=F32) for p in _split3(a)]
    return parts[0] + parts[1] + parts[2]


def _inproj_body(x_ref, nw_ref, w_ref, wab_ref, o_ref, oab_ref, hn_ref):
    @pl.when(pl.program_id(1) == 0)
    def _():
        hn_ref[...] = _rms(x_ref[...], nw_ref[...]).astype(BF16)
        oab_ref[...] = jnp.dot(hn_ref[...], wab_ref[...], preferred_element_type=F32)

    o_ref[...] = jnp.dot(hn_ref[...], w_ref[...], preferred_element_type=F32)


def _inproj(x, nw, w_main, w_ab):
    t, d = x.shape
    n = w_main.shape[1]
    return pl.pallas_call(
        _inproj_body,
        grid=(t // TM_INPROJ, n // TN_INPROJ),
        in_specs=[
            pl.BlockSpec((TM_INPROJ, d), lambda i, j: (i, 0)),
            pl.BlockSpec((1, d), lambda i, j: (0, 0)),
            pl.BlockSpec((d, TN_INPROJ), lambda i, j: (0, j)),
            pl.BlockSpec((d, LANE), lambda i, j: (0, 0)),
        ],
        out_specs=[
            pl.BlockSpec((TM_INPROJ, TN_INPROJ), lambda i, j: (i, j)),
            pl.BlockSpec((TM_INPROJ, LANE), lambda i, j: (i, 0)),
        ],
        out_shape=[jax.ShapeDtypeStruct((t, n), F32), jax.ShapeDtypeStruct((t, LANE), F32)],
        scratch_shapes=[pltpu.VMEM((TM_INPROJ, d), BF16)],
        compiler_params=_cparams(("parallel", "arbitrary")),
        name="inproj",
    )(x, nw, w_main, w_ab)


def _dn_local_body(q_ref, k_ref, v_ref, qh_ref, kh_ref, vh_ref, ab_ref, cw_ref, alog_ref, dtb_ref,
                   selg_ref, selb_ref, qa_ref, kd_ref, u_ref, w_ref, o0_ref, gend_ref, buf_ref):
    tt = TT_SEQ
    first_tile = pl.program_id(1) == 0

    def conv_silu(main_ref, halo_ref, c0):
        halo = halo_ref[...]
        buf_ref[0:HALO, :] = jnp.where(first_tile, jnp.zeros_like(halo), halo)
        buf_ref[HALO:HALO + tt, :] = main_ref[...]
        y = None
        for j in range(CONV_W):
            off = HALO - (CONV_W - 1) + j
            term = buf_ref[off:off + tt, :] * cw_ref[j:j + 1, c0:c0 + DN_QK]
            y = term if y is None else y + term
        return _silu(y)

    q = conv_silu(q_ref, qh_ref, 0)
    k = conv_silu(k_ref, kh_ref, DN_QK)
    v = conv_silu(v_ref, vh_ref, 2 * DN_QK)

    row = lax.broadcasted_iota(jnp.int32, (tt, tt), 0)
    col = lax.broadcasted_iota(jnp.int32, (tt, tt), 1)

    def same_block(bits):
        return (row >> bits) == (col >> bits)

    blk8, blk16, blk32, blk64 = same_block(3), same_block(4), same_block(5), same_block(6)
    causal = blk64 & (col <= row)
    strict = blk64 & (col < row)
    eye = (row == col).astype(F32)

    ab = ab_ref[...]
    g = -jnp.exp(alog_ref[...]) * _softplus(ab + dtb_ref[...])
    beta = jax.nn.sigmoid(ab)
    def ones_where(mask):
        return jnp.where(mask, 1.0, 0.0).astype(BF16)

    gc = _select_mm(ones_where(causal), g)
    gtot = _select_mm(ones_where(blk64), g)
    e_gc = _select_mm(gc, selg_ref[...])
    e_gt = _select_mm(gtot, selg_ref[...])
    e_beta = _select_mm(beta, selb_ref[...])
    r128 = lax.broadcasted_iota(jnp.int32, (LANE, LANE), 0)
    c128 = lax.broadcasted_iota(jnp.int32, (LANE, LANE), 1)
    gc_t = _select_mm(ones_where(r128 == c128), gc, (((1,), (1,)), ((), ())))
    eg = jnp.exp(e_gc)
    kdec = jnp.exp(e_gt - e_gc)

    for c in range(tt // CHUNK):
        gend_ref[0, c:c + 1, :] = jnp.exp(e_gt[c * CHUNK:c * CHUNK + 1, :])
    gend_ref[0, tt // CHUNK:, :] = jnp.zeros((HALO - tt // CHUNK, DN_V), F32)

    heads = range(DN_HEADS)
    hsl = [slice(h * DN_DK, (h + 1) * DN_DK) for h in heads]

    def per_head(fn):
        return [fn(h) for h in heads]

    def l2n(x):
        return x * lax.rsqrt(jnp.sum(x * x, axis=-1, keepdims=True) + EPS)

    qh = per_head(lambda h: l2n(q[:, hsl[h]]) * (DN_DK ** -0.5))
    kh = per_head(lambda h: l2n(k[:, hsl[h]]))
    kb = per_head(lambda h: kh[h] * e_beta[:, hsl[h]])
    dmat = per_head(lambda h: jnp.exp(jnp.where(
        causal, jnp.concatenate([e_gc[:, hsl[h]]] * (tt // LANE), axis=1) - gc_t[h:h + 1, :], NEG_INF)))
    scores = per_head(lambda h: _mm_nt(jnp.concatenate([kb[h], qh[h]], axis=0), kh[h]))
    nmat = per_head(lambda h: jnp.where(strict, scores[h][:tt] * dmat[h], 0.0))
    attn = per_head(lambda h: scores[h][tt:] * dmat[h])

    n0 = per_head(lambda h: jnp.where(blk8, nmat[h], 0.0))
    n2 = per_head(lambda h: _mm(n0[h], n0[h]))
    n4 = per_head(lambda h: _mm(n2[h], n2[h]))
    inv = per_head(lambda h: eye - n0[h])
    inv = per_head(lambda h: inv[h] + _mm(inv[h], n2[h]))
    inv = per_head(lambda h: inv[h] + _mm(inv[h], n4[h]))
    for inner, outer in ((blk8, blk16), (blk16, blk32), (blk32, blk64)):
        level = outer & jnp.logical_not(inner)
        prod = per_head(lambda h: _mm(inv[h], jnp.where(level, nmat[h], 0.0)))
        inv = per_head(lambda h: inv[h] - _mm(prod[h], inv[h]))

    sol = per_head(lambda h: _mm(inv[h], jnp.concatenate(
        [v[:, hsl[h]] * e_beta[:, hsl[h]], kb[h] * eg[:, hsl[h]]], axis=1)))
    asol = per_head(lambda h: _mm(attn[h], sol[h]))
    for h in heads:
        hs = hsl[h]
        u_ref[:, hs] = sol[h][:, :DN_DV]
        w_ref[:, hs] = sol[h][:, DN_DV:].astype(BF16)
        o0_ref[:, hs] = asol[h][:, :DN_DV]
        qa_ref[:, hs] = (qh[h] * eg[:, hs] - asol[h][:, DN_DV:]).astype(BF16)
        kd_ref[:, hs] = (kh[h] * kdec[:, hs]).astype(BF16)


def _dn_local(proj, proj_ab, cw, alog_row, dtb_row, selg, selb, b, s):
    t = b * s
    nt = s // TT_SEQ
    rows_per_halo = TT_SEQ // HALO

    def main_spec(cb):
        return pl.BlockSpec((TT_SEQ, DN_QK), lambda bi, i: (bi * nt + i, cb))

    def halo_spec(cb):
        return pl.BlockSpec(
            (HALO, DN_QK), lambda bi, i: (jnp.maximum((bi * nt + i) * rows_per_halo - 1, 0), cb))

    def full(shape):
        return pl.BlockSpec(shape, lambda bi, i: (0,) * len(shape))

    tok_spec = pl.BlockSpec((TT_SEQ, DN_V), lambda bi, i: (bi * nt + i, 0))
    return pl.pallas_call(
        _dn_local_body,
        grid=(b, nt),
        in_specs=[main_spec(0), main_spec(1), main_spec(2), halo_spec(0), halo_spec(1), halo_spec(2),
                  pl.BlockSpec((TT_SEQ, LANE), lambda bi, i: (bi * nt + i, 0)),
                  full((CONV_W, CONV_CH)), full((1, LANE)), full((1, LANE)),
                  full((LANE, DN_V)), full((LANE, DN_V))],
        out_specs=[tok_spec, tok_spec, tok_spec, tok_spec, tok_spec,
                   pl.BlockSpec((1, HALO, DN_V), lambda bi, i: (bi * nt + i, 0, 0))],
        out_shape=[jax.ShapeDtypeStruct((t, DN_V), BF16),
                   jax.ShapeDtypeStruct((t, DN_V), BF16),
                   jax.ShapeDtypeStruct((t, DN_V), F32),
                   jax.ShapeDtypeStruct((t, DN_V), BF16),
                   jax.ShapeDtypeStruct((t, DN_V), F32),
                   jax.ShapeDtypeStruct((t // TT_SEQ, HALO, DN_V), F32)],
        scratch_shapes=[pltpu.VMEM((HALO + TT_SEQ, DN_QK), F32)],
        compiler_params=_cparams(("parallel", "parallel")),
        name="dn_local",
    )(proj, proj, proj, proj, proj, proj, proj_ab, cw, alog_row, dtb_row, selg, selb)


def _dn_scan_body(qa_ref, kd_ref, u_ref, w_ref, o0_ref, gend_ref, o_ref, s_ref):
    nb = qa_ref.shape[0]

    @pl.when(pl.program_id(0) == 0)
    def _():
        s_ref[...] = jnp.zeros(s_ref.shape, F32)

    chains = [(b, h, slice(h * DN_DK, (h + 1) * DN_DK)) for b in range(nb) for h in range(DN_HEADS)]
    states = [s_ref[n] for n in range(len(chains))]
    for c in range(TT_SEQ // CHUNK):
        rs = slice(c * CHUNK, (c + 1) * CHUNK)
        both = [jnp.dot(jnp.concatenate([w_ref[b, rs, hs], qa_ref[b, rs, hs]], axis=0),
                        states[n].astype(BF16), preferred_element_type=F32)
                for n, (b, h, hs) in enumerate(chains)]
        for n, (b, h, hs) in enumerate(chains):
            o_ref[b, rs, hs] = both[n][CHUNK:] + o0_ref[b, rs, hs]
        states = [states[n] * gend_ref[b, 0, c:c + 1, hs]
                  + _mm_tn(kd_ref[b, rs, hs], u_ref[b, rs, hs] - both[n][:CHUNK])
                  for n, (b, h, hs) in enumerate(chains)]
    for n in range(len(chains)):
        s_ref[n] = states[n]


def _dn_scan(qa, kd, u, w, o0, gend, b, s):
    nt = s // TT_SEQ
    tok = lambda a: a.reshape(b, s, DN_V)
    tok_spec = pl.BlockSpec((b, TT_SEQ, DN_V), lambda i: (0, i, 0))
    out = pl.pallas_call(
        _dn_scan_body,
        grid=(nt,),
        in_specs=[tok_spec, tok_spec, tok_spec, tok_spec, tok_spec,
                  pl.BlockSpec((b, 1, HALO, DN_V), lambda i: (0, i, 0, 0))],
        out_specs=tok_spec,
        out_shape=jax.ShapeDtypeStruct((b, s, DN_V), F32),
        scratch_shapes=[pltpu.VMEM((b * DN_HEADS, DN_DK, DN_DV), F32)],
        compiler_params=_cparams(("arbitrary",)),
        name="dn_scan",
    )(tok(qa), tok(kd), tok(u), tok(w), tok(o0), gend.reshape(b, nt, HALO, DN_V))
    return out.reshape(b * s, DN_V)


def _log_gamma(h):
    return math.log1p(-(2.0 ** (-5.0 - h)))


def _ret_body(qk_ref, v_ref, pos_ref, freq_ref, o_ref, s_ref):
    tt = TT_SEQ

    @pl.when(pl.program_id(1) == 0)
    def _():
        s_ref[...] = jnp.zeros(s_ref.shape, F32)

    ang = pos_ref[...].astype(F32) * freq_ref[...]
    lane = lax.broadcasted_iota(jnp.int32, (tt, LANE), 1)
    first_half = (lane & (RET_DK // 2)) == 0
    cos = jnp.cos(ang)
    sin = jnp.sin(ang)
    ssin = jnp.where(first_half, -sin, sin)

    def rotary(x):
        swapped = jnp.where(first_half, pltpu.roll(x, LANE - RET_DK // 2, axis=1),
                            pltpu.roll(x, RET_DK // 2, axis=1))
        return x * cos + swapped * ssin

    row = lax.broadcasted_iota(jnp.int32, (tt, tt), 0)
    col = lax.broadcasted_iota(jnp.int32, (tt, tt), 1)
    rel = (row - col).astype(F32)
    causal = row >= col
    trow = lax.broadcasted_iota(jnp.int32, (tt, LANE), 0).astype(F32)
    srow = lax.broadcasted_iota(jnp.int32, (LANE, LANE), 0)

    for pr in range(RET_HEADS // 2):
        ps = slice(pr * LANE, (pr + 1) * LANE)
        qp = rotary(qk_ref[:, ps])
        kp = rotary(qk_ref[:, RET_QK + pr * LANE:RET_QK + (pr + 1) * LANE]) * (RET_DK ** -0.5)
        kpb = kp.astype(BF16)
        state = s_ref[pr]
        sb = state.astype(BF16)
        update = jnp.zeros((LANE, RET_DV), F32)
        for hh in range(2):
            h = 2 * pr + hh
            lg = _log_gamma(h)
            mine = (lane < RET_DK) if hh == 0 else (lane >= RET_DK)
            qm = jnp.where(mine, qp, 0.0).astype(BF16)
            vh = v_ref[:, h * RET_DV:(h + 1) * RET_DV].astype(BF16)
            att = _mm_nt(qm, kpb) * jnp.exp(jnp.where(causal, rel * lg, NEG_INF))
            inner = jnp.dot(att.astype(BF16), vh, preferred_element_type=F32)
            cross = jnp.dot(qm, sb, preferred_element_type=F32) * jnp.exp((trow + 1.0) * lg)
            o_ref[:, h * RET_DV:(h + 1) * RET_DV] = inner + cross
            kz = jnp.where(mine, kp, 0.0) * jnp.exp((tt - 1.0 - trow) * lg)
            update = update + _mm_tn(kz, vh)
        decay = jnp.where(srow < RET_DK, math.exp(tt * _log_gamma(2 * pr)),
                          math.exp(tt * _log_gamma(2 * pr + 1)))
        s_ref[pr] = state * decay + update


def _ret(proj, pos, freq_row, b, s):
    t = b * s
    nt = s // TT_SEQ
    return pl.pallas_call(
        _ret_body,
        grid=(b, nt),
        in_specs=[pl.BlockSpec((TT_SEQ, 2 * RET_QK), lambda bi, i: (bi * nt + i, 4)),
                  pl.BlockSpec((TT_SEQ, RET_V), lambda bi, i: (bi * nt + i, 5)),
                  pl.BlockSpec((TT_SEQ, 1), lambda bi, i: (bi * nt + i, 0)),
                  pl.BlockSpec((1, LANE), lambda bi, i: (0, 0))],
        out_specs=pl.BlockSpec((TT_SEQ, RET_V), lambda bi, i: (bi * nt + i, 0)),
        out_shape=jax.ShapeDtypeStruct((t, RET_V), F32),
        scratch_shapes=[pltpu.VMEM((RET_HEADS // 2, LANE, RET_DV), F32)],
        compiler_params=_cparams(("parallel", "arbitrary")),
        name="ret",
    )(proj, proj, pos, freq_row)


def _outproj_body(odn_ref, z_ref, or_ref, rg_ref, x_ref, dnw_ref, gnw_ref, gnb_ref, w_ref, o_ref,
                  mix_ref):
    for h in range(DN_HEADS):
        hs = slice(h * DN_DV, (h + 1) * DN_DV)
        mix_ref[:, hs] = (_rms(odn_ref[:, hs], dnw_ref[...]) * _silu(z_ref[:, hs])).astype(BF16)
    for h in range(RET_HEADS):
        hs = slice(h * RET_DV, (h + 1) * RET_DV)
        o = or_ref[:, hs]
        cen = o - jnp.mean(o, axis=-1, keepdims=True)
        y = cen * lax.rsqrt(jnp.mean(cen * cen, axis=-1, keepdims=True) + EPS)
        y = (y * gnw_ref[:, hs] + gnb_ref[:, hs]) * _silu(rg_ref[:, hs])
        mix_ref[:, DN_V + h * RET_DV:DN_V + (h + 1) * RET_DV] = y.astype(BF16)
    o_ref[...] = x_ref[...] + jnp.dot(mix_ref[...], w_ref[...], preferred_element_type=F32)


def _outproj(o_dn, proj, o_r, x, dnw, gnw, gnb, w_out):
    t, d = x.shape
    tm = TM_DENSE
    full = lambda shape: pl.BlockSpec(shape, lambda i: (0, 0))
    return pl.pallas_call(
        _outproj_body,
        grid=(t // tm,),
        in_specs=[pl.BlockSpec((tm, DN_V), lambda i: (i, 0)),
                  pl.BlockSpec((tm, DN_V), lambda i: (i, 3)),
                  pl.BlockSpec((tm, RET_V), lambda i: (i, 0)),
                  pl.BlockSpec((tm, RET_V), lambda i: (i, 6)),
                  pl.BlockSpec((tm, d), lambda i: (i, 0)),
                  full((1, DN_DV)), full((1, RET_V)), full((1, RET_V)), full((DN_V + RET_V, d))],
        out_specs=pl.BlockSpec((tm, d), lambda i: (i, 0)),
        out_shape=jax.ShapeDtypeStruct((t, d), F32),
        scratch_shapes=[pltpu.VMEM((tm, DN_V + RET_V), BF16)],
        compiler_params=_cparams(("parallel",)),
        name="outproj",
    )(o_dn, proj, o_r, proj, x, dnw, gnw, gnb, w_out)


def _pq_body(h_ref, nw_ref, w_ref, keys_ref, hnt_ref, sc_ref):
    hn = _rms(h_ref[...], nw_ref[...])
    hnt_ref[...] = hn.T.astype(BF16)
    q = jnp.dot(hn.astype(BF16), w_ref[...], preferred_element_type=F32)
    half = PK_DQ // 2
    for hp in range(2 * PK_HEADS):
        sc_ref[hp] = _mm_nt(keys_ref[hp], q[:, hp * half:(hp + 1) * half])


def _pq(h1, nw, w_pq, keys):
    t, d = h1.shape
    tm = TM_DENSE
    nk = 2 * PK_HEADS
    return pl.pallas_call(
        _pq_body,
        grid=(t // tm,),
        in_specs=[pl.BlockSpec((tm, d), lambda i: (i, 0)),
                  pl.BlockSpec((1, d), lambda i: (0, 0)),
                  pl.BlockSpec((d, PK_HEADS * PK_DQ), lambda i: (0, 0)),
                  pl.BlockSpec((nk, N_KEYS, PK_DQ // 2), lambda i: (0, 0, 0))],
        out_specs=[pl.BlockSpec((d, tm), lambda i: (0, i)),
                   pl.BlockSpec((nk, N_KEYS, tm), lambda i: (0, 0, i))],
        out_shape=[jax.ShapeDtypeStruct((d, t), BF16),
                   jax.ShapeDtypeStruct((nk, N_KEYS, t), F32)],
        compiler_params=_cparams(("parallel",)),
        name="pq",
    )(h1, nw, w_pq, keys)


def _route(a, bsc, exact):
    tl = a.shape[1]
    k = PK_TOPK
    keyid = lax.broadcasted_iota(jnp.int32, (N_KEYS, tl), 0)
    slot = lax.broadcasted_iota(jnp.int32, (k, tl), 0)

    def extract(s, ids, n_ids):
        m = jnp.max(s, axis=0, keepdims=True)
        hit = s == m
        if exact:
            hit = ids == jnp.min(jnp.where(hit, ids, n_ids), axis=0, keepdims=True)
        return m, hit

    def top_k(s):
        rank = jnp.full((N_KEYS, tl), float(k), F32)
        vals = jnp.zeros((k, tl), F32)
        for r in range(k):
            m, hit = extract(s, keyid, N_KEYS)
            rank = jnp.where(hit, float(r), rank)
            s = jnp.where(hit, NEG_INF, s)
            vals = jnp.where(slot == r, m, vals)
        return vals, rank

    av, rank_a = top_k(a)
    bv, rank_b = top_k(bsc)

    k2 = k // 2
    assert k2 & (k2 - 1) == 0
    n_cand = k + (k - 1) * k2
    cand = jnp.concatenate([av[0:1, :] + bv] + [av[r:r + 1, :] + bv[0:k2, :] for r in range(1, k)],
                           axis=0)
    row = lax.broadcasted_iota(jnp.int32, (n_cand, tl), 0)
    tail = row - k
    cid = jnp.where(row < k, row,
                    (1 + (tail >> (k2.bit_length() - 1))) * k + (tail & (k2 - 1)))
    work = cand
    for _ in range(k):
        _, hit = extract(work, cid, k * k)
        work = jnp.where(hit, NEG_INF, work)
    sel = (work == NEG_INF).astype(F32)
    zsum = jnp.sum(sel * jnp.exp(cand - cand[0:1, :]), axis=0, keepdims=True)

    cnt = jnp.zeros((N_KEYS, tl), F32)
    for r in range(k):
        lo, hi = (0, k) if r == 0 else (k + (r - 1) * k2, k + r * k2)
        cnt_r = jnp.sum(sel[lo:hi, :], axis=0, keepdims=True)
        cnt = jnp.where(rank_a == float(r), cnt_r, cnt)

    def full_count(x):
        return jnp.sum(x, axis=0, keepdims=True) == float(k)

    ok = (full_count((rank_a < float(k)).astype(F32)) & full_count((rank_b < float(k)).astype(F32))
          & full_count(sel))
    eb = jnp.exp(bsc - bv[0:1, :])
    ea = jnp.exp(a - av[0:1, :]) / zsum
    return rank_b, eb, cnt, ea, ok


def _router_body(sc_ref, rankb_ref, eb_ref, cnt_ref, ea_ref):
    def run(exact):
        rank_b, eb, cnt, ea, ok = _route(sc_ref[0], sc_ref[1], exact)
        rankb_ref[0] = rank_b.astype(BF16)
        eb_ref[0] = eb.astype(BF16)
        cnt_ref[0] = cnt
        ea_ref[0] = ea
        return ok

    ok = run(exact=False)
    n_bad = jnp.sum(jnp.where(ok, 0.0, 1.0), axis=1, keepdims=True)

    @pl.when(n_bad[0, 0] > 0.0)
    def _():
        run(exact=True)


def _router(scores):
    nk, n, t = scores.shape
    tl = TL_ROUTER
    spec = pl.BlockSpec((1, n, tl), lambda h, j: (h, 0, j))
    shape = lambda dt: jax.ShapeDtypeStruct((PK_HEADS, n, t), dt)
    return pl.pallas_call(
        _router_body,
        grid=(PK_HEADS, t // tl),
        in_specs=[pl.BlockSpec((2, n, tl), lambda h, j: (h, 0, j))],
        out_specs=[spec, spec, spec, spec],
        out_shape=[shape(BF16), shape(BF16), shape(F32), shape(F32)],
        compiler_params=_cparams(("parallel", "parallel")),
        name="router",
    )(scores)


def _peer_pre_body(u_ref, hnt_ref, o_ref):
    o_ref[...] = jnp.dot(u_ref[...], hnt_ref[...], preferred_element_type=F32)


def _peer_step(pre_ref, pre_next_ref, hnt_ref, u_ref, vt_ref, rankb_ref, eb_ref, cnt_ref, ea_ref,
               acc_ref, act_ref):
    pre_next_ref[...] = jnp.dot(u_ref[...], hnt_ref[...], preferred_element_type=F32)
    for il in range(NI_PEER):
        rs = slice(il * N_KEYS, (il + 1) * N_KEYS)
        gate = jnp.zeros((N_KEYS, TM_PEER), BF16)
        for h in range(PK_HEADS):
            picked = rankb_ref[h] < cnt_ref[h, il:il + 1, :].astype(BF16)
            gate = gate + jnp.where(picked, eb_ref[h], 0.0) * ea_ref[h, il:il + 1, :].astype(BF16)
        act_ref[rs, :] = _gelu(pre_ref[rs, :]).astype(BF16) * gate
    acc_ref[...] += jnp.dot(vt_ref[...], act_ref[...], preferred_element_type=F32)


def _peer_body(pre0_ref, hnt_ref, u_ref, vt_ref, rankb_ref, eb_ref, cnt_ref, ea_ref, o_ref,
               acc_ref, act_ref, pre_a_ref, pre_b_ref):
    i = pl.program_id(0)
    j = pl.program_id(1)
    args = (hnt_ref, u_ref, vt_ref, rankb_ref, eb_ref, cnt_ref, ea_ref, acc_ref, act_ref)

    @pl.when((i == 0) & (j == 0))
    def _():
        pre_a_ref[...] = pre0_ref[...]

    @pl.when(j == 0)
    def _():
        acc_ref[...] = jnp.zeros(acc_ref.shape, F32)

    @pl.when(lax.rem(j, 2) == 0)
    def _():
        _peer_step(pre_a_ref, pre_b_ref, *args)

    @pl.when(lax.rem(j, 2) == 1)
    def _():
        _peer_step(pre_b_ref, pre_a_ref, *args)

    @pl.when(j == pl.num_programs(1) - 1)
    def _():
        o_ref[...] = acc_ref[...].T


def _peer(hnt, u_bf, vt_bf, rankb, eb, cnt, ea):
    d, t = hnt.shape
    ne = u_bf.shape[0]
    tm, te = TM_PEER, TE_PEER
    ni, nj = t // tm, ne // te
    assert nj % 2 == 0, "pre-activation buffers alternate with the expert-tile index"
    pre0 = pl.pallas_call(
        _peer_pre_body,
        grid=(1,),
        in_specs=[pl.BlockSpec((te, d), lambda i: (0, 0)), pl.BlockSpec((d, tm), lambda i: (0, 0))],
        out_specs=pl.BlockSpec((te, tm), lambda i: (0, 0)),
        out_shape=jax.ShapeDtypeStruct((te, tm), F32),
        compiler_params=_cparams(("arbitrary",)),
        name="peer_pre",
    )(u_bf, hnt)

    def next_i(i, j):
        return jnp.minimum(i + (j + 1) // nj, ni - 1)

    key_spec = pl.BlockSpec((PK_HEADS, N_KEYS, tm), lambda i, j: (0, 0, i))
    blk_spec = pl.BlockSpec((PK_HEADS, NI_PEER, tm), lambda i, j: (0, j, i))
    return pl.pallas_call(
        _peer_body,
        grid=(ni, nj),
        in_specs=[pl.BlockSpec((te, tm), lambda i, j: (0, 0)),
                  pl.BlockSpec((d, tm), lambda i, j: (0, next_i(i, j))),
                  pl.BlockSpec((te, d), lambda i, j: ((j + 1) % nj, 0)),
                  pl.BlockSpec((d, te), lambda i, j: (0, j)),
                  key_spec, key_spec, blk_spec, blk_spec],
        out_specs=pl.BlockSpec((tm, d), lambda i, j: (i, 0)),
        out_shape=jax.ShapeDtypeStruct((t, d), F32),
        scratch_shapes=[pltpu.VMEM((d, tm), F32), pltpu.VMEM((te, tm), BF16),
                        pltpu.VMEM((te, tm), F32), pltpu.VMEM((te, tm), F32)],
        compiler_params=_cparams(("arbitrary", "arbitrary")),
        name="peer",
    )(pre0, hnt, u_bf, vt_bf, rankb, eb, cnt, ea)


def _ple_body(h_ref, po_ref, p_ref, nple_ref, wg_ref, wp_ref, nfin_ref, o_ref):
    h2 = h_ref[...] + po_ref[...]
    gate = jax.nn.sigmoid(
        jnp.dot(_rms(h2, nple_ref[...]).astype(BF16), wg_ref[...], preferred_element_type=F32))
    ple = jnp.dot(p_ref[...].astype(BF16), wp_ref[...], preferred_element_type=F32)
    o_ref[...] = _rms(h2 + gate * ple, nfin_ref[...])


def _ple(h1, peer_out, p, nple, wg, wp, nfin):
    t, d = h1.shape
    tm = TM_DENSE
    full = lambda shape: pl.BlockSpec(shape, lambda i: (0, 0))
    return pl.pallas_call(
        _ple_body,
        grid=(t // tm,),
        in_specs=[pl.BlockSpec((tm, d), lambda i: (i, 0)),
                  pl.BlockSpec((tm, d), lambda i: (i, 0)),
                  pl.BlockSpec((tm, PLE_DIM), lambda i: (i, 0)),
                  full((1, d)), full((d, d)), full((PLE_DIM, d)), full((1, d))],
        out_specs=pl.BlockSpec((tm, d), lambda i: (i, 0)),
        out_shape=jax.ShapeDtypeStruct((t, d), F32),
        compiler_params=_cparams(("parallel",)),
        name="ple",
    )(h1, peer_out, p, nple, wg, wp, nfin)


def _lane_row(vec, offset):
    return jnp.zeros((1, LANE), F32).at[0, offset:offset + vec.shape[0]].set(vec.astype(F32))


def _head_selector(offset):
    sel = np.zeros((LANE, DN_V), np.float32)
    for h in range(DN_HEADS):
        sel[offset + h, h * DN_DV:(h + 1) * DN_DV] = 1.0
    return jnp.asarray(sel, dtype=BF16)


def kernel(x, p, positions, norm_mix, w_in, conv_w, a_log, dt_bias, dn_norm, ret_gn_w, ret_gn_b,
           w_out, norm_ffn, w_pq, sub_keys, expert_u, expert_v, norm_ple, w_ple_gate, w_ple_proj,
           norm_final):
    b, s, d = x.shape
    t = b * s
    depth = w_in.shape[0]
    assert depth == 1, "the final rms_norm is fused into the single layer's ple kernel"
    half = RET_DK // 2
    inv_freq = ROPE_BASE ** (-jnp.arange(half, dtype=F32) / half)
    freq_row = jnp.tile(inv_freq, LANE // half).reshape(1, LANE)
    pos = positions.reshape(t, 1)
    selg, selb = _head_selector(0), _head_selector(DN_HEADS)

    h = x.reshape(t, d)
    for i in range(depth):
        w = w_in[i].astype(BF16)
        w_main = jnp.concatenate([w[:, :AB_OFF], w[:, AB_OFF + 2 * DN_HEADS:]], axis=1)
        w_ab = jnp.pad(w[:, AB_OFF:AB_OFF + 2 * DN_HEADS], ((0, 0), (0, LANE - 2 * DN_HEADS)))
        proj, proj_ab = _inproj(h, norm_mix[i].reshape(1, d), w_main, w_ab)

        qa, kd, u, wv, o0, gend = _dn_local(
            proj, proj_ab, conv_w[i].astype(F32).T, _lane_row(a_log[i], 0), _lane_row(dt_bias[i], 0),
            selg, selb, b, s)
        o_dn = _dn_scan(qa, kd, u, wv, o0, gend, b, s)
        o_r = _ret(proj, pos, freq_row, b, s)
        h1 = _outproj(o_dn, proj, o_r, h, dn_norm[i].reshape(1, DN_DV), ret_gn_w[i].reshape(1, RET_V),
                      ret_gn_b[i].reshape(1, RET_V), w_out[i].astype(BF16))

        keys = sub_keys[i].reshape(2 * PK_HEADS, N_KEYS, PK_DQ // 2).astype(BF16)
        hnt, scores = _pq(h1, norm_ffn[i].reshape(1, d), w_pq[i].astype(BF16), keys)
        rankb, eb, cnt, ea = _router(scores)
        peer_out = _peer(hnt, expert_u[i].astype(BF16), expert_v[i].T.astype(BF16), rankb, eb, cnt, ea)

        h = _ple(h1, peer_out, p[i].reshape(t, PLE_DIM), norm_ple[i].reshape(1, d),
                 w_ple_gate[i].astype(BF16), w_ple_proj[i].astype(BF16), norm_final.reshape(1, d))
    return h.reshape(b, s, d)
```

```python
import functools
import math

import numpy as np
import jax
import jax.numpy as jnp
from jax import lax
from jax.experimental import pallas as pl
from jax.experimental.pallas import tpu as pltpu

F32 = jnp.float32
BF16 = jnp.bfloat16
F8 = jnp.float8_e4m3fn
F8_TARGET = 224.0
F8_TINY = 1e-30
NEG_INF = float("-inf")

EPS = 1e-6
D_MODEL = 2048
DN_HEADS = 8
DN_DK = 128
DN_DV = 128
CONV_W = 4
RET_HEADS = 8
RET_DK = 64
RET_DV = 128
CHUNK = 64
ROPE_BASE = 10000.0
N_KEYS = 128
PK_HEADS = 8
PK_DQ = 256
PK_TOPK = 16
PLE_DIM = 256

DN_QK = DN_HEADS * DN_DK
DN_V = DN_HEADS * DN_DV
CONV_CH = 2 * DN_QK + DN_V
RET_QK = RET_HEADS * RET_DK
RET_V = RET_HEADS * RET_DV
AB_OFF = CONV_CH + DN_V
MAIN_COLS = CONV_CH + DN_V + 2 * RET_QK + 2 * RET_V
LANE = 128
HALO = 8

TM_INPROJ = 1024
TN_INPROJ = 512
TT_SEQ = 256
TM_DENSE = 256
TL_ROUTER = 256
TM_PEER = 512
NI_PEER = 8
TE_PEER = NI_PEER * N_KEYS
VMEM_LIMIT = 56 * 1024 * 1024


def _cparams(sem, flags=None):
    return pltpu.CompilerParams(dimension_semantics=sem, vmem_limit_bytes=VMEM_LIMIT, flags=flags)


def _rms(x, w):
    return x * lax.rsqrt(jnp.mean(x * x, axis=-1, keepdims=True) + EPS) * w


def _silu(x):
    return x * jax.nn.sigmoid(x)


def _softplus(x):
    return jnp.maximum(x, 0.0) + jnp.log1p(jnp.exp(-jnp.abs(x)))


def _gelu(x):
    return 0.5 * x * (1.0 + lax.erf(x * (2.0 ** -0.5)))


def _mm(a, b):
    return jnp.dot(a.astype(BF16), b.astype(BF16), preferred_element_type=F32)


def _mm_nt(a, b):
    return lax.dot_general(a.astype(BF16), b.astype(BF16), (((1,), (1,)), ((), ())),
                           preferred_element_type=F32)


def _mm_tn(a, b):
    return lax.dot_general(a.astype(BF16), b.astype(BF16), (((0,), (0,)), ((), ())),
                           preferred_element_type=F32)


def _split3(x):
    hi = x.astype(BF16)
    rest = x - hi.astype(F32)
    mid = rest.astype(BF16)
    lo = (rest - mid.astype(F32)).astype(BF16)
    return hi, mid, lo


def _select_mm(a, b, dims=(((1,), (0,)), ((), ()))):
    if a.dtype == BF16:
        parts = [lax.dot_general(a, p, dims, preferred_element_type=F32) for p in _split3(b)]
    else:
        parts = [lax.dot_general(p, b, dims, preferred_element_type=F32) for p in _split3(a)]
    return parts[0] + parts[1] + parts[2]


def _inproj_body(x_ref, nw_ref, w_ref, wab_ref, o_ref, oab_ref, hn_ref):
    @pl.when(pl.program_id(1) == 0)
    def _():
        hn_ref[...] = _rms(x_ref[...], nw_ref[...]).astype(BF16)
        oab_ref[...] = jnp.dot(hn_ref[...], wab_ref[...], preferred_element_type=F32)

    o_ref[...] = jnp.dot(hn_ref[...], w_ref[...], preferred_element_type=F32)


def _inproj(x, nw, w_main, w_ab):
    t, d = x.shape
    n = w_main.shape[1]
    return pl.pallas_call(
        _inproj_body,
        grid=(t // TM_INPROJ, n // TN_INPROJ),
        in_specs=[
            pl.BlockSpec((TM_INPROJ, d), lambda i, j: (i, 0)),
            pl.BlockSpec((1, d), lambda i, j: (0, 0)),
            pl.BlockSpec((d, TN_INPROJ), lambda i, j: (0, j)),
            pl.BlockSpec((d, LANE), lambda i, j: (0, 0)),
        ],
        out_specs=[
            pl.BlockSpec((TM_INPROJ, TN_INPROJ), lambda i, j: (i, j)),
            pl.BlockSpec((TM_INPROJ, LANE), lambda i, j: (i, 0)),
        ],
        out_shape=[jax.ShapeDtypeStruct((t, n), F32), jax.ShapeDtypeStruct((t, LANE), F32)],
        scratch_shapes=[pltpu.VMEM((TM_INPROJ, d), BF16)],
        compiler_params=_cparams(("parallel", "arbitrary")),
        name="inproj",
    )(x, nw, w_main, w_ab)


def _dn_local_body(q_ref, k_ref, v_ref, qh_ref, kh_ref, vh_ref, ab_ref, cw_ref, alog_ref, dtb_ref,
                   selg_ref, selb_ref, qa_ref, kd_ref, u_ref, w_ref, o0_ref, gend_ref, buf_ref):
    tt = TT_SEQ
    first_tile = pl.program_id(1) == 0

    def conv_silu(main_ref, halo_ref, c0):
        halo = halo_ref[...]
        buf_ref[0:HALO, :] = jnp.where(first_tile, jnp.zeros_like(halo), halo)
        buf_ref[HALO:HALO + tt, :] = main_ref[...]
        y = None
        for j in range(CONV_W):
            off = HALO - (CONV_W - 1) + j
            term = buf_ref[off:off + tt, :] * cw_ref[j:j + 1, c0:c0 + DN_QK]
            y = term if y is None else y + term
        return _silu(y)

    q = conv_silu(q_ref, qh_ref, 0)
    k = conv_silu(k_ref, kh_ref, DN_QK)
    v = conv_silu(v_ref, vh_ref, 2 * DN_QK)

    row = lax.broadcasted_iota(jnp.int32, (tt, tt), 0)
    col = lax.broadcasted_iota(jnp.int32, (tt, tt), 1)

    def same_block(bits):
        return (row >> bits) == (col >> bits)

    blk8, blk16, blk32, blk64 = same_block(3), same_block(4), same_block(5), same_block(6)
    causal = blk64 & (col <= row)
    strict = blk64 & (col < row)
    eye = (row == col).astype(F32)

    ab = ab_ref[...]
    g = -jnp.exp(alog_ref[...]) * _softplus(ab + dtb_ref[...])
    beta = jax.nn.sigmoid(ab)
    def ones_where(mask):
        return jnp.where(mask, 1.0, 0.0).astype(BF16)

    gc = _select_mm(ones_where(causal), g)
    gtot = _select_mm(ones_where(blk64), g)
    e_gc = _select_mm(gc, selg_ref[...])
    e_gt = _select_mm(gtot, selg_ref[...])
    e_beta = _select_mm(beta, selb_ref[...])
    r128 = lax.broadcasted_iota(jnp.int32, (LANE, LANE), 0)
    c128 = lax.broadcasted_iota(jnp.int32, (LANE, LANE), 1)
    gc_t = _select_mm(ones_where(r128 == c128), gc, (((1,), (1,)), ((), ())))
    eg = jnp.exp(e_gc)
    kdec = jnp.exp(e_gt - e_gc)

    for c in range(tt // CHUNK):
        gend_ref[0, c:c + 1, :] = jnp.exp(e_gt[c * CHUNK:c * CHUNK + 1, :])
    gend_ref[0, tt // CHUNK:, :] = jnp.zeros((HALO - tt // CHUNK, DN_V), F32)

    heads = range(DN_HEADS)
    hsl = [slice(h * DN_DK, (h + 1) * DN_DK) for h in heads]

    def per_head(fn):
        return [fn(h) for h in heads]

    def l2n(x):
        return x * lax.rsqrt(jnp.sum(x * x, axis=-1, keepdims=True) + EPS)

    qh = per_head(lambda h: l2n(q[:, hsl[h]]) * (DN_DK ** -0.5))
    kh = per_head(lambda h: l2n(k[:, hsl[h]]))
    kb = per_head(lambda h: kh[h] * e_beta[:, hsl[h]])
    dmat = per_head(lambda h: jnp.exp(jnp.where(
        causal, jnp.concatenate([e_gc[:, hsl[h]]] * (tt // LANE), axis=1) - gc_t[h:h + 1, :], NEG_INF)))
    scores = per_head(lambda h: _mm_nt(jnp.concatenate([kb[h], qh[h]], axis=0), kh[h]))
    nmat = per_head(lambda h: jnp.where(strict, scores[h][:tt] * dmat[h], 0.0))
    attn = per_head(lambda h: scores[h][tt:] * dmat[h])

    n0 = per_head(lambda h: jnp.where(blk8, nmat[h], 0.0))
    n2 = per_head(lambda h: _mm(n0[h], n0[h]))
    n4 = per_head(lambda h: _mm(n2[h], n2[h]))
    inv = per_head(lambda h: eye - n0[h])
    inv = per_head(lambda h: inv[h] + _mm(inv[h], n2[h]))
    inv = per_head(lambda h: inv[h] + _mm(inv[h], n4[h]))
    for inner, outer in ((blk8, blk16), (blk16, blk32), (blk32, blk64)):
        level = outer & jnp.logical_not(inner)
        prod = per_head(lambda h: _mm(inv[h], jnp.where(level, nmat[h], 0.0)))
        inv = per_head(lambda h: inv[h] - _mm(prod[h], inv[h]))

    sol = per_head(lambda h: _mm(inv[h], jnp.concatenate(
        [v[:, hsl[h]] * e_beta[:, hsl[h]], kb[h] * eg[:, hsl[h]]], axis=1)))
    asol = per_head(lambda h: _mm(attn[h], sol[h]))
    for h in heads:
        hs = hsl[h]
        u_ref[:, hs] = sol[h][:, :DN_DV]
        w_ref[:, hs] = sol[h][:, DN_DV:].astype(BF16)
        o0_ref[:, hs] = asol[h][:, :DN_DV]
        qa_ref[:, hs] = (qh[h] * eg[:, hs] - asol[h][:, DN_DV:]).astype(BF16)
        kd_ref[:, hs] = (kh[h] * kdec[:, hs]).astype(BF16)


def _dn_local(proj, proj_ab, cw, alog_row, dtb_row, selg, selb, b, s):
    t = b * s
    nt = s // TT_SEQ
    rows_per_halo = TT_SEQ // HALO

    def main_spec(cb):
        return pl.BlockSpec((TT_SEQ, DN_QK), lambda bi, i: (bi * nt + i, cb))

    def halo_spec(cb):
        return pl.BlockSpec(
            (HALO, DN_QK), lambda bi, i: (jnp.maximum((bi * nt + i) * rows_per_halo - 1, 0), cb))

    def full(shape):
        return pl.BlockSpec(shape, lambda bi, i: (0,) * len(shape))

    tok_spec = pl.BlockSpec((TT_SEQ, DN_V), lambda bi, i: (bi * nt + i, 0))
    return pl.pallas_call(
        _dn_local_body,
        grid=(b, nt),
        in_specs=[main_spec(0), main_spec(1), main_spec(2), halo_spec(0), halo_spec(1), halo_spec(2),
                  pl.BlockSpec((TT_SEQ, LANE), lambda bi, i: (bi * nt + i, 0)),
                  full((CONV_W, CONV_CH)), full((1, LANE)), full((1, LANE)),
                  full((LANE, DN_V)), full((LANE, DN_V))],
        out_specs=[tok_spec, tok_spec, tok_spec, tok_spec, tok_spec,
                   pl.BlockSpec((1, HALO, DN_V), lambda bi, i: (bi * nt + i, 0, 0))],
        out_shape=[jax.ShapeDtypeStruct((t, DN_V), BF16),
                   jax.ShapeDtypeStruct((t, DN_V), BF16),
                   jax.ShapeDtypeStruct((t, DN_V), F32),
                   jax.ShapeDtypeStruct((t, DN_V), BF16),
                   jax.ShapeDtypeStruct((t, DN_V), F32),
                   jax.ShapeDtypeStruct((t // TT_SEQ, HALO, DN_V), F32)],
        scratch_shapes=[pltpu.VMEM((HALO + TT_SEQ, DN_QK), F32)],
        compiler_params=_cparams(("parallel", "parallel")),
        name="dn_local",
    )(proj, proj, proj, proj, proj, proj, proj_ab, cw, alog_row, dtb_row, selg, selb)


def _dn_scan_body(qa_ref, kd_ref, u_ref, w_ref, o0_ref, gend_ref, o_ref, s_ref):
    nb = qa_ref.shape[0]

    @pl.when(pl.program_id(0) == 0)
    def _():
        s_ref[...] = jnp.zeros(s_ref.shape, F32)

    chains = [(b, h, slice(h * DN_DK, (h + 1) * DN_DK)) for b in range(nb) for h in range(DN_HEADS)]
    states = [s_ref[n] for n in range(len(chains))]
    for c in range(TT_SEQ // CHUNK):
        rs = slice(c * CHUNK, (c + 1) * CHUNK)
        both = [jnp.dot(jnp.concatenate([w_ref[b, rs, hs], qa_ref[b, rs, hs]], axis=0),
                        states[n].astype(BF16), preferred_element_type=F32)
                for n, (b, h, hs) in enumerate(chains)]
        for n, (b, h, hs) in enumerate(chains):
            o_ref[b, rs, hs] = both[n][CHUNK:] + o0_ref[b, rs, hs]
        states = [states[n] * gend_ref[b, 0, c:c + 1, hs]
                  + _mm_tn(kd_ref[b, rs, hs], u_ref[b, rs, hs] - both[n][:CHUNK])
                  for n, (b, h, hs) in enumerate(chains)]
    for n in range(len(chains)):
        s_ref[n] = states[n]


def _dn_scan(qa, kd, u, w, o0, gend, b, s):
    nt = s // TT_SEQ
    tok = lambda a: a.reshape(b, s, DN_V)
    tok_spec = pl.BlockSpec((b, TT_SEQ, DN_V), lambda i: (0, i, 0))
    out = pl.pallas_call(
        _dn_scan_body,
        grid=(nt,),
        in_specs=[tok_spec, tok_spec, tok_spec, tok_spec, tok_spec,
                  pl.BlockSpec((b, 1, HALO, DN_V), lambda i: (0, i, 0, 0))],
        out_specs=tok_spec,
        out_shape=jax.ShapeDtypeStruct((b, s, DN_V), F32),
        scratch_shapes=[pltpu.VMEM((b * DN_HEADS, DN_DK, DN_DV), F32)],
        compiler_params=_cparams(("arbitrary",)),
        name="dn_scan",
    )(tok(qa), tok(kd), tok(u), tok(w), tok(o0), gend.reshape(b, nt, HALO, DN_V))
    return out.reshape(b * s, DN_V)


def _log_gamma(h):
    return math.log1p(-(2.0 ** (-5.0 - h)))


def _ret_body(qk_ref, v_ref, pos_ref, freq_ref, o_ref, s_ref):
    tt = TT_SEQ

    @pl.when(pl.program_id(1) == 0)
    def _():
        s_ref[...] = jnp.zeros(s_ref.shape, F32)

    ang = pos_ref[...].astype(F32) * freq_ref[...]
    lane = lax.broadcasted_iota(jnp.int32, (tt, LANE), 1)
    first_half = (lane & (RET_DK // 2)) == 0
    cos = jnp.cos(ang)
    sin = jnp.sin(ang)
    ssin = jnp.where(first_half, -sin, sin)

    def rotary(x):
        swapped = jnp.where(first_half, pltpu.roll(x, LANE - RET_DK // 2, axis=1),
                            pltpu.roll(x, RET_DK // 2, axis=1))
        return x * cos + swapped * ssin

    row = lax.broadcasted_iota(jnp.int32, (tt, tt), 0)
    col = lax.broadcasted_iota(jnp.int32, (tt, tt), 1)
    rel = (row - col).astype(F32)
    causal = row >= col
    trow = lax.broadcasted_iota(jnp.int32, (tt, LANE), 0).astype(F32)
    srow = lax.broadcasted_iota(jnp.int32, (LANE, LANE), 0)

    for pr in range(RET_HEADS // 2):
        ps = slice(pr * LANE, (pr + 1) * LANE)
        qp = rotary(qk_ref[:, ps])
        kp = rotary(qk_ref[:, RET_QK + pr * LANE:RET_QK + (pr + 1) * LANE]) * (RET_DK ** -0.5)
        kpb = kp.astype(BF16)
        state = s_ref[pr]
        sb = state.astype(BF16)
        update = jnp.zeros((LANE, RET_DV), F32)
        for hh in range(2):
            h = 2 * pr + hh
            lg = _log_gamma(h)
            mine = (lane < RET_DK) if hh == 0 else (lane >= RET_DK)
            qm = jnp.where(mine, qp, 0.0).astype(BF16)
            vh = v_ref[:, h * RET_DV:(h + 1) * RET_DV].astype(BF16)
            att = _mm_nt(qm, kpb) * jnp.exp(jnp.where(causal, rel * lg, NEG_INF))
            inner = jnp.dot(att.astype(BF16), vh, preferred_element_type=F32)
            cross = jnp.dot(qm, sb, preferred_element_type=F32) * jnp.exp((trow + 1.0) * lg)
            o_ref[:, h * RET_DV:(h + 1) * RET_DV] = inner + cross
            kz = jnp.where(mine, kp, 0.0) * jnp.exp((tt - 1.0 - trow) * lg)
            update = update + _mm_tn(kz, vh)
        decay = jnp.where(srow < RET_DK, math.exp(tt * _log_gamma(2 * pr)),
                          math.exp(tt * _log_gamma(2 * pr + 1)))
        s_ref[pr] = state * decay + update


def _ret(proj, pos, freq_row, b, s):
    t = b * s
    nt = s // TT_SEQ
    return pl.pallas_call(
        _ret_body,
        grid=(b, nt),
        in_specs=[pl.BlockSpec((TT_SEQ, 2 * RET_QK), lambda bi, i: (bi * nt + i, 4)),
                  pl.BlockSpec((TT_SEQ, RET_V), lambda bi, i: (bi * nt + i, 5)),
                  pl.BlockSpec((TT_SEQ, 1), lambda bi, i: (bi * nt + i, 0)),
                  pl.BlockSpec((1, LANE), lambda bi, i: (0, 0))],
        out_specs=pl.BlockSpec((TT_SEQ, RET_V), lambda bi, i: (bi * nt + i, 0)),
        out_shape=jax.ShapeDtypeStruct((t, RET_V), F32),
        scratch_shapes=[pltpu.VMEM((RET_HEADS // 2, LANE, RET_DV), F32)],
        compiler_params=_cparams(("parallel", "arbitrary")),
        name="ret",
    )(proj, proj, pos, freq_row)


def _outproj_body(odn_ref, z_ref, or_ref, rg_ref, x_ref, dnw_ref, gnw_ref, gnb_ref, w_ref, o_ref,
                  mix_ref):
    for h in range(DN_HEADS):
        hs = slice(h * DN_DV, (h + 1) * DN_DV)
        mix_ref[:, hs] = (_rms(odn_ref[:, hs], dnw_ref[...]) * _silu(z_ref[:, hs])).astype(BF16)
    for h in range(RET_HEADS):
        hs = slice(h * RET_DV, (h + 1) * RET_DV)
        o = or_ref[:, hs]
        cen = o - jnp.mean(o, axis=-1, keepdims=True)
        y = cen * lax.rsqrt(jnp.mean(cen * cen, axis=-1, keepdims=True) + EPS)
        y = (y * gnw_ref[:, hs] + gnb_ref[:, hs]) * _silu(rg_ref[:, hs])
        mix_ref[:, DN_V + h * RET_DV:DN_V + (h + 1) * RET_DV] = y.astype(BF16)
    o_ref[...] = x_ref[...] + jnp.dot(mix_ref[...], w_ref[...], preferred_element_type=F32)


def _outproj(o_dn, proj, o_r, x, dnw, gnw, gnb, w_out):
    t, d = x.shape
    tm = TM_DENSE
    full = lambda shape: pl.BlockSpec(shape, lambda i: (0, 0))
    return pl.pallas_call(
        _outproj_body,
        grid=(t // tm,),
        in_specs=[pl.BlockSpec((tm, DN_V), lambda i: (i, 0)),
                  pl.BlockSpec((tm, DN_V), lambda i: (i, 3)),
                  pl.BlockSpec((tm, RET_V), lambda i: (i, 0)),
                  pl.BlockSpec((tm, RET_V), lambda i: (i, 6)),
                  pl.BlockSpec((tm, d), lambda i: (i, 0)),
                  full((1, DN_DV)), full((1, RET_V)), full((1, RET_V)), full((DN_V + RET_V, d))],
        out_specs=pl.BlockSpec((tm, d), lambda i: (i, 0)),
        out_shape=jax.ShapeDtypeStruct((t, d), F32),
        scratch_shapes=[pltpu.VMEM((tm, DN_V + RET_V), BF16)],
        compiler_params=_cparams(("parallel",)),
        name="outproj",
    )(o_dn, proj, o_r, proj, x, dnw, gnw, gnb, w_out)


def _pq_body(h_ref, nw_ref, w_ref, keys_ref, hnt_ref, inv_ref, sc_ref):
    hn = _rms(h_ref[...], nw_ref[...])
    hnt = hn.T
    amax = jnp.maximum(jnp.max(jnp.abs(hnt), axis=0, keepdims=True), F8_TINY)
    hnt_ref[...] = (hnt * (F8_TARGET / amax)).astype(F8)
    inv_ref[...] = amax * (1.0 / F8_TARGET)
    q = jnp.dot(hn.astype(BF16), w_ref[...], preferred_element_type=F32)
    half = PK_DQ // 2
    for hp in range(2 * PK_HEADS):
        sc_ref[hp] = _mm_nt(keys_ref[hp], q[:, hp * half:(hp + 1) * half])


def _pq(h1, nw, w_pq, keys):
    t, d = h1.shape
    tm = TM_DENSE
    nk = 2 * PK_HEADS
    return pl.pallas_call(
        _pq_body,
        grid=(t // tm,),
        in_specs=[pl.BlockSpec((tm, d), lambda i: (i, 0)),
                  pl.BlockSpec((1, d), lambda i: (0, 0)),
                  pl.BlockSpec((d, PK_HEADS * PK_DQ), lambda i: (0, 0)),
                  pl.BlockSpec((nk, N_KEYS, PK_DQ // 2), lambda i: (0, 0, 0))],
        out_specs=[pl.BlockSpec((d, tm), lambda i: (0, i)),
                   pl.BlockSpec((1, tm), lambda i: (0, i)),
                   pl.BlockSpec((nk, N_KEYS, tm), lambda i: (0, 0, i))],
        out_shape=[jax.ShapeDtypeStruct((d, t), F8),
                   jax.ShapeDtypeStruct((1, t), F32),
                   jax.ShapeDtypeStruct((nk, N_KEYS, t), F32)],
        compiler_params=_cparams(("parallel",)),
        name="pq",
    )(h1, nw, w_pq, keys)


def _route(a, bsc, exact):
    tl = a.shape[1]
    k = PK_TOPK
    keyid = lax.broadcasted_iota(jnp.int32, (N_KEYS, tl), 0)
    slot = lax.broadcasted_iota(jnp.int32, (k, tl), 0)

    def extract(s, ids, n_ids):
        m = jnp.max(s, axis=0, keepdims=True)
        hit = s == m
        if exact:
            hit = ids == jnp.min(jnp.where(hit, ids, n_ids), axis=0, keepdims=True)
        return m, hit

    def top_k(s):
        rank = jnp.full((N_KEYS, tl), float(k), F32)
        vals = jnp.zeros((k, tl), F32)
        for r in range(k):
            m, hit = extract(s, keyid, N_KEYS)
            rank = jnp.where(hit, float(r), rank)
            s = jnp.where(hit, NEG_INF, s)
            vals = jnp.where(slot == r, m, vals)
        return vals, rank

    av, rank_a = top_k(a)
    bv, rank_b = top_k(bsc)

    k2 = k // 2
    assert k2 & (k2 - 1) == 0
    n_cand = k + (k - 1) * k2
    cand = jnp.concatenate([av[0:1, :] + bv] + [av[r:r + 1, :] + bv[0:k2, :] for r in range(1, k)],
                           axis=0)
    row = lax.broadcasted_iota(jnp.int32, (n_cand, tl), 0)
    tail = row - k
    cid = jnp.where(row < k, row,
                    (1 + (tail >> (k2.bit_length() - 1))) * k + (tail & (k2 - 1)))
    work = cand
    for _ in range(k):
        _, hit = extract(work, cid, k * k)
        work = jnp.where(hit, NEG_INF, work)
    sel = (work == NEG_INF).astype(F32)
    zsum = jnp.sum(sel * jnp.exp(cand - cand[0:1, :]), axis=0, keepdims=True)

    cnt = jnp.zeros((N_KEYS, tl), F32)
    for r in range(k):
        lo, hi = (0, k) if r == 0 else (k + (r - 1) * k2, k + r * k2)
        cnt_r = jnp.sum(sel[lo:hi, :], axis=0, keepdims=True)
        cnt = jnp.where(rank_a == float(r), cnt_r, cnt)

    def full_count(x):
        return jnp.sum(x, axis=0, keepdims=True) == float(k)

    ok = (full_count((rank_a < float(k)).astype(F32)) & full_count((rank_b < float(k)).astype(F32))
          & full_count(sel))
    eb = jnp.exp(bsc - bv[0:1, :])
    ea = jnp.exp(a - av[0:1, :]) / zsum
    return rank_b, eb, cnt, ea, ok


def _router_body(sc_ref, rankb_ref, eb_ref, cnt_ref, ea_ref):
    def run(exact):
        rank_b, eb, cnt, ea, ok = _route(sc_ref[0], sc_ref[1], exact)
        rankb_ref[0] = rank_b.astype(BF16)
        eb_ref[0] = eb.astype(BF16)
        cnt_ref[0] = cnt
        ea_ref[0] = ea
        return ok

    ok = run(exact=False)
    n_bad = jnp.sum(jnp.where(ok, 0.0, 1.0), axis=1, keepdims=True)

    @pl.when(n_bad[0, 0] > 0.0)
    def _():
        run(exact=True)


def _router(scores):
    nk, n, t = scores.shape
    tl = TL_ROUTER
    spec = pl.BlockSpec((1, n, tl), lambda h, j: (h, 0, j))
    shape = lambda dt: jax.ShapeDtypeStruct((PK_HEADS, n, t), dt)
    return pl.pallas_call(
        _router_body,
        grid=(PK_HEADS, t // tl),
        in_specs=[pl.BlockSpec((2, n, tl), lambda h, j: (h, 0, j))],
        out_specs=[spec, spec, spec, spec],
        out_shape=[shape(BF16), shape(BF16), shape(F32), shape(F32)],
        compiler_params=_cparams(("parallel", "parallel")),
        name="router",
    )(scores)


def _u_prep_body(u_ref, o_ref, inv_ref):
    u = u_ref[...]
    amax = jnp.max(jnp.max(jnp.abs(u), axis=0, keepdims=True), axis=1, keepdims=True)
    amax = jnp.maximum(amax, F8_TINY)
    o_ref[...] = (u * (F8_TARGET / amax)).astype(F8)
    inv_ref[0] = jnp.broadcast_to(amax * (1.0 / F8_TARGET), inv_ref.shape[1:])


def _u_prep(u):
    ne, d = u.shape
    te, tm = TE_PEER, TM_PEER
    return pl.pallas_call(
        _u_prep_body,
        grid=(ne // te,),
        in_specs=[pl.BlockSpec((te, d), lambda j: (j, 0))],
        out_specs=[pl.BlockSpec((te, d), lambda j: (j, 0)), pl.BlockSpec((1, 1, tm), lambda j: (j, 0, 0))],
        out_shape=[jax.ShapeDtypeStruct((ne, d), F8), jax.ShapeDtypeStruct((ne // te, 1, tm), F32)],
        compiler_params=_cparams(("parallel",)),
        name="u_prep",
    )(u)


def _peer_pre_body(u_ref, hnt_ref, o_ref):
    o_ref[...] = jnp.dot(u_ref[...], hnt_ref[...], preferred_element_type=F32)


def _peer_step(pre_ref, pre_next_ref, hnt_ref, u_ref, vt_ref, rankb_ref, eb_ref, cnt_ref, ea_ref,
               xinv_ref, uinv_ref, acc_ref, act_ref):
    pre_next_ref[...] = jnp.dot(u_ref[...], hnt_ref[...], preferred_element_type=F32)
    unscale = xinv_ref[...] * uinv_ref[0]
    for il in range(NI_PEER):
        rs = slice(il * N_KEYS, (il + 1) * N_KEYS)
        gate = jnp.zeros((N_KEYS, TM_PEER), BF16)
        for h in range(PK_HEADS):
            picked = rankb_ref[h] < cnt_ref[h, il:il + 1, :].astype(BF16)
            gate = gate + jnp.where(picked, eb_ref[h], 0.0) * ea_ref[h, il:il + 1, :].astype(BF16)
        act_ref[rs, :] = _gelu(pre_ref[rs, :] * unscale).astype(BF16) * gate
    acc_ref[...] += jnp.dot(vt_ref[...], act_ref[...], preferred_element_type=F32)


def _peer_body(pre0_ref, hnt_ref, u_ref, vt_ref, rankb_ref, eb_ref, cnt_ref, ea_ref, xinv_ref,
               uinv_ref, o_ref, acc_ref, act_ref, pre_a_ref, pre_b_ref):
    i = pl.program_id(0)
    j = pl.program_id(1)
    args = (hnt_ref, u_ref, vt_ref, rankb_ref, eb_ref, cnt_ref, ea_ref, xinv_ref, uinv_ref, acc_ref,
            act_ref)

    @pl.when((i == 0) & (j == 0))
    def _():
        pre_a_ref[...] = pre0_ref[...]

    @pl.when(j == 0)
    def _():
        acc_ref[...] = jnp.zeros(acc_ref.shape, F32)

    @pl.when(lax.rem(j, 2) == 0)
    def _():
        _peer_step(pre_a_ref, pre_b_ref, *args)

    @pl.when(lax.rem(j, 2) == 1)
    def _():
        _peer_step(pre_b_ref, pre_a_ref, *args)

    @pl.when(j == pl.num_programs(1) - 1)
    def _():
        o_ref[...] = acc_ref[...].T


def _peer(hnt, xinv, u_bf, uinv, vt_bf, rankb, eb, cnt, ea):
    d, t = hnt.shape
    ne = u_bf.shape[0]
    tm, te = TM_PEER, TE_PEER
    ni, nj = t // tm, ne // te
    assert nj % 2 == 0, "pre-activation buffers alternate with the expert-tile index"
    pre0 = pl.pallas_call(
        _peer_pre_body,
        grid=(1,),
        in_specs=[pl.BlockSpec((te, d), lambda i: (0, 0)), pl.BlockSpec((d, tm), lambda i: (0, 0))],
        out_specs=pl.BlockSpec((te, tm), lambda i: (0, 0)),
        out_shape=jax.ShapeDtypeStruct((te, tm), F32),
        compiler_params=_cparams(("arbitrary",)),
        name="peer_pre",
    )(u_bf, hnt)

    def next_i(i, j):
        return jnp.minimum(i + (j + 1) // nj, ni - 1)

    key_spec = pl.BlockSpec((PK_HEADS, N_KEYS, tm), lambda i, j: (0, 0, i))
    blk_spec = pl.BlockSpec((PK_HEADS, NI_PEER, tm), lambda i, j: (0, j, i))
    return pl.pallas_call(
        _peer_body,
        grid=(ni, nj),
        in_specs=[pl.BlockSpec((te, tm), lambda i, j: (0, 0)),
                  pl.BlockSpec((d, tm), lambda i, j: (0, next_i(i, j))),
                  pl.BlockSpec((te, d), lambda i, j: ((j + 1) % nj, 0)),
                  pl.BlockSpec((d, te), lambda i, j: (0, j)),
                  key_spec, key_spec, blk_spec, blk_spec,
                  pl.BlockSpec((1, tm), lambda i, j: (0, i)),
                  pl.BlockSpec((1, 1, tm), lambda i, j: (j, 0, 0))],
        out_specs=pl.BlockSpec((tm, d), lambda i, j: (i, 0)),
        out_shape=jax.ShapeDtypeStruct((t, d), F32),
        scratch_shapes=[pltpu.VMEM((d, tm), F32), pltpu.VMEM((te, tm), BF16),
                        pltpu.VMEM((te, tm), F32), pltpu.VMEM((te, tm), F32)],
        compiler_params=_cparams(("arbitrary", "arbitrary")),
        name="peer",
    )(pre0, hnt, u_bf, vt_bf, rankb, eb, cnt, ea, xinv, uinv)


def _ple_body(h_ref, po_ref, p_ref, nple_ref, wg_ref, wp_ref, nfin_ref, o_ref):
    h2 = h_ref[...] + po_ref[...]
    gate = jax.nn.sigmoid(
        jnp.dot(_rms(h2, nple_ref[...]).astype(BF16), wg_ref[...], preferred_element_type=F32))
    ple = jnp.dot(p_ref[...].astype(BF16), wp_ref[...], preferred_element_type=F32)
    o_ref[...] = _rms(h2 + gate * ple, nfin_ref[...])


def _ple(h1, peer_out, p, nple, wg, wp, nfin):
    t, d = h1.shape
    tm = TM_DENSE
    full = lambda shape: pl.BlockSpec(shape, lambda i: (0, 0))
    return pl.pallas_call(
        _ple_body,
        grid=(t // tm,),
        in_specs=[pl.BlockSpec((tm, d), lambda i: (i, 0)),
                  pl.BlockSpec((tm, d), lambda i: (i, 0)),
                  pl.BlockSpec((tm, PLE_DIM), lambda i: (i, 0)),
                  full((1, d)), full((d, d)), full((PLE_DIM, d)), full((1, d))],
        out_specs=pl.BlockSpec((tm, d), lambda i: (i, 0)),
        out_shape=jax.ShapeDtypeStruct((t, d), F32),
        compiler_params=_cparams(("parallel",)),
        name="ple",
    )(h1, peer_out, p, nple, wg, wp, nfin)


def _lane_row(vec, offset):
    return jnp.zeros((1, LANE), F32).at[0, offset:offset + vec.shape[0]].set(vec.astype(F32))


def _head_selector(offset):
    sel = np.zeros((LANE, DN_V), np.float32)
    for h in range(DN_HEADS):
        sel[offset + h, h * DN_DV:(h + 1) * DN_DV] = 1.0
    return jnp.asarray(sel, dtype=BF16)


def kernel(x, p, positions, norm_mix, w_in, conv_w, a_log, dt_bias, dn_norm, ret_gn_w, ret_gn_b,
           w_out, norm_ffn, w_pq, sub_keys, expert_u, expert_v, norm_ple, w_ple_gate, w_ple_proj,
           norm_final):
    b, s, d = x.shape
    t = b * s
    depth = w_in.shape[0]
    assert depth == 1, "the final rms_norm is fused into the single layer's ple kernel"
    half = RET_DK // 2
    inv_freq = ROPE_BASE ** (-jnp.arange(half, dtype=F32) / half)
    freq_row = jnp.tile(inv_freq, LANE // half).reshape(1, LANE)
    pos = positions.reshape(t, 1)
    selg, selb = _head_selector(0), _head_selector(DN_HEADS)

    h = x.reshape(t, d)
    for i in range(depth):
        w = w_in[i].astype(BF16)
        w_main = jnp.concatenate([w[:, :AB_OFF], w[:, AB_OFF + 2 * DN_HEADS:]], axis=1)
        w_ab = jnp.pad(w[:, AB_OFF:AB_OFF + 2 * DN_HEADS], ((0, 0), (0, LANE - 2 * DN_HEADS)))
        proj, proj_ab = _inproj(h, norm_mix[i].reshape(1, d), w_main, w_ab)

        qa, kd, u, wv, o0, gend = _dn_local(
            proj, proj_ab, conv_w[i].astype(F32).T, _lane_row(a_log[i], 0), _lane_row(dt_bias[i], 0),
            selg, selb, b, s)
        o_dn = _dn_scan(qa, kd, u, wv, o0, gend, b, s)
        o_r = _ret(proj, pos, freq_row, b, s)
        h1 = _outproj(o_dn, proj, o_r, h, dn_norm[i].reshape(1, DN_DV), ret_gn_w[i].reshape(1, RET_V),
                      ret_gn_b[i].reshape(1, RET_V), w_out[i].astype(BF16))

        keys = sub_keys[i].reshape(2 * PK_HEADS, N_KEYS, PK_DQ // 2).astype(BF16)
        hnt, xinv, scores = _pq(h1, norm_ffn[i].reshape(1, d), w_pq[i].astype(BF16), keys)
        rankb, eb, cnt, ea = _router(scores)
        u_f8, uinv = _u_prep(expert_u[i])
        peer_out = _peer(hnt, xinv, u_f8, uinv, expert_v[i].T.astype(BF16), rankb, eb, cnt, ea)

        h = _ple(h1, peer_out, p[i].reshape(t, PLE_DIM), norm_ple[i].reshape(1, d),
                 w_ple_gate[i].astype(BF16), w_ple_proj[i].astype(BF16), norm_final.reshape(1, d))
    return h.reshape(b, s, d)
```

```python
import functools
import math

import numpy as np
import jax
import jax.numpy as jnp
from jax import lax
from jax.experimental import pallas as pl
from jax.experimental.pallas import tpu as pltpu

F32 = jnp.float32
BF16 = jnp.bfloat16
F8 = jnp.float8_e4m3fn
F8_TARGET = 224.0
F8_TINY = 1e-30
NEG_INF = float("-inf")

EPS = 1e-6
D_MODEL = 2048
DN_HEADS = 8
DN_DK = 128
DN_DV = 128
CONV_W = 4
RET_HEADS = 8
RET_DK = 64
RET_DV = 128
CHUNK = 64
ROPE_BASE = 10000.0
N_KEYS = 128
PK_HEADS = 8
PK_DQ = 256
PK_TOPK = 16
PLE_DIM = 256

DN_QK = DN_HEADS * DN_DK
DN_V = DN_HEADS * DN_DV
CONV_CH = 2 * DN_QK + DN_V
RET_QK = RET_HEADS * RET_DK
RET_V = RET_HEADS * RET_DV
AB_OFF = CONV_CH + DN_V
MAIN_COLS = CONV_CH + DN_V + 2 * RET_QK + 2 * RET_V
LANE = 128
HALO = 8
CONV_HALO = 16

TM_INPROJ = 1024
TN_INPROJ = 512
TT_SEQ = 256
TM_DENSE = 256
TL_ROUTER = 256
TM_PEER = 512
NI_PEER = 8
TE_PEER = NI_PEER * N_KEYS
VMEM_LIMIT = 56 * 1024 * 1024


def _cparams(sem, flags=None):
    return pltpu.CompilerParams(dimension_semantics=sem, vmem_limit_bytes=VMEM_LIMIT, flags=flags)


def _rms(x, w):
    return x * lax.rsqrt(jnp.mean(x * x, axis=-1, keepdims=True) + EPS) * w


def _silu(x):
    return x * jax.nn.sigmoid(x)


def _softplus(x):
    return jnp.maximum(x, 0.0) + jnp.log1p(jnp.exp(-jnp.abs(x)))


def _gelu(x):
    return 0.5 * x * (1.0 + lax.erf(x * (2.0 ** -0.5)))


def _mm(a, b):
    return jnp.dot(a.astype(BF16), b.astype(BF16), preferred_element_type=F32)


def _mm_nt(a, b):
    return lax.dot_general(a.astype(BF16), b.astype(BF16), (((1,), (1,)), ((), ())),
                           preferred_element_type=F32)


def _mm_tn(a, b):
    return lax.dot_general(a.astype(BF16), b.astype(BF16), (((0,), (0,)), ((), ())),
                           preferred_element_type=F32)


def _split3(x):
    hi = x.astype(BF16)
    rest = x - hi.astype(F32)
    mid = rest.astype(BF16)
    lo = (rest - mid.astype(F32)).astype(BF16)
    return hi, mid, lo


def _select_mm(a, b, dims=(((1,), (0,)), ((), ()))):
    if a.dtype == BF16:
        parts = [lax.dot_general(a, p, dims, preferred_element_type=F32) for p in _split3(b)]
    else:
        parts = [lax.dot_general(p, b, dims, preferred_element_type=F32) for p in _split3(a)]
    return parts[0] + parts[1] + parts[2]


def _inproj_body(x_ref, nw_ref, w_ref, wab_ref, o_ref, oab_ref, hn_ref):
    @pl.when(pl.program_id(1) == 0)
    def _():
        hn_ref[...] = _rms(x_ref[...], nw_ref[...]).astype(BF16)
        oab_ref[...] = jnp.dot(hn_ref[...], wab_ref[...], preferred_element_type=F32)

    o_ref[...] = jnp.dot(hn_ref[...], w_ref[...], preferred_element_type=F32).astype(BF16)


def _inproj(x, nw, w_main, w_ab):
    t, d = x.shape
    n = w_main.shape[1]
    return pl.pallas_call(
        _inproj_body,
        grid=(t // TM_INPROJ, n // TN_INPROJ),
        in_specs=[
            pl.BlockSpec((TM_INPROJ, d), lambda i, j: (i, 0)),
            pl.BlockSpec((1, d), lambda i, j: (0, 0)),
            pl.BlockSpec((d, TN_INPROJ), lambda i, j: (0, j)),
            pl.BlockSpec((d, LANE), lambda i, j: (0, 0)),
        ],
        out_specs=[
            pl.BlockSpec((TM_INPROJ, TN_INPROJ), lambda i, j: (i, j)),
            pl.BlockSpec((TM_INPROJ, LANE), lambda i, j: (i, 0)),
        ],
        out_shape=[jax.ShapeDtypeStruct((t, n), BF16), jax.ShapeDtypeStruct((t, LANE), F32)],
        scratch_shapes=[pltpu.VMEM((TM_INPROJ, d), BF16)],
        compiler_params=_cparams(("parallel", "arbitrary")),
        name="inproj",
    )(x, nw, w_main, w_ab)


def _dn_local_body(q_ref, k_ref, v_ref, qh_ref, kh_ref, vh_ref, ab_ref, cw_ref, alog_ref, dtb_ref,
                   selg_ref, selb_ref, qa_ref, kd_ref, u_ref, w_ref, o0_ref, gend_ref, buf_ref):
    tt = TT_SEQ
    first_tile = pl.program_id(1) == 0

    def conv_silu(main_ref, halo_ref, c0):
        halo = halo_ref[...].astype(F32)
        buf_ref[0:CONV_HALO, :] = jnp.where(first_tile, jnp.zeros_like(halo), halo)
        buf_ref[CONV_HALO:CONV_HALO + tt, :] = main_ref[...].astype(F32)
        y = None
        for j in range(CONV_W):
            off = CONV_HALO - (CONV_W - 1) + j
            term = buf_ref[off:off + tt, :] * cw_ref[j:j + 1, c0:c0 + DN_QK]
            y = term if y is None else y + term
        return _silu(y)

    q = conv_silu(q_ref, qh_ref, 0)
    k = conv_silu(k_ref, kh_ref, DN_QK)
    v = conv_silu(v_ref, vh_ref, 2 * DN_QK)

    row = lax.broadcasted_iota(jnp.int32, (tt, tt), 0)
    col = lax.broadcasted_iota(jnp.int32, (tt, tt), 1)

    def same_block(bits):
        return (row >> bits) == (col >> bits)

    blk8, blk16, blk32, blk64 = same_block(3), same_block(4), same_block(5), same_block(6)
    causal = blk64 & (col <= row)
    strict = blk64 & (col < row)
    eye = (row == col).astype(F32)

    ab = ab_ref[...]
    g = -jnp.exp(alog_ref[...]) * _softplus(ab + dtb_ref[...])
    beta = jax.nn.sigmoid(ab)
    def ones_where(mask):
        return jnp.where(mask, 1.0, 0.0).astype(BF16)

    gc = _select_mm(ones_where(causal), g)
    gtot = _select_mm(ones_where(blk64), g)
    e_gc = _select_mm(gc, selg_ref[...])
    e_gt = _select_mm(gtot, selg_ref[...])
    e_beta = _select_mm(beta, selb_ref[...])
    r128 = lax.broadcasted_iota(jnp.int32, (LANE, LANE), 0)
    c128 = lax.broadcasted_iota(jnp.int32, (LANE, LANE), 1)
    gc_t = _select_mm(ones_where(r128 == c128), gc, (((1,), (1,)), ((), ())))
    eg = jnp.exp(e_gc)
    kdec = jnp.exp(e_gt - e_gc)

    for c in range(tt // CHUNK):
        gend_ref[0, c:c + 1, :] = jnp.exp(e_gt[c * CHUNK:c * CHUNK + 1, :])
    gend_ref[0, tt // CHUNK:, :] = jnp.zeros((HALO - tt // CHUNK, DN_V), F32)

    heads = range(DN_HEADS)
    hsl = [slice(h * DN_DK, (h + 1) * DN_DK) for h in heads]

    def per_head(fn):
        return [fn(h) for h in heads]

    def l2n(x):
        return x * lax.rsqrt(jnp.sum(x * x, axis=-1, keepdims=True) + EPS)

    qh = per_head(lambda h: l2n(q[:, hsl[h]]) * (DN_DK ** -0.5))
    kh = per_head(lambda h: l2n(k[:, hsl[h]]))
    kb = per_head(lambda h: kh[h] * e_beta[:, hsl[h]])
    dmat = per_head(lambda h: jnp.exp(jnp.where(
        causal, jnp.concatenate([e_gc[:, hsl[h]]] * (tt // LANE), axis=1) - gc_t[h:h + 1, :], NEG_INF)))
    scores = per_head(lambda h: _mm_nt(jnp.concatenate([kb[h], qh[h]], axis=0), kh[h]))
    nmat = per_head(lambda h: jnp.where(strict, scores[h][:tt] * dmat[h], 0.0))
    attn = per_head(lambda h: scores[h][tt:] * dmat[h])

    n0 = per_head(lambda h: jnp.where(blk8, nmat[h], 0.0))
    n2 = per_head(lambda h: _mm(n0[h], n0[h]))
    n4 = per_head(lambda h: _mm(n2[h], n2[h]))
    inv = per_head(lambda h: eye - n0[h])
    inv = per_head(lambda h: inv[h] + _mm(inv[h], n2[h]))
    inv = per_head(lambda h: inv[h] + _mm(inv[h], n4[h]))
    for inner, outer in ((blk8, blk16), (blk16, blk32), (blk32, blk64)):
        level = outer & jnp.logical_not(inner)
        prod = per_head(lambda h: _mm(inv[h], jnp.where(level, nmat[h], 0.0)))
        inv = per_head(lambda h: inv[h] - _mm(prod[h], inv[h]))

    sol = per_head(lambda h: _mm(inv[h], jnp.concatenate(
        [v[:, hsl[h]] * e_beta[:, hsl[h]], kb[h] * eg[:, hsl[h]]], axis=1)))
    asol = per_head(lambda h: _mm(attn[h], sol[h]))
    for h in heads:
        hs = hsl[h]
        u_ref[:, hs] = sol[h][:, :DN_DV]
        w_ref[:, hs] = sol[h][:, DN_DV:].astype(BF16)
        o0_ref[:, hs] = asol[h][:, :DN_DV]
        qa_ref[:, hs] = (qh[h] * eg[:, hs] - asol[h][:, DN_DV:]).astype(BF16)
        kd_ref[:, hs] = (kh[h] * kdec[:, hs]).astype(BF16)


def _dn_local(proj, proj_ab, cw, alog_row, dtb_row, selg, selb, b, s):
    t = b * s
    nt = s // TT_SEQ
    rows_per_halo = TT_SEQ // CONV_HALO

    def main_spec(cb):
        return pl.BlockSpec((TT_SEQ, DN_QK), lambda bi, i: (bi * nt + i, cb))

    def halo_spec(cb):
        return pl.BlockSpec(
            (CONV_HALO, DN_QK), lambda bi, i: (jnp.maximum((bi * nt + i) * rows_per_halo - 1, 0), cb))

    def full(shape):
        return pl.BlockSpec(shape, lambda bi, i: (0,) * len(shape))

    tok_spec = pl.BlockSpec((TT_SEQ, DN_V), lambda bi, i: (bi * nt + i, 0))
    return pl.pallas_call(
        _dn_local_body,
        grid=(b, nt),
        in_specs=[main_spec(0), main_spec(1), main_spec(2), halo_spec(0), halo_spec(1), halo_spec(2),
                  pl.BlockSpec((TT_SEQ, LANE), lambda bi, i: (bi * nt + i, 0)),
                  full((CONV_W, CONV_CH)), full((1, LANE)), full((1, LANE)),
                  full((LANE, DN_V)), full((LANE, DN_V))],
        out_specs=[tok_spec, tok_spec, tok_spec, tok_spec, tok_spec,
                   pl.BlockSpec((1, HALO, DN_V), lambda bi, i: (bi * nt + i, 0, 0))],
        out_shape=[jax.ShapeDtypeStruct((t, DN_V), BF16),
                   jax.ShapeDtypeStruct((t, DN_V), BF16),
                   jax.ShapeDtypeStruct((t, DN_V), F32),
                   jax.ShapeDtypeStruct((t, DN_V), BF16),
                   jax.ShapeDtypeStruct((t, DN_V), F32),
                   jax.ShapeDtypeStruct((t // TT_SEQ, HALO, DN_V), F32)],
        scratch_shapes=[pltpu.VMEM((CONV_HALO + TT_SEQ, DN_QK), F32)],
        compiler_params=_cparams(("parallel", "parallel")),
        name="dn_local",
    )(proj, proj, proj, proj, proj, proj, proj_ab, cw, alog_row, dtb_row, selg, selb)


def _dn_scan_body(qa_ref, kd_ref, u_ref, w_ref, o0_ref, gend_ref, o_ref, s_ref):
    nb = qa_ref.shape[0]

    @pl.when(pl.program_id(0) == 0)
    def _():
        s_ref[...] = jnp.zeros(s_ref.shape, F32)

    chains = [(b, h, slice(h * DN_DK, (h + 1) * DN_DK)) for b in range(nb) for h in range(DN_HEADS)]
    states = [s_ref[n] for n in range(len(chains))]
    for c in range(TT_SEQ // CHUNK):
        rs = slice(c * CHUNK, (c + 1) * CHUNK)
        both = [jnp.dot(jnp.concatenate([w_ref[b, rs, hs], qa_ref[b, rs, hs]], axis=0),
                        states[n].astype(BF16), preferred_element_type=F32)
                for n, (b, h, hs) in enumerate(chains)]
        for n, (b, h, hs) in enumerate(chains):
            o_ref[b, rs, hs] = both[n][CHUNK:] + o0_ref[b, rs, hs]
        states = [states[n] * gend_ref[b, 0, c:c + 1, hs]
                  + _mm_tn(kd_ref[b, rs, hs], u_ref[b, rs, hs] - both[n][:CHUNK])
                  for n, (b, h, hs) in enumerate(chains)]
    for n in range(len(chains)):
        s_ref[n] = states[n]


def _dn_scan(qa, kd, u, w, o0, gend, b, s):
    nt = s // TT_SEQ
    tok = lambda a: a.reshape(b, s, DN_V)
    tok_spec = pl.BlockSpec((b, TT_SEQ, DN_V), lambda i: (0, i, 0))
    out = pl.pallas_call(
        _dn_scan_body,
        grid=(nt,),
        in_specs=[tok_spec, tok_spec, tok_spec, tok_spec, tok_spec,
                  pl.BlockSpec((b, 1, HALO, DN_V), lambda i: (0, i, 0, 0))],
        out_specs=tok_spec,
        out_shape=jax.ShapeDtypeStruct((b, s, DN_V), F32),
        scratch_shapes=[pltpu.VMEM((b * DN_HEADS, DN_DK, DN_DV), F32)],
        compiler_params=_cparams(("arbitrary",)),
        name="dn_scan",
    )(tok(qa), tok(kd), tok(u), tok(w), tok(o0), gend.reshape(b, nt, HALO, DN_V))
    return out.reshape(b * s, DN_V)


def _log_gamma(h):
    return math.log1p(-(2.0 ** (-5.0 - h)))


def _ret_body(qk_ref, v_ref, pos_ref, freq_ref, o_ref, s_ref):
    tt = TT_SEQ

    @pl.when(pl.program_id(1) == 0)
    def _():
        s_ref[...] = jnp.zeros(s_ref.shape, F32)

    ang = pos_ref[...].astype(F32) * freq_ref[...]
    lane = lax.broadcasted_iota(jnp.int32, (tt, LANE), 1)
    first_half = (lane & (RET_DK // 2)) == 0
    cos = jnp.cos(ang)
    sin = jnp.sin(ang)
    ssin = jnp.where(first_half, -sin, sin)

    def rotary(x):
        swapped = jnp.where(first_half, pltpu.roll(x, LANE - RET_DK // 2, axis=1),
                            pltpu.roll(x, RET_DK // 2, axis=1))
        return x * cos + swapped * ssin

    row = lax.broadcasted_iota(jnp.int32, (tt, tt), 0)
    col = lax.broadcasted_iota(jnp.int32, (tt, tt), 1)
    rel = (row - col).astype(F32)
    causal = row >= col
    trow = lax.broadcasted_iota(jnp.int32, (tt, LANE), 0).astype(F32)
    srow = lax.broadcasted_iota(jnp.int32, (LANE, LANE), 0)

    for pr in range(RET_HEADS // 2):
        ps = slice(pr * LANE, (pr + 1) * LANE)
        qp = rotary(qk_ref[:, ps].astype(F32))
        kp = rotary(qk_ref[:, RET_QK + pr * LANE:RET_QK + (pr + 1) * LANE].astype(F32)) * (RET_DK ** -0.5)
        kpb = kp.astype(BF16)
        state = s_ref[pr]
        sb = state.astype(BF16)
        update = jnp.zeros((LANE, RET_DV), F32)
        for hh in range(2):
            h = 2 * pr + hh
            lg = _log_gamma(h)
            mine = (lane < RET_DK) if hh == 0 else (lane >= RET_DK)
            qm = jnp.where(mine, qp, 0.0).astype(BF16)
            vh = v_ref[:, h * RET_DV:(h + 1) * RET_DV].astype(BF16)
            att = _mm_nt(qm, kpb) * jnp.exp(jnp.where(causal, rel * lg, NEG_INF))
            inner = jnp.dot(att.astype(BF16), vh, preferred_element_type=F32)
            cross = jnp.dot(qm, sb, preferred_element_type=F32) * jnp.exp((trow + 1.0) * lg)
            o_ref[:, h * RET_DV:(h + 1) * RET_DV] = inner + cross
            kz = jnp.where(mine, kp, 0.0) * jnp.exp((tt - 1.0 - trow) * lg)
            update = update + _mm_tn(kz, vh)
        decay = jnp.where(srow < RET_DK, math.exp(tt * _log_gamma(2 * pr)),
                          math.exp(tt * _log_gamma(2 * pr + 1)))
        s_ref[pr] = state * decay + update


def _ret(proj, pos, freq_row, b, s):
    t = b * s
    nt = s // TT_SEQ
    return pl.pallas_call(
        _ret_body,
        grid=(b, nt),
        in_specs=[pl.BlockSpec((TT_SEQ, 2 * RET_QK), lambda bi, i: (bi * nt + i, 4)),
                  pl.BlockSpec((TT_SEQ, RET_V), lambda bi, i: (bi * nt + i, 5)),
                  pl.BlockSpec((TT_SEQ, 1), lambda bi, i: (bi * nt + i, 0)),
                  pl.BlockSpec((1, LANE), lambda bi, i: (0, 0))],
        out_specs=pl.BlockSpec((TT_SEQ, RET_V), lambda bi, i: (bi * nt + i, 0)),
        out_shape=jax.ShapeDtypeStruct((t, RET_V), F32),
        scratch_shapes=[pltpu.VMEM((RET_HEADS // 2, LANE, RET_DV), F32)],
        compiler_params=_cparams(("parallel", "arbitrary")),
        name="ret",
    )(proj, proj, pos, freq_row)


def _outproj_body(odn_ref, z_ref, or_ref, rg_ref, x_ref, dnw_ref, gnw_ref, gnb_ref, w_ref, o_ref,
                  mix_ref):
    for h in range(DN_HEADS):
        hs = slice(h * DN_DV, (h + 1) * DN_DV)
        mix_ref[:, hs] = (_rms(odn_ref[:, hs], dnw_ref[...])
                          * _silu(z_ref[:, hs].astype(F32))).astype(BF16)
    for h in range(RET_HEADS):
        hs = slice(h * RET_DV, (h + 1) * RET_DV)
        o = or_ref[:, hs]
        cen = o - jnp.mean(o, axis=-1, keepdims=True)
        y = cen * lax.rsqrt(jnp.mean(cen * cen, axis=-1, keepdims=True) + EPS)
        y = (y * gnw_ref[:, hs] + gnb_ref[:, hs]) * _silu(rg_ref[:, hs].astype(F32))
        mix_ref[:, DN_V + h * RET_DV:DN_V + (h + 1) * RET_DV] = y.astype(BF16)
    o_ref[...] = x_ref[...] + jnp.dot(mix_ref[...], w_ref[...], preferred_element_type=F32)


def _outproj(o_dn, proj, o_r, x, dnw, gnw, gnb, w_out):
    t, d = x.shape
    tm = TM_DENSE
    full = lambda shape: pl.BlockSpec(shape, lambda i: (0, 0))
    return pl.pallas_call(
        _outproj_body,
        grid=(t // tm,),
        in_specs=[pl.BlockSpec((tm, DN_V), lambda i: (i, 0)),
                  pl.BlockSpec((tm, DN_V), lambda i: (i, 3)),
                  pl.BlockSpec((tm, RET_V), lambda i: (i, 0)),
                  pl.BlockSpec((tm, RET_V), lambda i: (i, 6)),
                  pl.BlockSpec((tm, d), lambda i: (i, 0)),
                  full((1, DN_DV)), full((1, RET_V)), full((1, RET_V)), full((DN_V + RET_V, d))],
        out_specs=pl.BlockSpec((tm, d), lambda i: (i, 0)),
        out_shape=jax.ShapeDtypeStruct((t, d), F32),
        scratch_shapes=[pltpu.VMEM((tm, DN_V + RET_V), BF16)],
        compiler_params=_cparams(("parallel",)),
        name="outproj",
    )(o_dn, proj, o_r, proj, x, dnw, gnw, gnb, w_out)


def _pq_body(h_ref, nw_ref, w_ref, keys_ref, hnt_ref, inv_ref, sc_ref):
    hn = _rms(h_ref[...], nw_ref[...])
    hnt = hn.T
    amax = jnp.maximum(jnp.max(jnp.abs(hnt), axis=0, keepdims=True), F8_TINY)
    hnt_ref[...] = (hnt * (F8_TARGET / amax)).astype(F8)
    inv_ref[...] = amax * (1.0 / F8_TARGET)
    q = jnp.dot(hn.astype(BF16), w_ref[...], preferred_element_type=F32)
    half = PK_DQ // 2
    for hp in range(2 * PK_HEADS):
        sc_ref[hp] = _mm_nt(keys_ref[hp], q[:, hp * half:(hp + 1) * half])


def _pq(h1, nw, w_pq, keys):
    t, d = h1.shape
    tm = TM_DENSE
    nk = 2 * PK_HEADS
    return pl.pallas_call(
        _pq_body,
        grid=(t // tm,),
        in_specs=[pl.BlockSpec((tm, d), lambda i: (i, 0)),
                  pl.BlockSpec((1, d), lambda i: (0, 0)),
                  pl.BlockSpec((d, PK_HEADS * PK_DQ), lambda i: (0, 0)),
                  pl.BlockSpec((nk, N_KEYS, PK_DQ // 2), lambda i: (0, 0, 0))],
        out_specs=[pl.BlockSpec((d, tm), lambda i: (0, i)),
                   pl.BlockSpec((1, tm), lambda i: (0, i)),
                   pl.BlockSpec((nk, N_KEYS, tm), lambda i: (0, 0, i))],
        out_shape=[jax.ShapeDtypeStruct((d, t), F8),
                   jax.ShapeDtypeStruct((1, t), F32),
                   jax.ShapeDtypeStruct((nk, N_KEYS, t), F32)],
        compiler_params=_cparams(("parallel",)),
        name="pq",
    )(h1, nw, w_pq, keys)


def _route(a, bsc, exact):
    tl = a.shape[1]
    k = PK_TOPK
    keyid = lax.broadcasted_iota(jnp.int32, (N_KEYS, tl), 0)
    slot = lax.broadcasted_iota(jnp.int32, (k, tl), 0)

    def extract(s, ids, n_ids):
        m = jnp.max(s, axis=0, keepdims=True)
        hit = s == m
        if exact:
            hit = ids == jnp.min(jnp.where(hit, ids, n_ids), axis=0, keepdims=True)
        return m, hit

    def top_k(s):
        rank = jnp.full((N_KEYS, tl), float(k), F32)
        vals = jnp.zeros((k, tl), F32)
        for r in range(k):
            m, hit = extract(s, keyid, N_KEYS)
            rank = jnp.where(hit, float(r), rank)
            s = jnp.where(hit, NEG_INF, s)
            vals = jnp.where(slot == r, m, vals)
        return vals, rank

    av, rank_a = top_k(a)
    bv, rank_b = top_k(bsc)

    k2 = k // 2
    assert k2 & (k2 - 1) == 0
    n_cand = k + (k - 1) * k2
    cand = jnp.concatenate([av[0:1, :] + bv] + [av[r:r + 1, :] + bv[0:k2, :] for r in range(1, k)],
                           axis=0)
    row = lax.broadcasted_iota(jnp.int32, (n_cand, tl), 0)
    tail = row - k
    cid = jnp.where(row < k, row,
                    (1 + (tail >> (k2.bit_length() - 1))) * k + (tail & (k2 - 1)))
    work = cand
    for _ in range(k):
        _, hit = extract(work, cid, k * k)
        work = jnp.where(hit, NEG_INF, work)
    sel = (work == NEG_INF).astype(F32)
    zsum = jnp.sum(sel * jnp.exp(cand - cand[0:1, :]), axis=0, keepdims=True)

    cnt = jnp.zeros((N_KEYS, tl), F32)
    for r in range(k):
        lo, hi = (0, k) if r == 0 else (k + (r - 1) * k2, k + r * k2)
        cnt_r = jnp.sum(sel[lo:hi, :], axis=0, keepdims=True)
        cnt = jnp.where(rank_a == float(r), cnt_r, cnt)

    def full_count(x):
        return jnp.sum(x, axis=0, keepdims=True) == float(k)

    ok = (full_count((rank_a < float(k)).astype(F32)) & full_count((rank_b < float(k)).astype(F32))
          & full_count(sel))
    eb = jnp.exp(bsc - bv[0:1, :])
    ea = jnp.exp(a - av[0:1, :]) / zsum
    return rank_b, eb, cnt, ea, ok


def _router_body(sc_ref, rankb_ref, eb_ref, cnt_ref, ea_ref):
    def run(exact):
        rank_b, eb, cnt, ea, ok = _route(sc_ref[0], sc_ref[1], exact)
        rankb_ref[0] = rank_b.astype(BF16)
        eb_ref[0] = eb.astype(BF16)
        cnt_ref[0] = cnt
        ea_ref[0] = ea
        return ok

    ok = run(exact=False)
    n_bad = jnp.sum(jnp.where(ok, 0.0, 1.0), axis=1, keepdims=True)

    @pl.when(n_bad[0, 0] > 0.0)
    def _():
        run(exact=True)


def _router(scores):
    nk, n, t = scores.shape
    tl = TL_ROUTER
    spec = pl.BlockSpec((1, n, tl), lambda h, j: (h, 0, j))
    shape = lambda dt: jax.ShapeDtypeStruct((PK_HEADS, n, t), dt)
    return pl.pallas_call(
        _router_body,
        grid=(PK_HEADS, t // tl),
        in_specs=[pl.BlockSpec((2, n, tl), lambda h, j: (h, 0, j))],
        out_specs=[spec, spec, spec, spec],
        out_shape=[shape(BF16), shape(BF16), shape(F32), shape(F32)],
        compiler_params=_cparams(("parallel", "parallel")),
        name="router",
    )(scores)


def _u_prep_body(u_ref, o_ref, inv_ref):
    u = u_ref[...]
    amax = jnp.max(jnp.max(jnp.abs(u), axis=0, keepdims=True), axis=1, keepdims=True)
    amax = jnp.maximum(amax, F8_TINY)
    o_ref[...] = (u * (F8_TARGET / amax)).astype(F8)
    inv_ref[0] = jnp.broadcast_to(amax * (1.0 / F8_TARGET), inv_ref.shape[1:])


def _u_prep(u):
    ne, d = u.shape
    te, tm = TE_PEER, TM_PEER
    return pl.pallas_call(
        _u_prep_body,
        grid=(ne // te,),
        in_specs=[pl.BlockSpec((te, d), lambda j: (j, 0))],
        out_specs=[pl.BlockSpec((te, d), lambda j: (j, 0)), pl.BlockSpec((1, 1, tm), lambda j: (j, 0, 0))],
        out_shape=[jax.ShapeDtypeStruct((ne, d), F8), jax.ShapeDtypeStruct((ne // te, 1, tm), F32)],
        compiler_params=_cparams(("parallel",)),
        name="u_prep",
    )(u)


def _v_prep_body(v_ref, o_ref):
    o_ref[...] = v_ref[...].T.astype(BF16)


def _v_prep(v):
    ne, d = v.shape
    te = TE_PEER
    return pl.pallas_call(
        _v_prep_body,
        grid=(ne // te,),
        in_specs=[pl.BlockSpec((te, d), lambda j: (j, 0))],
        out_specs=pl.BlockSpec((d, te), lambda j: (0, j)),
        out_shape=jax.ShapeDtypeStruct((d, ne), BF16),
        compiler_params=_cparams(("parallel",)),
        name="v_prep",
    )(v)


def _peer_pre_body(u_ref, hnt_ref, o_ref):
    o_ref[...] = jnp.dot(u_ref[...], hnt_ref[...], preferred_element_type=F32)


def _peer_step(pre_ref, pre_next_ref, hnt_ref, u_ref, vt_ref, rankb_ref, eb_ref, cnt_ref, ea_ref,
               xinv_ref, uinv_ref, acc_ref, act_ref):
    pre_next_ref[...] = jnp.dot(u_ref[...], hnt_ref[...], preferred_element_type=F32)
    unscale = xinv_ref[...] * uinv_ref[0]
    for il in range(NI_PEER):
        rs = slice(il * N_KEYS, (il + 1) * N_KEYS)
        gate = jnp.zeros((N_KEYS, TM_PEER), BF16)
        for h in range(PK_HEADS):
            picked = rankb_ref[h] < cnt_ref[h, il:il + 1, :].astype(BF16)
            gate = gate + jnp.where(picked, eb_ref[h], 0.0) * ea_ref[h, il:il + 1, :].astype(BF16)
        act_ref[rs, :] = _gelu(pre_ref[rs, :] * unscale).astype(BF16) * gate
    acc_ref[...] += jnp.dot(vt_ref[...], act_ref[...], preferred_element_type=F32)


def _peer_body(pre0_ref, hnt_ref, u_ref, vt_ref, rankb_ref, eb_ref, cnt_ref, ea_ref, xinv_ref,
               uinv_ref, o_ref, acc_ref, act_ref, pre_a_ref, pre_b_ref):
    i = pl.program_id(0)
    j = pl.program_id(1)
    args = (hnt_ref, u_ref, vt_ref, rankb_ref, eb_ref, cnt_ref, ea_ref, xinv_ref, uinv_ref, acc_ref,
            act_ref)

    @pl.when((i == 0) & (j == 0))
    def _():
        pre_a_ref[...] = pre0_ref[...]

    @pl.when(j == 0)
    def _():
        acc_ref[...] = jnp.zeros(acc_ref.shape, F32)

    @pl.when(lax.rem(j, 2) == 0)
    def _():
        _peer_step(pre_a_ref, pre_b_ref, *args)

    @pl.when(lax.rem(j, 2) == 1)
    def _():
        _peer_step(pre_b_ref, pre_a_ref, *args)

    @pl.when(j == pl.num_programs(1) - 1)
    def _():
        o_ref[...] = acc_ref[...].T


def _peer(hnt, xinv, u_f8, uinv, vt_bf, rankb, eb, cnt, ea):
    d, t = hnt.shape
    ne = u_f8.shape[0]
    tm, te = TM_PEER, TE_PEER
    ni, nj = t // tm, ne // te
    assert nj % 2 == 0, "pre-activation buffers alternate with the expert-tile index"
    pre0 = pl.pallas_call(
        _peer_pre_body,
        grid=(1,),
        in_specs=[pl.BlockSpec((te, d), lambda i: (0, 0)), pl.BlockSpec((d, tm), lambda i: (0, 0))],
        out_specs=pl.BlockSpec((te, tm), lambda i: (0, 0)),
        out_shape=jax.ShapeDtypeStruct((te, tm), F32),
        compiler_params=_cparams(("arbitrary",)),
        name="peer_pre",
    )(u_f8, hnt)

    def next_i(i, j):
        return jnp.minimum(i + (j + 1) // nj, ni - 1)

    key_spec = pl.BlockSpec((PK_HEADS, N_KEYS, tm), lambda i, j: (0, 0, i))
    blk_spec = pl.BlockSpec((PK_HEADS, NI_PEER, tm), lambda i, j: (0, j, i))
    return pl.pallas_call(
        _peer_body,
        grid=(ni, nj),
        in_specs=[pl.BlockSpec((te, tm), lambda i, j: (0, 0)),
                  pl.BlockSpec((d, tm), lambda i, j: (0, next_i(i, j))),
                  pl.BlockSpec((te, d), lambda i, j: ((j + 1) % nj, 0)),
                  pl.BlockSpec((d, te), lambda i, j: (0, j)),
                  key_spec, key_spec, blk_spec, blk_spec,
                  pl.BlockSpec((1, tm), lambda i, j: (0, i)),
                  pl.BlockSpec((1, 1, tm), lambda i, j: (j, 0, 0))],
        out_specs=pl.BlockSpec((tm, d), lambda i, j: (i, 0)),
        out_shape=jax.ShapeDtypeStruct((t, d), F32),
        scratch_shapes=[pltpu.VMEM((d, tm), F32), pltpu.VMEM((te, tm), BF16),
                        pltpu.VMEM((te, tm), F32), pltpu.VMEM((te, tm), F32)],
        compiler_params=_cparams(("arbitrary", "arbitrary")),
        name="peer",
    )(pre0, hnt, u_f8, vt_bf, rankb, eb, cnt, ea, xinv, uinv)


def _ple_body(h_ref, po_ref, p_ref, nple_ref, wg_ref, wp_ref, nfin_ref, o_ref):
    h2 = h_ref[...] + po_ref[...]
    gate = jax.nn.sigmoid(
        jnp.dot(_rms(h2, nple_ref[...]).astype(BF16), wg_ref[...], preferred_element_type=F32))
    ple = jnp.dot(p_ref[...].astype(BF16), wp_ref[...], preferred_element_type=F32)
    o_ref[...] = _rms(h2 + gate * ple, nfin_ref[...])


def _ple(h1, peer_out, p, nple, wg, wp, nfin):
    t, d = h1.shape
    tm = TM_DENSE
    full = lambda shape: pl.BlockSpec(shape, lambda i: (0, 0))
    return pl.pallas_call(
        _ple_body,
        grid=(t // tm,),
        in_specs=[pl.BlockSpec((tm, d), lambda i: (i, 0)),
                  pl.BlockSpec((tm, d), lambda i: (i, 0)),
                  pl.BlockSpec((tm, PLE_DIM), lambda i: (i, 0)),
                  full((1, d)), full((d, d)), full((PLE_DIM, d)), full((1, d))],
        out_specs=pl.BlockSpec((tm, d), lambda i: (i, 0)),
        out_shape=jax.ShapeDtypeStruct((t, d), F32),
        compiler_params=_cparams(("parallel",)),
        name="ple",
    )(h1, peer_out, p, nple, wg, wp, nfin)


def _lane_row(vec, offset):
    return jnp.zeros((1, LANE), F32).at[0, offset:offset + vec.shape[0]].set(vec.astype(F32))


def _head_selector(offset):
    sel = np.zeros((LANE, DN_V), np.float32)
    for h in range(DN_HEADS):
        sel[offset + h, h * DN_DV:(h + 1) * DN_DV] = 1.0
    return jnp.asarray(sel, dtype=BF16)


def kernel(x, p, positions, norm_mix, w_in, conv_w, a_log, dt_bias, dn_norm, ret_gn_w, ret_gn_b,
           w_out, norm_ffn, w_pq, sub_keys, expert_u, expert_v, norm_ple, w_ple_gate, w_ple_proj,
           norm_final):
    b, s, d = x.shape
    t = b * s
    depth = w_in.shape[0]
    assert depth == 1, "the final rms_norm is fused into the single layer's ple kernel"
    half = RET_DK // 2
    inv_freq = ROPE_BASE ** (-jnp.arange(half, dtype=F32) / half)
    freq_row = jnp.tile(inv_freq, LANE // half).reshape(1, LANE)
    pos = positions.reshape(t, 1)
    selg, selb = _head_selector(0), _head_selector(DN_HEADS)

    h = x.reshape(t, d)
    for i in range(depth):
        w = w_in[i].astype(BF16)
        w_main = jnp.concatenate([w[:, :AB_OFF], w[:, AB_OFF + 2 * DN_HEADS:]], axis=1)
        w_ab = jnp.pad(w[:, AB_OFF:AB_OFF + 2 * DN_HEADS], ((0, 0), (0, LANE - 2 * DN_HEADS)))
        proj, proj_ab = _inproj(h, norm_mix[i].reshape(1, d), w_main, w_ab)

        qa, kd, u, wv, o0, gend = _dn_local(
            proj, proj_ab, conv_w[i].astype(F32).T, _lane_row(a_log[i], 0), _lane_row(dt_bias[i], 0),
            selg, selb, b, s)
        o_dn = _dn_scan(qa, kd, u, wv, o0, gend, b, s)
        o_r = _ret(proj, pos, freq_row, b, s)
        h1 = _outproj(o_dn, proj, o_r, h, dn_norm[i].reshape(1, DN_DV), ret_gn_w[i].reshape(1, RET_V),
                      ret_gn_b[i].reshape(1, RET_V), w_out[i].astype(BF16))

        keys = sub_keys[i].reshape(2 * PK_HEADS, N_KEYS, PK_DQ // 2).astype(BF16)
        hnt, xinv, scores = _pq(h1, norm_ffn[i].reshape(1, d), w_pq[i].astype(BF16), keys)
        rankb, eb, cnt, ea = _router(scores)
        u_f8, uinv = _u_prep(expert_u[i])
        peer_out = _peer(hnt, xinv, u_f8, uinv, _v_prep(expert_v[i]), rankb, eb, cnt, ea)

        h = _ple(h1, peer_out, p[i].reshape(t, PLE_DIM), norm_ple[i].reshape(1, d),
                 w_ple_gate[i].astype(BF16), w_ple_proj[i].astype(BF16), norm_final.reshape(1, d))
    return h.reshape(b, s, d)
```

```python
import functools
import math

import numpy as np
import jax
import jax.numpy as jnp
from jax import lax
from jax.experimental import pallas as pl
from jax.experimental.pallas import tpu as pltpu

F32 = jnp.float32
BF16 = jnp.bfloat16
F8 = jnp.float8_e4m3fn
F8_TARGET = 224.0
F8_TINY = 1e-30
NEG_INF = float("-inf")

EPS = 1e-6
D_MODEL = 2048
DN_HEADS = 8
DN_DK = 128
DN_DV = 128
CONV_W = 4
RET_HEADS = 8
RET_DK = 64
RET_DV = 128
CHUNK = 64
ROPE_BASE = 10000.0
N_KEYS = 128
PK_HEADS = 8
PK_DQ = 256
PK_TOPK = 16
PLE_DIM = 256

DN_QK = DN_HEADS * DN_DK
DN_V = DN_HEADS * DN_DV
CONV_CH = 2 * DN_QK + DN_V
RET_QK = RET_HEADS * RET_DK
RET_V = RET_HEADS * RET_DV
AB_OFF = CONV_CH + DN_V
MAIN_COLS = CONV_CH + DN_V + 2 * RET_QK + 2 * RET_V
LANE = 128
HALO = 8
CONV_HALO = 16

TM_INPROJ = 1024
TN_INPROJ = 1024
TT_SEQ = 256
TM_DENSE = 256
TL_ROUTER = 512
TM_PEER = 512
NI_PEER = 8
TE_PEER = NI_PEER * N_KEYS
VMEM_LIMIT = 56 * 1024 * 1024


def _cparams(sem, flags=None):
    return pltpu.CompilerParams(dimension_semantics=sem, vmem_limit_bytes=VMEM_LIMIT, flags=flags)


def _rms(x, w):
    return x * lax.rsqrt(jnp.mean(x * x, axis=-1, keepdims=True) + EPS) * w


def _silu(x):
    return x * jax.nn.sigmoid(x)


def _softplus(x):
    return jnp.maximum(x, 0.0) + jnp.log1p(jnp.exp(-jnp.abs(x)))


def _gelu(x):
    return 0.5 * x * (1.0 + lax.erf(x * (2.0 ** -0.5)))


def _mm(a, b):
    return jnp.dot(a.astype(BF16), b.astype(BF16), preferred_element_type=F32)


def _mm_nt(a, b):
    return lax.dot_general(a.astype(BF16), b.astype(BF16), (((1,), (1,)), ((), ())),
                           preferred_element_type=F32)


def _mm_tn(a, b):
    return lax.dot_general(a.astype(BF16), b.astype(BF16), (((0,), (0,)), ((), ())),
                           preferred_element_type=F32)


def _split3(x):
    hi = x.astype(BF16)
    rest = x - hi.astype(F32)
    mid = rest.astype(BF16)
    lo = (rest - mid.astype(F32)).astype(BF16)
    return hi, mid, lo


def _select_mm(a, b, dims=(((1,), (0,)), ((), ()))):
    if a.dtype == BF16:
        parts = [lax.dot_general(a, p, dims, preferred_element_type=F32) for p in _split3(b)]
    else:
        parts = [lax.dot_general(p, b, dims, preferred_element_type=F32) for p in _split3(a)]
    return parts[0] + parts[1] + parts[2]


def _inproj_body(x_ref, nw_ref, w_head_ref, w_tail_ref, wab_ref, o_ref, oab_ref, hn_ref):
    j = pl.program_id(1)
    n_head = AB_OFF // TN_INPROJ

    @pl.when(j == 0)
    def _():
        hn_ref[...] = _rms(x_ref[...], nw_ref[...]).astype(BF16)
        oab_ref[...] = jnp.dot(hn_ref[...], wab_ref[...], preferred_element_type=F32)

    @pl.when(j < n_head)
    def _():
        o_ref[...] = jnp.dot(hn_ref[...], w_head_ref[...], preferred_element_type=F32).astype(BF16)

    @pl.when(j >= n_head)
    def _():
        o_ref[...] = jnp.dot(hn_ref[...], w_tail_ref[...], preferred_element_type=F32).astype(BF16)


def _inproj(x, nw, w_all, w_tail, w_ab):
    t, d = x.shape
    n_head = AB_OFF // TN_INPROJ
    n = AB_OFF + w_tail.shape[1]
    return pl.pallas_call(
        _inproj_body,
        grid=(t // TM_INPROJ, n // TN_INPROJ),
        in_specs=[
            pl.BlockSpec((TM_INPROJ, d), lambda i, j: (i, 0)),
            pl.BlockSpec((1, d), lambda i, j: (0, 0)),
            pl.BlockSpec((d, TN_INPROJ), lambda i, j: (0, jnp.minimum(j, n_head - 1))),
            pl.BlockSpec((d, TN_INPROJ), lambda i, j: (0, jnp.maximum(j - n_head, 0))),
            pl.BlockSpec((d, LANE), lambda i, j: (0, 0)),
        ],
        out_specs=[
            pl.BlockSpec((TM_INPROJ, TN_INPROJ), lambda i, j: (i, j)),
            pl.BlockSpec((TM_INPROJ, LANE), lambda i, j: (i, 0)),
        ],
        out_shape=[jax.ShapeDtypeStruct((t, n), BF16), jax.ShapeDtypeStruct((t, LANE), F32)],
        scratch_shapes=[pltpu.VMEM((TM_INPROJ, d), BF16)],
        compiler_params=_cparams(("parallel", "arbitrary")),
        name="inproj",
    )(x, nw, w_all, w_tail, w_ab)


def _dn_local_body(q_ref, k_ref, v_ref, qh_ref, kh_ref, vh_ref, ab_ref, cw_ref, alog_ref, dtb_ref,
                   selg_ref, selb_ref, qa_ref, kd_ref, u_ref, w_ref, o0_ref, gend_ref, buf_ref):
    tt = TT_SEQ
    first_tile = pl.program_id(1) == 0

    def conv_silu(main_ref, halo_ref, c0):
        halo = halo_ref[...].astype(F32)
        buf_ref[0:CONV_HALO, :] = jnp.where(first_tile, jnp.zeros_like(halo), halo)
        buf_ref[CONV_HALO:CONV_HALO + tt, :] = main_ref[...].astype(F32)
        y = None
        for j in range(CONV_W):
            off = CONV_HALO - (CONV_W - 1) + j
            term = buf_ref[off:off + tt, :] * cw_ref[j:j + 1, c0:c0 + DN_QK]
            y = term if y is None else y + term
        return _silu(y)

    q = conv_silu(q_ref, qh_ref, 0)
    k = conv_silu(k_ref, kh_ref, DN_QK)
    v = conv_silu(v_ref, vh_ref, 2 * DN_QK)

    row = lax.broadcasted_iota(jnp.int32, (tt, tt), 0)
    col = lax.broadcasted_iota(jnp.int32, (tt, tt), 1)

    def same_block(bits):
        return (row >> bits) == (col >> bits)

    blk8, blk16, blk32, blk64 = same_block(3), same_block(4), same_block(5), same_block(6)
    causal = blk64 & (col <= row)
    strict = blk64 & (col < row)
    eye = (row == col).astype(F32)

    ab = ab_ref[...]
    g = -jnp.exp(alog_ref[...]) * _softplus(ab + dtb_ref[...])
    beta = jax.nn.sigmoid(ab)
    def ones_where(mask):
        return jnp.where(mask, 1.0, 0.0).astype(BF16)

    gc = _select_mm(ones_where(causal), g)
    gtot = _select_mm(ones_where(blk64), g)
    e_gc = _select_mm(gc, selg_ref[...])
    e_gt = _select_mm(gtot, selg_ref[...])
    e_beta = _select_mm(beta, selb_ref[...])
    r128 = lax.broadcasted_iota(jnp.int32, (LANE, LANE), 0)
    c128 = lax.broadcasted_iota(jnp.int32, (LANE, LANE), 1)
    gc_t = _select_mm(ones_where(r128 == c128), gc, (((1,), (1,)), ((), ())))
    eg = jnp.exp(e_gc)
    kdec = jnp.exp(e_gt - e_gc)

    for c in range(tt // CHUNK):
        gend_ref[0, c:c + 1, :] = jnp.exp(e_gt[c * CHUNK:c * CHUNK + 1, :])
    gend_ref[0, tt // CHUNK:, :] = jnp.zeros((HALO - tt // CHUNK, DN_V), F32)

    heads = range(DN_HEADS)
    hsl = [slice(h * DN_DK, (h + 1) * DN_DK) for h in heads]

    def per_head(fn):
        return [fn(h) for h in heads]

    def l2n(x):
        return x * lax.rsqrt(jnp.sum(x * x, axis=-1, keepdims=True) + EPS)

    qh = per_head(lambda h: l2n(q[:, hsl[h]]) * (DN_DK ** -0.5))
    kh = per_head(lambda h: l2n(k[:, hsl[h]]))
    kb = per_head(lambda h: kh[h] * e_beta[:, hsl[h]])
    dmat = per_head(lambda h: jnp.exp(jnp.where(
        causal, jnp.concatenate([e_gc[:, hsl[h]]] * (tt // LANE), axis=1) - gc_t[h:h + 1, :], NEG_INF)))
    scores = per_head(lambda h: _mm_nt(jnp.concatenate([kb[h], qh[h]], axis=0), kh[h]))
    nmat = per_head(lambda h: jnp.where(strict, scores[h][:tt] * dmat[h], 0.0))
    attn = per_head(lambda h: scores[h][tt:] * dmat[h])

    n0 = per_head(lambda h: jnp.where(blk8, nmat[h], 0.0))
    n2 = per_head(lambda h: _mm(n0[h], n0[h]))
    n4 = per_head(lambda h: _mm(n2[h], n2[h]))
    inv = per_head(lambda h: eye - n0[h])
    inv = per_head(lambda h: inv[h] + _mm(inv[h], n2[h]))
    inv = per_head(lambda h: inv[h] + _mm(inv[h], n4[h]))
    for inner, outer in ((blk8, blk16), (blk16, blk32), (blk32, blk64)):
        level = outer & jnp.logical_not(inner)
        prod = per_head(lambda h: _mm(inv[h], jnp.where(level, nmat[h], 0.0)))
        inv = per_head(lambda h: inv[h] - _mm(prod[h], inv[h]))

    sol = per_head(lambda h: _mm(inv[h], jnp.concatenate(
        [v[:, hsl[h]] * e_beta[:, hsl[h]], kb[h] * eg[:, hsl[h]]], axis=1)))
    asol = per_head(lambda h: _mm(attn[h], sol[h]))
    for h in heads:
        hs = hsl[h]
        u_ref[:, hs] = sol[h][:, :DN_DV]
        w_ref[:, hs] = sol[h][:, DN_DV:].astype(BF16)
        o0_ref[:, hs] = asol[h][:, :DN_DV]
        qa_ref[:, hs] = (qh[h] * eg[:, hs] - asol[h][:, DN_DV:]).astype(BF16)
        kd_ref[:, hs] = (kh[h] * kdec[:, hs]).astype(BF16)


def _dn_local(proj, proj_ab, cw, alog_row, dtb_row, selg, selb, b, s):
    t = b * s
    nt = s // TT_SEQ
    rows_per_halo = TT_SEQ // CONV_HALO

    def main_spec(cb):
        return pl.BlockSpec((TT_SEQ, DN_QK), lambda bi, i: (bi * nt + i, cb))

    def halo_spec(cb):
        return pl.BlockSpec(
            (CONV_HALO, DN_QK), lambda bi, i: (jnp.maximum((bi * nt + i) * rows_per_halo - 1, 0), cb))

    def full(shape):
        return pl.BlockSpec(shape, lambda bi, i: (0,) * len(shape))

    tok_spec = pl.BlockSpec((TT_SEQ, DN_V), lambda bi, i: (bi * nt + i, 0))
    return pl.pallas_call(
        _dn_local_body,
        grid=(b, nt),
        in_specs=[main_spec(0), main_spec(1), main_spec(2), halo_spec(0), halo_spec(1), halo_spec(2),
                  pl.BlockSpec((TT_SEQ, LANE), lambda bi, i: (bi * nt + i, 0)),
                  full((CONV_W, CONV_CH)), full((1, LANE)), full((1, LANE)),
                  full((LANE, DN_V)), full((LANE, DN_V))],
        out_specs=[tok_spec, tok_spec, tok_spec, tok_spec, tok_spec,
                   pl.BlockSpec((1, HALO, DN_V), lambda bi, i: (bi * nt + i, 0, 0))],
        out_shape=[jax.ShapeDtypeStruct((t, DN_V), BF16),
                   jax.ShapeDtypeStruct((t, DN_V), BF16),
                   jax.ShapeDtypeStruct((t, DN_V), F32),
                   jax.ShapeDtypeStruct((t, DN_V), BF16),
                   jax.ShapeDtypeStruct((t, DN_V), F32),
                   jax.ShapeDtypeStruct((t // TT_SEQ, HALO, DN_V), F32)],
        scratch_shapes=[pltpu.VMEM((CONV_HALO + TT_SEQ, DN_QK), F32)],
        compiler_params=_cparams(("parallel", "parallel")),
        name="dn_local",
    )(proj, proj, proj, proj, proj, proj, proj_ab, cw, alog_row, dtb_row, selg, selb)


def _dn_scan_body(qa_ref, kd_ref, u_ref, w_ref, o0_ref, gend_ref, o_ref, s_ref):
    nb = qa_ref.shape[0]

    @pl.when(pl.program_id(0) == 0)
    def _():
        s_ref[...] = jnp.zeros(s_ref.shape, F32)

    chains = [(b, h, slice(h * DN_DK, (h + 1) * DN_DK)) for b in range(nb) for h in range(DN_HEADS)]
    states = [s_ref[n] for n in range(len(chains))]
    for c in range(TT_SEQ // CHUNK):
        rs = slice(c * CHUNK, (c + 1) * CHUNK)
        both = [jnp.dot(jnp.concatenate([w_ref[b, rs, hs], qa_ref[b, rs, hs]], axis=0),
                        states[n].astype(BF16), preferred_element_type=F32)
                for n, (b, h, hs) in enumerate(chains)]
        for n, (b, h, hs) in enumerate(chains):
            o_ref[b, rs, hs] = both[n][CHUNK:] + o0_ref[b, rs, hs]
        states = [states[n] * gend_ref[b, 0, c:c + 1, hs]
                  + _mm_tn(kd_ref[b, rs, hs], u_ref[b, rs, hs] - both[n][:CHUNK])
                  for n, (b, h, hs) in enumerate(chains)]
    for n in range(len(chains)):
        s_ref[n] = states[n]


def _dn_scan(qa, kd, u, w, o0, gend, b, s):
    nt = s // TT_SEQ
    tok = lambda a: a.reshape(b, s, DN_V)
    tok_spec = pl.BlockSpec((b, TT_SEQ, DN_V), lambda i: (0, i, 0))
    out = pl.pallas_call(
        _dn_scan_body,
        grid=(nt,),
        in_specs=[tok_spec, tok_spec, tok_spec, tok_spec, tok_spec,
                  pl.BlockSpec((b, 1, HALO, DN_V), lambda i: (0, i, 0, 0))],
        out_specs=tok_spec,
        out_shape=jax.ShapeDtypeStruct((b, s, DN_V), F32),
        scratch_shapes=[pltpu.VMEM((b * DN_HEADS, DN_DK, DN_DV), F32)],
        compiler_params=_cparams(("arbitrary",)),
        name="dn_scan",
    )(tok(qa), tok(kd), tok(u), tok(w), tok(o0), gend.reshape(b, nt, HALO, DN_V))
    return out.reshape(b * s, DN_V)


def _log_gamma(h):
    return math.log1p(-(2.0 ** (-5.0 - h)))


def _ret_body(qk_ref, v_ref, pos_ref, freq_ref, o_ref, s_ref):
    tt = TT_SEQ

    @pl.when(pl.program_id(1) == 0)
    def _():
        s_ref[...] = jnp.zeros(s_ref.shape, F32)

    ang = pos_ref[...].astype(F32) * freq_ref[...]
    lane = lax.broadcasted_iota(jnp.int32, (tt, LANE), 1)
    first_half = (lane & (RET_DK // 2)) == 0
    cos = jnp.cos(ang)
    sin = jnp.sin(ang)
    ssin = jnp.where(first_half, -sin, sin)

    def rotary(x):
        swapped = jnp.where(first_half, pltpu.roll(x, LANE - RET_DK // 2, axis=1),
                            pltpu.roll(x, RET_DK // 2, axis=1))
        return x * cos + swapped * ssin

    row = lax.broadcasted_iota(jnp.int32, (tt, tt), 0)
    col = lax.broadcasted_iota(jnp.int32, (tt, tt), 1)
    rel = (row - col).astype(F32)
    causal = row >= col
    trow = lax.broadcasted_iota(jnp.int32, (tt, LANE), 0).astype(F32)
    srow = lax.broadcasted_iota(jnp.int32, (LANE, LANE), 0)

    for pr in range(RET_HEADS // 2):
        ps = slice(pr * LANE, (pr + 1) * LANE)
        qp = rotary(qk_ref[:, ps].astype(F32))
        kp = rotary(qk_ref[:, RET_QK + pr * LANE:RET_QK + (pr + 1) * LANE].astype(F32)) * (RET_DK ** -0.5)
        kpb = kp.astype(BF16)
        state = s_ref[pr]
        sb = state.astype(BF16)
        update = jnp.zeros((LANE, RET_DV), F32)
        for hh in range(2):
            h = 2 * pr + hh
            lg = _log_gamma(h)
            mine = (lane < RET_DK) if hh == 0 else (lane >= RET_DK)
            qm = jnp.where(mine, qp, 0.0).astype(BF16)
            vh = v_ref[:, h * RET_DV:(h + 1) * RET_DV].astype(BF16)
            att = _mm_nt(qm, kpb) * jnp.exp(jnp.where(causal, rel * lg, NEG_INF))
            inner = jnp.dot(att.astype(BF16), vh, preferred_element_type=F32)
            cross = jnp.dot(qm, sb, preferred_element_type=F32) * jnp.exp((trow + 1.0) * lg)
            o_ref[:, h * RET_DV:(h + 1) * RET_DV] = inner + cross
            kz = jnp.where(mine, kp, 0.0) * jnp.exp((tt - 1.0 - trow) * lg)
            update = update + _mm_tn(kz, vh)
        decay = jnp.where(srow < RET_DK, math.exp(tt * _log_gamma(2 * pr)),
                          math.exp(tt * _log_gamma(2 * pr + 1)))
        s_ref[pr] = state * decay + update


def _ret(proj, pos, freq_row, b, s):
    t = b * s
    nt = s // TT_SEQ
    return pl.pallas_call(
        _ret_body,
        grid=(b, nt),
        in_specs=[pl.BlockSpec((TT_SEQ, 2 * RET_QK), lambda bi, i: (bi * nt + i, 4)),
                  pl.BlockSpec((TT_SEQ, RET_V), lambda bi, i: (bi * nt + i, 5)),
                  pl.BlockSpec((TT_SEQ, 1), lambda bi, i: (bi * nt + i, 0)),
                  pl.BlockSpec((1, LANE), lambda bi, i: (0, 0))],
        out_specs=pl.BlockSpec((TT_SEQ, RET_V), lambda bi, i: (bi * nt + i, 0)),
        out_shape=jax.ShapeDtypeStruct((t, RET_V), F32),
        scratch_shapes=[pltpu.VMEM((RET_HEADS // 2, LANE, RET_DV), F32)],
        compiler_params=_cparams(("parallel", "arbitrary")),
        name="ret",
    )(proj, proj, pos, freq_row)


def _outproj_body(odn_ref, z_ref, or_ref, rg_ref, x_ref, dnw_ref, gnw_ref, gnb_ref, w_ref, o_ref,
                  mix_ref):
    for h in range(DN_HEADS):
        hs = slice(h * DN_DV, (h + 1) * DN_DV)
        mix_ref[:, hs] = (_rms(odn_ref[:, hs], dnw_ref[...])
                          * _silu(z_ref[:, hs].astype(F32))).astype(BF16)
    for h in range(RET_HEADS):
        hs = slice(h * RET_DV, (h + 1) * RET_DV)
        o = or_ref[:, hs]
        cen = o - jnp.mean(o, axis=-1, keepdims=True)
        y = cen * lax.rsqrt(jnp.mean(cen * cen, axis=-1, keepdims=True) + EPS)
        y = (y * gnw_ref[:, hs] + gnb_ref[:, hs]) * _silu(rg_ref[:, hs].astype(F32))
        mix_ref[:, DN_V + h * RET_DV:DN_V + (h + 1) * RET_DV] = y.astype(BF16)
    o_ref[...] = x_ref[...] + jnp.dot(mix_ref[...], w_ref[...], preferred_element_type=F32)


def _outproj(o_dn, proj, o_r, x, dnw, gnw, gnb, w_out):
    t, d = x.shape
    tm = TM_DENSE
    full = lambda shape: pl.BlockSpec(shape, lambda i: (0, 0))
    return pl.pallas_call(
        _outproj_body,
        grid=(t // tm,),
        in_specs=[pl.BlockSpec((tm, DN_V), lambda i: (i, 0)),
                  pl.BlockSpec((tm, DN_V), lambda i: (i, 3)),
                  pl.BlockSpec((tm, RET_V), lambda i: (i, 0)),
                  pl.BlockSpec((tm, RET_V), lambda i: (i, 6)),
                  pl.BlockSpec((tm, d), lambda i: (i, 0)),
                  full((1, DN_DV)), full((1, RET_V)), full((1, RET_V)), full((DN_V + RET_V, d))],
        out_specs=pl.BlockSpec((tm, d), lambda i: (i, 0)),
        out_shape=jax.ShapeDtypeStruct((t, d), F32),
        scratch_shapes=[pltpu.VMEM((tm, DN_V + RET_V), BF16)],
        compiler_params=_cparams(("parallel",)),
        name="outproj",
    )(o_dn, proj, o_r, proj, x, dnw, gnw, gnb, w_out)


def _pq_body(h_ref, nw_ref, w_ref, keys_ref, hnt_ref, inv_ref, sc_ref):
    hn = _rms(h_ref[...], nw_ref[...])
    hnt = hn.T
    amax = jnp.maximum(jnp.max(jnp.abs(hnt), axis=0, keepdims=True), F8_TINY)
    hnt_ref[...] = (hnt * (F8_TARGET / amax)).astype(F8)
    inv_ref[...] = amax * (1.0 / F8_TARGET)
    q = jnp.dot(hn.astype(BF16), w_ref[...], preferred_element_type=F32)
    half = PK_DQ // 2
    for hp in range(2 * PK_HEADS):
        sc_ref[hp] = _mm_nt(keys_ref[hp], q[:, hp * half:(hp + 1) * half])


def _pq(h1, nw, w_pq, keys):
    t, d = h1.shape
    tm = TM_DENSE
    nk = 2 * PK_HEADS
    return pl.pallas_call(
        _pq_body,
        grid=(t // tm,),
        in_specs=[pl.BlockSpec((tm, d), lambda i: (i, 0)),
                  pl.BlockSpec((1, d), lambda i: (0, 0)),
                  pl.BlockSpec((d, PK_HEADS * PK_DQ), lambda i: (0, 0)),
                  pl.BlockSpec((nk, N_KEYS, PK_DQ // 2), lambda i: (0, 0, 0))],
        out_specs=[pl.BlockSpec((d, tm), lambda i: (0, i)),
                   pl.BlockSpec((1, tm), lambda i: (0, i)),
                   pl.BlockSpec((nk, N_KEYS, tm), lambda i: (0, 0, i))],
        out_shape=[jax.ShapeDtypeStruct((d, t), F8),
                   jax.ShapeDtypeStruct((1, t), F32),
                   jax.ShapeDtypeStruct((nk, N_KEYS, t), F32)],
        compiler_params=_cparams(("parallel",)),
        name="pq",
    )(h1, nw, w_pq, keys)


def _route(a, bsc, exact):
    tl = a.shape[1]
    k = PK_TOPK
    keyid = lax.broadcasted_iota(jnp.int32, (N_KEYS, tl), 0)
    slot = lax.broadcasted_iota(jnp.int32, (k, tl), 0)

    def extract(s, ids, n_ids):
        m = jnp.max(s, axis=0, keepdims=True)
        hit = s == m
        if exact:
            hit = ids == jnp.min(jnp.where(hit, ids, n_ids), axis=0, keepdims=True)
        return m, hit

    def top_k(s):
        rank = jnp.full((N_KEYS, tl), float(k), F32)
        vals = jnp.zeros((k, tl), F32)
        for r in range(k):
            m, hit = extract(s, keyid, N_KEYS)
            rank = jnp.where(hit, float(r), rank)
            s = jnp.where(hit, NEG_INF, s)
            vals = jnp.where(slot == r, m, vals)
        return vals, rank

    av, rank_a = top_k(a)
    bv, rank_b = top_k(bsc)

    k2 = k // 2
    assert k2 & (k2 - 1) == 0
    n_cand = k + (k - 1) * k2
    cand = jnp.concatenate([av[0:1, :] + bv] + [av[r:r + 1, :] + bv[0:k2, :] for r in range(1, k)],
                           axis=0)
    row = lax.broadcasted_iota(jnp.int32, (n_cand, tl), 0)
    tail = row - k
    cid = jnp.where(row < k, row,
                    (1 + (tail >> (k2.bit_length() - 1))) * k + (tail & (k2 - 1)))
    work = cand
    for _ in range(k):
        _, hit = extract(work, cid, k * k)
        work = jnp.where(hit, NEG_INF, work)
    sel = (work == NEG_INF).astype(F32)
    zsum = jnp.sum(sel * jnp.exp(cand - cand[0:1, :]), axis=0, keepdims=True)

    cnt = jnp.zeros((N_KEYS, tl), F32)
    for r in range(k):
        lo, hi = (0, k) if r == 0 else (k + (r - 1) * k2, k + r * k2)
        cnt_r = jnp.sum(sel[lo:hi, :], axis=0, keepdims=True)
        cnt = jnp.where(rank_a == float(r), cnt_r, cnt)

    def full_count(x):
        return jnp.sum(x, axis=0, keepdims=True) == float(k)

    ok = (full_count((rank_a < float(k)).astype(F32)) & full_count((rank_b < float(k)).astype(F32))
          & full_count(sel))
    eb = jnp.exp(bsc - bv[0:1, :])
    ea = jnp.exp(a - av[0:1, :]) / zsum
    return rank_b, eb, cnt, ea, ok


def _router_body(sc_ref, rankb_ref, eb_ref, cnt_ref, ea_ref):
    def run(exact):
        rank_b, eb, cnt, ea, ok = _route(sc_ref[0], sc_ref[1], exact)
        rankb_ref[0] = rank_b.astype(BF16)
        eb_ref[0] = eb.astype(BF16)
        cnt_ref[0] = cnt
        ea_ref[0] = ea
        return ok

    ok = run(exact=False)
    n_bad = jnp.sum(jnp.where(ok, 0.0, 1.0), axis=1, keepdims=True)

    @pl.when(n_bad[0, 0] > 0.0)
    def _():
        run(exact=True)


def _router(scores):
    nk, n, t = scores.shape
    tl = TL_ROUTER
    spec = pl.BlockSpec((1, n, tl), lambda h, j: (h, 0, j))
    shape = lambda dt: jax.ShapeDtypeStruct((PK_HEADS, n, t), dt)
    return pl.pallas_call(
        _router_body,
        grid=(PK_HEADS, t // tl),
        in_specs=[pl.BlockSpec((2, n, tl), lambda h, j: (h, 0, j))],
        out_specs=[spec, spec, spec, spec],
        out_shape=[shape(BF16), shape(BF16), shape(F32), shape(F32)],
        compiler_params=_cparams(("parallel", "parallel")),
        name="router",
    )(scores)


def _u_prep_body(u_ref, o_ref, inv_ref):
    u = u_ref[...]
    amax = jnp.max(jnp.max(jnp.abs(u), axis=0, keepdims=True), axis=1, keepdims=True)
    amax = jnp.maximum(amax, F8_TINY)
    o_ref[...] = (u * (F8_TARGET / amax)).astype(F8)
    inv_ref[0] = jnp.broadcast_to(amax * (1.0 / F8_TARGET), inv_ref.shape[1:])


def _u_prep(u):
    ne, d = u.shape
    te, tm = TE_PEER, TM_PEER
    return pl.pallas_call(
        _u_prep_body,
        grid=(ne // te,),
        in_specs=[pl.BlockSpec((te, d), lambda j: (j, 0))],
        out_specs=[pl.BlockSpec((te, d), lambda j: (j, 0)), pl.BlockSpec((1, 1, tm), lambda j: (j, 0, 0))],
        out_shape=[jax.ShapeDtypeStruct((ne, d), F8), jax.ShapeDtypeStruct((ne // te, 1, tm), F32)],
        compiler_params=_cparams(("parallel",)),
        name="u_prep",
    )(u)


def _v_prep_body(v_ref, o_ref):
    o_ref[...] = v_ref[...].T.astype(BF16)


def _v_prep(v):
    ne, d = v.shape
    te = TE_PEER
    return pl.pallas_call(
        _v_prep_body,
        grid=(ne // te,),
        in_specs=[pl.BlockSpec((te, d), lambda j: (j, 0))],
        out_specs=pl.BlockSpec((d, te), lambda j: (0, j)),
        out_shape=jax.ShapeDtypeStruct((d, ne), BF16),
        compiler_params=_cparams(("parallel",)),
        name="v_prep",
    )(v)


def _peer_pre_body(u_ref, hnt_ref, o_ref):
    o_ref[...] = jnp.dot(u_ref[...], hnt_ref[...], preferred_element_type=F32)


def _peer_step(pre_ref, pre_next_ref, hnt_ref, u_ref, vt_ref, rankb_ref, eb_ref, cnt_ref, ea_ref,
               xinv_ref, uinv_ref, acc_ref, act_ref):
    pre_next_ref[...] = jnp.dot(u_ref[...], hnt_ref[...], preferred_element_type=F32)
    unscale = xinv_ref[...] * uinv_ref[0]
    for il in range(NI_PEER):
        rs = slice(il * N_KEYS, (il + 1) * N_KEYS)
        gate = jnp.zeros((N_KEYS, TM_PEER), BF16)
        for h in range(PK_HEADS):
            picked = rankb_ref[h] < cnt_ref[h, il:il + 1, :].astype(BF16)
            gate = gate + jnp.where(picked, eb_ref[h], 0.0) * ea_ref[h, il:il + 1, :].astype(BF16)
        act_ref[rs, :] = _gelu(pre_ref[rs, :] * unscale).astype(BF16) * gate
    acc_ref[...] += jnp.dot(vt_ref[...], act_ref[...], preferred_element_type=F32)


def _peer_body(pre0_ref, hnt_ref, u_ref, vt_ref, rankb_ref, eb_ref, cnt_ref, ea_ref, xinv_ref,
               uinv_ref, o_ref, acc_ref, act_ref, pre_a_ref, pre_b_ref):
    i = pl.program_id(0)
    j = pl.program_id(1)
    args = (hnt_ref, u_ref, vt_ref, rankb_ref, eb_ref, cnt_ref, ea_ref, xinv_ref, uinv_ref, acc_ref,
            act_ref)

    @pl.when((i == 0) & (j == 0))
    def _():
        pre_a_ref[...] = pre0_ref[...]

    @pl.when(j == 0)
    def _():
        acc_ref[...] = jnp.zeros(acc_ref.shape, F32)

    @pl.when(lax.rem(j, 2) == 0)
    def _():
        _peer_step(pre_a_ref, pre_b_ref, *args)

    @pl.when(lax.rem(j, 2) == 1)
    def _():
        _peer_step(pre_b_ref, pre_a_ref, *args)

    @pl.when(j == pl.num_programs(1) - 1)
    def _():
        o_ref[...] = acc_ref[...].T


def _peer(hnt, xinv, u_f8, uinv, vt_bf, rankb, eb, cnt, ea):
    d, t = hnt.shape
    ne = u_f8.shape[0]
    tm, te = TM_PEER, TE_PEER
    ni, nj = t // tm, ne // te
    assert nj % 2 == 0, "pre-activation buffers alternate with the expert-tile index"
    pre0 = pl.pallas_call(
        _peer_pre_body,
        grid=(1,),
        in_specs=[pl.BlockSpec((te, d), lambda i: (0, 0)), pl.BlockSpec((d, tm), lambda i: (0, 0))],
        out_specs=pl.BlockSpec((te, tm), lambda i: (0, 0)),
        out_shape=jax.ShapeDtypeStruct((te, tm), F32),
        compiler_params=_cparams(("arbitrary",)),
        name="peer_pre",
    )(u_f8, hnt)

    def next_i(i, j):
        return jnp.minimum(i + (j + 1) // nj, ni - 1)

    key_spec = pl.BlockSpec((PK_HEADS, N_KEYS, tm), lambda i, j: (0, 0, i))
    blk_spec = pl.BlockSpec((PK_HEADS, NI_PEER, tm), lambda i, j: (0, j, i))
    return pl.pallas_call(
        _peer_body,
        grid=(ni, nj),
        in_specs=[pl.BlockSpec((te, tm), lambda i, j: (0, 0)),
                  pl.BlockSpec((d, tm), lambda i, j: (0, next_i(i, j))),
                  pl.BlockSpec((te, d), lambda i, j: ((j + 1) % nj, 0)),
                  pl.BlockSpec((d, te), lambda i, j: (0, j)),
                  key_spec, key_spec, blk_spec, blk_spec,
                  pl.BlockSpec((1, tm), lambda i, j: (0, i)),
                  pl.BlockSpec((1, 1, tm), lambda i, j: (j, 0, 0))],
        out_specs=pl.BlockSpec((tm, d), lambda i, j: (i, 0)),
        out_shape=jax.ShapeDtypeStruct((t, d), F32),
        scratch_shapes=[pltpu.VMEM((d, tm), F32), pltpu.VMEM((te, tm), BF16),
                        pltpu.VMEM((te, tm), F32), pltpu.VMEM((te, tm), F32)],
        compiler_params=_cparams(("arbitrary", "arbitrary")),
        name="peer",
    )(pre0, hnt, u_f8, vt_bf, rankb, eb, cnt, ea, xinv, uinv)


def _ple_body(h_ref, po_ref, p_ref, nple_ref, wg_ref, wp_ref, nfin_ref, o_ref):
    h2 = h_ref[...] + po_ref[...]
    gate = jax.nn.sigmoid(
        jnp.dot(_rms(h2, nple_ref[...]).astype(BF16), wg_ref[...], preferred_element_type=F32))
    ple = jnp.dot(p_ref[...].astype(BF16), wp_ref[...], preferred_element_type=F32)
    o_ref[...] = _rms(h2 + gate * ple, nfin_ref[...])


def _ple(h1, peer_out, p, nple, wg, wp, nfin):
    t, d = h1.shape
    tm = TM_DENSE
    full = lambda shape: pl.BlockSpec(shape, lambda i: (0, 0))
    return pl.pallas_call(
        _ple_body,
        grid=(t // tm,),
        in_specs=[pl.BlockSpec((tm, d), lambda i: (i, 0)),
                  pl.BlockSpec((tm, d), lambda i: (i, 0)),
                  pl.BlockSpec((tm, PLE_DIM), lambda i: (i, 0)),
                  full((1, d)), full((d, d)), full((PLE_DIM, d)), full((1, d))],
        out_specs=pl.BlockSpec((tm, d), lambda i: (i, 0)),
        out_shape=jax.ShapeDtypeStruct((t, d), F32),
        compiler_params=_cparams(("parallel",)),
        name="ple",
    )(h1, peer_out, p, nple, wg, wp, nfin)


def _lane_row(vec, offset):
    return jnp.zeros((1, LANE), F32).at[0, offset:offset + vec.shape[0]].set(vec.astype(F32))


def _head_selector(offset):
    sel = np.zeros((LANE, DN_V), np.float32)
    for h in range(DN_HEADS):
        sel[offset + h, h * DN_DV:(h + 1) * DN_DV] = 1.0
    return jnp.asarray(sel, dtype=BF16)


def kernel(x, p, positions, norm_mix, w_in, conv_w, a_log, dt_bias, dn_norm, ret_gn_w, ret_gn_b,
           w_out, norm_ffn, w_pq, sub_keys, expert_u, expert_v, norm_ple, w_ple_gate, w_ple_proj,
           norm_final):
    b, s, d = x.shape
    t = b * s
    depth = w_in.shape[0]
    assert depth == 1, "the final rms_norm is fused into the single layer's ple kernel"
    half = RET_DK // 2
    inv_freq = ROPE_BASE ** (-jnp.arange(half, dtype=F32) / half)
    freq_row = jnp.tile(inv_freq, LANE // half).reshape(1, LANE)
    pos = positions.reshape(t, 1)
    selg, selb = _head_selector(0), _head_selector(DN_HEADS)

    h = x.reshape(t, d)
    for i in range(depth):
        w = w_in[i].astype(BF16)
        w_ab = jnp.pad(w[:, AB_OFF:AB_OFF + 2 * DN_HEADS], ((0, 0), (0, LANE - 2 * DN_HEADS)))
        proj, proj_ab = _inproj(h, norm_mix[i].reshape(1, d), w, w[:, AB_OFF + 2 * DN_HEADS:], w_ab)

        qa, kd, u, wv, o0, gend = _dn_local(
            proj, proj_ab, conv_w[i].astype(F32).T, _lane_row(a_log[i], 0), _lane_row(dt_bias[i], 0),
            selg, selb, b, s)
        o_dn = _dn_scan(qa, kd, u, wv, o0, gend, b, s)
        o_r = _ret(proj, pos, freq_row, b, s)
        h1 = _outproj(o_dn, proj, o_r, h, dn_norm[i].reshape(1, DN_DV), ret_gn_w[i].reshape(1, RET_V),
                      ret_gn_b[i].reshape(1, RET_V), w_out[i].astype(BF16))

        keys = sub_keys[i].reshape(2 * PK_HEADS, N_KEYS, PK_DQ // 2).astype(BF16)
        hnt, xinv, scores = _pq(h1, norm_ffn[i].reshape(1, d), w_pq[i].astype(BF16), keys)
        rankb, eb, cnt, ea = _router(scores)
        u_f8, uinv = _u_prep(expert_u[i])
        peer_out = _peer(hnt, xinv, u_f8, uinv, _v_prep(expert_v[i]), rankb, eb, cnt, ea)

        h = _ple(h1, peer_out, p[i].reshape(t, PLE_DIM), norm_ple[i].reshape(1, d),
                 w_ple_gate[i].astype(BF16), w_ple_proj[i].astype(BF16), norm_final.reshape(1, d))
    return h.reshape(b, s, d)
```

```python
import functools
import math

import numpy as np
import jax
import jax.numpy as jnp
from jax import lax
from jax.experimental import pallas as pl
from jax.experimental.pallas import tpu as pltpu

F32 = jnp.float32
BF16 = jnp.bfloat16
F8 = jnp.float8_e4m3fn
F8_TARGET = 224.0
F8_TINY = 1e-30
NEG_INF = float("-inf")

EPS = 1e-6
D_MODEL = 2048
DN_HEADS = 8
DN_DK = 128
DN_DV = 128
CONV_W = 4
RET_HEADS = 8
RET_DK = 64
RET_DV = 128
CHUNK = 64
ROPE_BASE = 10000.0
N_KEYS = 128
PK_HEADS = 8
PK_DQ = 256
PK_TOPK = 16
PLE_DIM = 256

DN_QK = DN_HEADS * DN_DK
DN_V = DN_HEADS * DN_DV
CONV_CH = 2 * DN_QK + DN_V
RET_QK = RET_HEADS * RET_DK
RET_V = RET_HEADS * RET_DV
AB_OFF = CONV_CH + DN_V
MAIN_COLS = CONV_CH + DN_V + 2 * RET_QK + 2 * RET_V
LANE = 128
HALO = 8
CONV_HALO = 16

TM_INPROJ = 1024
TN_INPROJ = 1024
TT_SEQ = 256
TM_DENSE = 256
TL_ROUTER = 512
TM_PEER = 512
NI_PEER = 8
TE_PEER = NI_PEER * N_KEYS
VMEM_LIMIT = 56 * 1024 * 1024


def _cparams(sem, flags=None):
    return pltpu.CompilerParams(dimension_semantics=sem, vmem_limit_bytes=VMEM_LIMIT, flags=flags)


def _rms(x, w):
    return x * lax.rsqrt(jnp.mean(x * x, axis=-1, keepdims=True) + EPS) * w


def _silu(x):
    return x * jax.nn.sigmoid(x)


def _softplus(x):
    return jnp.maximum(x, 0.0) + jnp.log1p(jnp.exp(-jnp.abs(x)))


def _gelu(x):
    return 0.5 * x * (1.0 + lax.erf(x * (2.0 ** -0.5)))


def _mm(a, b):
    return jnp.dot(a.astype(BF16), b.astype(BF16), preferred_element_type=F32)


def _mm_nt(a, b):
    return lax.dot_general(a.astype(BF16), b.astype(BF16), (((1,), (1,)), ((), ())),
                           preferred_element_type=F32)


def _mm_tn(a, b):
    return lax.dot_general(a.astype(BF16), b.astype(BF16), (((0,), (0,)), ((), ())),
                           preferred_element_type=F32)


def _rms_matmul(x, norm_w, w_ref, k_chunk=256):
    acc = None
    sumsq = None
    for c in range(x.shape[1] // k_chunk):
        cs = slice(c * k_chunk, (c + 1) * k_chunk)
        xc = x[:, cs]
        part = jnp.sum(xc * xc, axis=-1, keepdims=True)
        sumsq = part if sumsq is None else sumsq + part
        prod = jnp.dot((xc * norm_w[:, cs]).astype(BF16), w_ref[cs, :], preferred_element_type=F32)
        acc = prod if acc is None else acc + prod
    inv_rms = lax.rsqrt(sumsq * (1.0 / x.shape[1]) + EPS)
    return acc * inv_rms, inv_rms


def _split3(x):
    hi = x.astype(BF16)
    rest = x - hi.astype(F32)
    mid = rest.astype(BF16)
    lo = (rest - mid.astype(F32)).astype(BF16)
    return hi, mid, lo


def _select_mm(a, b, dims=(((1,), (0,)), ((), ()))):
    if a.dtype == BF16:
        parts = [lax.dot_general(a, p, dims, preferred_element_type=F32) for p in _split3(b)]
    else:
        parts = [lax.dot_general(p, b, dims, preferred_element_type=F32) for p in _split3(a)]
    return parts[0] + parts[1] + parts[2]


def _inproj_body(x_ref, nw_ref, w_head_ref, w_tail_ref, wab_ref, o_ref, oab_ref, hn_ref):
    j = pl.program_id(1)
    n_head = AB_OFF // TN_INPROJ

    @pl.when(j == 0)
    def _():
        hn_ref[...] = _rms(x_ref[...], nw_ref[...]).astype(BF16)
        oab_ref[...] = jnp.dot(hn_ref[...], wab_ref[...], preferred_element_type=F32)

    @pl.when(j < n_head)
    def _():
        o_ref[...] = jnp.dot(hn_ref[...], w_head_ref[...], preferred_element_type=F32).astype(BF16)

    @pl.when(j >= n_head)
    def _():
        o_ref[...] = jnp.dot(hn_ref[...], w_tail_ref[...], preferred_element_type=F32).astype(BF16)


def _inproj(x, nw, w_all, w_tail, w_ab):
    t, d = x.shape
    n_head = AB_OFF // TN_INPROJ
    n = AB_OFF + w_tail.shape[1]
    return pl.pallas_call(
        _inproj_body,
        grid=(t // TM_INPROJ, n // TN_INPROJ),
        in_specs=[
            pl.BlockSpec((TM_INPROJ, d), lambda i, j: (i, 0)),
            pl.BlockSpec((1, d), lambda i, j: (0, 0)),
            pl.BlockSpec((d, TN_INPROJ), lambda i, j: (0, jnp.minimum(j, n_head - 1))),
            pl.BlockSpec((d, TN_INPROJ), lambda i, j: (0, jnp.maximum(j - n_head, 0))),
            pl.BlockSpec((d, LANE), lambda i, j: (0, 0)),
        ],
        out_specs=[
            pl.BlockSpec((TM_INPROJ, TN_INPROJ), lambda i, j: (i, j)),
            pl.BlockSpec((TM_INPROJ, LANE), lambda i, j: (i, 0)),
        ],
        out_shape=[jax.ShapeDtypeStruct((t, n), BF16), jax.ShapeDtypeStruct((t, LANE), F32)],
        scratch_shapes=[pltpu.VMEM((TM_INPROJ, d), BF16)],
        compiler_params=_cparams(("parallel", "arbitrary")),
        name="inproj",
    )(x, nw, w_all, w_tail, w_ab)


def _dn_local_body(q_ref, k_ref, v_ref, qh_ref, kh_ref, vh_ref, ab_ref, cw_ref, alog_ref, dtb_ref,
                   selg_ref, selb_ref, qa_ref, kd_ref, u_ref, w_ref, o0_ref, gend_ref, buf_ref):
    tt = TT_SEQ
    first_tile = pl.program_id(1) == 0

    def conv_silu(main_ref, halo_ref, c0):
        halo = halo_ref[...].astype(F32)
        buf_ref[0:CONV_HALO, :] = jnp.where(first_tile, jnp.zeros_like(halo), halo)
        buf_ref[CONV_HALO:CONV_HALO + tt, :] = main_ref[...].astype(F32)
        y = None
        for j in range(CONV_W):
            off = CONV_HALO - (CONV_W - 1) + j
            term = buf_ref[off:off + tt, :] * cw_ref[j:j + 1, c0:c0 + DN_QK]
            y = term if y is None else y + term
        return _silu(y)

    q = conv_silu(q_ref, qh_ref, 0)
    k = conv_silu(k_ref, kh_ref, DN_QK)
    v = conv_silu(v_ref, vh_ref, 2 * DN_QK)

    row = lax.broadcasted_iota(jnp.int32, (tt, tt), 0)
    col = lax.broadcasted_iota(jnp.int32, (tt, tt), 1)

    def same_block(bits):
        return (row >> bits) == (col >> bits)

    blk8, blk16, blk32, blk64 = same_block(3), same_block(4), same_block(5), same_block(6)
    causal = blk64 & (col <= row)
    strict = blk64 & (col < row)
    eye = (row == col).astype(F32)

    ab = ab_ref[...]
    g = -jnp.exp(alog_ref[...]) * _softplus(ab + dtb_ref[...])
    beta = jax.nn.sigmoid(ab)
    def ones_where(mask):
        return jnp.where(mask, 1.0, 0.0).astype(BF16)

    gc = _select_mm(ones_where(causal), g)
    gtot = _select_mm(ones_where(blk64), g)
    e_gc = _select_mm(gc, selg_ref[...])
    e_gt = _select_mm(gtot, selg_ref[...])
    e_beta = _select_mm(beta, selb_ref[...])
    r128 = lax.broadcasted_iota(jnp.int32, (LANE, LANE), 0)
    c128 = lax.broadcasted_iota(jnp.int32, (LANE, LANE), 1)
    gc_t = _select_mm(ones_where(r128 == c128), gc, (((1,), (1,)), ((), ())))
    eg = jnp.exp(e_gc)
    kdec = jnp.exp(e_gt - e_gc)

    for c in range(tt // CHUNK):
        gend_ref[0, c:c + 1, :] = jnp.exp(e_gt[c * CHUNK:c * CHUNK + 1, :])
    gend_ref[0, tt // CHUNK:, :] = jnp.zeros((HALO - tt // CHUNK, DN_V), F32)

    heads = range(DN_HEADS)
    hsl = [slice(h * DN_DK, (h + 1) * DN_DK) for h in heads]

    def per_head(fn):
        return [fn(h) for h in heads]

    def l2n(x):
        return x * lax.rsqrt(jnp.sum(x * x, axis=-1, keepdims=True) + EPS)

    qh = per_head(lambda h: l2n(q[:, hsl[h]]) * (DN_DK ** -0.5))
    kh = per_head(lambda h: l2n(k[:, hsl[h]]))
    kb = per_head(lambda h: kh[h] * e_beta[:, hsl[h]])
    dmat = per_head(lambda h: jnp.exp(jnp.where(
        causal, jnp.concatenate([e_gc[:, hsl[h]]] * (tt // LANE), axis=1) - gc_t[h:h + 1, :], NEG_INF)))
    scores = per_head(lambda h: _mm_nt(jnp.concatenate([kb[h], qh[h]], axis=0), kh[h]))
    nmat = per_head(lambda h: jnp.where(strict, scores[h][:tt] * dmat[h], 0.0))
    attn = per_head(lambda h: scores[h][tt:] * dmat[h])

    n0 = per_head(lambda h: jnp.where(blk8, nmat[h], 0.0))
    n2 = per_head(lambda h: _mm(n0[h], n0[h]))
    n4 = per_head(lambda h: _mm(n2[h], n2[h]))
    inv = per_head(lambda h: eye - n0[h])
    inv = per_head(lambda h: inv[h] + _mm(inv[h], n2[h]))
    inv = per_head(lambda h: inv[h] + _mm(inv[h], n4[h]))
    for inner, outer in ((blk8, blk16), (blk16, blk32), (blk32, blk64)):
        level = outer & jnp.logical_not(inner)
        prod = per_head(lambda h: _mm(inv[h], jnp.where(level, nmat[h], 0.0)))
        inv = per_head(lambda h: inv[h] - _mm(prod[h], inv[h]))

    sol = per_head(lambda h: _mm(inv[h], jnp.concatenate(
        [v[:, hsl[h]] * e_beta[:, hsl[h]], kb[h] * eg[:, hsl[h]]], axis=1)))
    asol = per_head(lambda h: _mm(attn[h], sol[h]))
    for h in heads:
        hs = hsl[h]
        u_ref[:, hs] = sol[h][:, :DN_DV]
        w_ref[:, hs] = sol[h][:, DN_DV:].astype(BF16)
        o0_ref[:, hs] = asol[h][:, :DN_DV]
        qa_ref[:, hs] = (qh[h] * eg[:, hs] - asol[h][:, DN_DV:]).astype(BF16)
        kd_ref[:, hs] = (kh[h] * kdec[:, hs]).astype(BF16)


def _dn_local(proj, proj_ab, cw, alog_row, dtb_row, selg, selb, b, s):
    t = b * s
    nt = s // TT_SEQ
    rows_per_halo = TT_SEQ // CONV_HALO

    def main_spec(cb):
        return pl.BlockSpec((TT_SEQ, DN_QK), lambda bi, i: (bi * nt + i, cb))

    def halo_spec(cb):
        return pl.BlockSpec(
            (CONV_HALO, DN_QK), lambda bi, i: (jnp.maximum((bi * nt + i) * rows_per_halo - 1, 0), cb))

    def full(shape):
        return pl.BlockSpec(shape, lambda bi, i: (0,) * len(shape))

    tok_spec = pl.BlockSpec((TT_SEQ, DN_V), lambda bi, i: (bi * nt + i, 0))
    return pl.pallas_call(
        _dn_local_body,
        grid=(b, nt),
        in_specs=[main_spec(0), main_spec(1), main_spec(2), halo_spec(0), halo_spec(1), halo_spec(2),
                  pl.BlockSpec((TT_SEQ, LANE), lambda bi, i: (bi * nt + i, 0)),
                  full((CONV_W, CONV_CH)), full((1, LANE)), full((1, LANE)),
                  full((LANE, DN_V)), full((LANE, DN_V))],
        out_specs=[tok_spec, tok_spec, tok_spec, tok_spec, tok_spec,
                   pl.BlockSpec((1, HALO, DN_V), lambda bi, i: (bi * nt + i, 0, 0))],
        out_shape=[jax.ShapeDtypeStruct((t, DN_V), BF16),
                   jax.ShapeDtypeStruct((t, DN_V), BF16),
                   jax.ShapeDtypeStruct((t, DN_V), F32),
                   jax.ShapeDtypeStruct((t, DN_V), BF16),
                   jax.ShapeDtypeStruct((t, DN_V), F32),
                   jax.ShapeDtypeStruct((t // TT_SEQ, HALO, DN_V), F32)],
        scratch_shapes=[pltpu.VMEM((CONV_HALO + TT_SEQ, DN_QK), F32)],
        compiler_params=_cparams(("parallel", "parallel")),
        name="dn_local",
    )(proj, proj, proj, proj, proj, proj, proj_ab, cw, alog_row, dtb_row, selg, selb)


def _dn_scan_body(qa_ref, kd_ref, u_ref, w_ref, o0_ref, gend_ref, o_ref, s_ref):
    nb = qa_ref.shape[0]

    @pl.when(pl.program_id(0) == 0)
    def _():
        s_ref[...] = jnp.zeros(s_ref.shape, F32)

    chains = [(b, h, slice(h * DN_DK, (h + 1) * DN_DK)) for b in range(nb) for h in range(DN_HEADS)]
    states = [s_ref[n] for n in range(len(chains))]
    for c in range(TT_SEQ // CHUNK):
        rs = slice(c * CHUNK, (c + 1) * CHUNK)
        both = [jnp.dot(jnp.concatenate([w_ref[b, rs, hs], qa_ref[b, rs, hs]], axis=0),
                        states[n].astype(BF16), preferred_element_type=F32)
                for n, (b, h, hs) in enumerate(chains)]
        for n, (b, h, hs) in enumerate(chains):
            o_ref[b, rs, hs] = both[n][CHUNK:] + o0_ref[b, rs, hs]
        states = [states[n] * gend_ref[b, 0, c:c + 1, hs]
                  + _mm_tn(kd_ref[b, rs, hs], u_ref[b, rs, hs] - both[n][:CHUNK])
                  for n, (b, h, hs) in enumerate(chains)]
    for n in range(len(chains)):
        s_ref[n] = states[n]


def _dn_scan(qa, kd, u, w, o0, gend, b, s):
    nt = s // TT_SEQ
    tok = lambda a: a.reshape(b, s, DN_V)
    tok_spec = pl.BlockSpec((b, TT_SEQ, DN_V), lambda i: (0, i, 0))
    out = pl.pallas_call(
        _dn_scan_body,
        grid=(nt,),
        in_specs=[tok_spec, tok_spec, tok_spec, tok_spec, tok_spec,
                  pl.BlockSpec((b, 1, HALO, DN_V), lambda i: (0, i, 0, 0))],
        out_specs=tok_spec,
        out_shape=jax.ShapeDtypeStruct((b, s, DN_V), F32),
        scratch_shapes=[pltpu.VMEM((b * DN_HEADS, DN_DK, DN_DV), F32)],
        compiler_params=_cparams(("arbitrary",)),
        name="dn_scan",
    )(tok(qa), tok(kd), tok(u), tok(w), tok(o0), gend.reshape(b, nt, HALO, DN_V))
    return out.reshape(b * s, DN_V)


def _log_gamma(h):
    return math.log1p(-(2.0 ** (-5.0 - h)))


def _ret_body(qk_ref, v_ref, pos_ref, freq_ref, o_ref, s_ref):
    tt = TT_SEQ

    @pl.when(pl.program_id(1) == 0)
    def _():
        s_ref[...] = jnp.zeros(s_ref.shape, F32)

    ang = pos_ref[...].astype(F32) * freq_ref[...]
    lane = lax.broadcasted_iota(jnp.int32, (tt, LANE), 1)
    first_half = (lane & (RET_DK // 2)) == 0
    cos = jnp.cos(ang)
    sin = jnp.sin(ang)
    ssin = jnp.where(first_half, -sin, sin)

    def rotary(x):
        swapped = jnp.where(first_half, pltpu.roll(x, LANE - RET_DK // 2, axis=1),
                            pltpu.roll(x, RET_DK // 2, axis=1))
        return x * cos + swapped * ssin

    row = lax.broadcasted_iota(jnp.int32, (tt, tt), 0)
    col = lax.broadcasted_iota(jnp.int32, (tt, tt), 1)
    rel = (row - col).astype(F32)
    causal = row >= col
    trow = lax.broadcasted_iota(jnp.int32, (tt, LANE), 0).astype(F32)
    srow = lax.broadcasted_iota(jnp.int32, (LANE, LANE), 0)

    for pr in range(RET_HEADS // 2):
        ps = slice(pr * LANE, (pr + 1) * LANE)
        qp = rotary(qk_ref[:, ps].astype(F32))
        kp = rotary(qk_ref[:, RET_QK + pr * LANE:RET_QK + (pr + 1) * LANE].astype(F32)) * (RET_DK ** -0.5)
        kpb = kp.astype(BF16)
        state = s_ref[pr]
        sb = state.astype(BF16)
        update = jnp.zeros((LANE, RET_DV), F32)
        for hh in range(2):
            h = 2 * pr + hh
            lg = _log_gamma(h)
            mine = (lane < RET_DK) if hh == 0 else (lane >= RET_DK)
            qm = jnp.where(mine, qp, 0.0).astype(BF16)
            vh = v_ref[:, h * RET_DV:(h + 1) * RET_DV].astype(BF16)
            att = _mm_nt(qm, kpb) * jnp.exp(jnp.where(causal, rel * lg, NEG_INF))
            inner = jnp.dot(att.astype(BF16), vh, preferred_element_type=F32)
            cross = jnp.dot(qm, sb, preferred_element_type=F32) * jnp.exp((trow + 1.0) * lg)
            o_ref[:, h * RET_DV:(h + 1) * RET_DV] = inner + cross
            kz = jnp.where(mine, kp, 0.0) * jnp.exp((tt - 1.0 - trow) * lg)
            update = update + _mm_tn(kz, vh)
        decay = jnp.where(srow < RET_DK, math.exp(tt * _log_gamma(2 * pr)),
                          math.exp(tt * _log_gamma(2 * pr + 1)))
        s_ref[pr] = state * decay + update


def _ret(proj, pos, freq_row, b, s):
    t = b * s
    nt = s // TT_SEQ
    return pl.pallas_call(
        _ret_body,
        grid=(b, nt),
        in_specs=[pl.BlockSpec((TT_SEQ, 2 * RET_QK), lambda bi, i: (bi * nt + i, 4)),
                  pl.BlockSpec((TT_SEQ, RET_V), lambda bi, i: (bi * nt + i, 5)),
                  pl.BlockSpec((TT_SEQ, 1), lambda bi, i: (bi * nt + i, 0)),
                  pl.BlockSpec((1, LANE), lambda bi, i: (0, 0))],
        out_specs=pl.BlockSpec((TT_SEQ, RET_V), lambda bi, i: (bi * nt + i, 0)),
        out_shape=jax.ShapeDtypeStruct((t, RET_V), F32),
        scratch_shapes=[pltpu.VMEM((RET_HEADS // 2, LANE, RET_DV), F32)],
        compiler_params=_cparams(("parallel", "arbitrary")),
        name="ret",
    )(proj, proj, pos, freq_row)


def _outproj_body(odn_ref, z_ref, or_ref, rg_ref, x_ref, dnw_ref, gnw_ref, gnb_ref, w_ref, o_ref):
    def dn_head(h):
        hs = slice(h * DN_DV, (h + 1) * DN_DV)
        return _rms(odn_ref[:, hs], dnw_ref[...]) * _silu(z_ref[:, hs].astype(F32))

    def ret_head(h):
        hs = slice(h * RET_DV, (h + 1) * RET_DV)
        o = or_ref[:, hs]
        cen = o - jnp.mean(o, axis=-1, keepdims=True)
        y = cen * lax.rsqrt(jnp.mean(cen * cen, axis=-1, keepdims=True) + EPS)
        return (y * gnw_ref[:, hs] + gnb_ref[:, hs]) * _silu(rg_ref[:, hs].astype(F32))

    acc = x_ref[...]
    pair = 2 * DN_DV
    heads = [(dn_head, h) for h in range(DN_HEADS)] + [(ret_head, h) for h in range(RET_HEADS)]
    for c in range(len(heads) // 2):
        (f0, h0), (f1, h1) = heads[2 * c], heads[2 * c + 1]
        mix = jnp.concatenate([f0(h0), f1(h1)], axis=1).astype(BF16)
        acc = acc + jnp.dot(mix, w_ref[c * pair:(c + 1) * pair, :], preferred_element_type=F32)
    o_ref[...] = acc


def _outproj(o_dn, proj, o_r, x, dnw, gnw, gnb, w_out):
    t, d = x.shape
    tm = TM_DENSE
    full = lambda shape: pl.BlockSpec(shape, lambda i: (0, 0))
    return pl.pallas_call(
        _outproj_body,
        grid=(t // tm,),
        in_specs=[pl.BlockSpec((tm, DN_V), lambda i: (i, 0)),
                  pl.BlockSpec((tm, DN_V), lambda i: (i, 3)),
                  pl.BlockSpec((tm, RET_V), lambda i: (i, 0)),
                  pl.BlockSpec((tm, RET_V), lambda i: (i, 6)),
                  pl.BlockSpec((tm, d), lambda i: (i, 0)),
                  full((1, DN_DV)), full((1, RET_V)), full((1, RET_V)), full((DN_V + RET_V, d))],
        out_specs=pl.BlockSpec((tm, d), lambda i: (i, 0)),
        out_shape=jax.ShapeDtypeStruct((t, d), F32),
        compiler_params=_cparams(("parallel",)),
        name="outproj",
    )(o_dn, proj, o_r, proj, x, dnw, gnw, gnb, w_out)


def _pq_body(h_ref, nw_ref, w_ref, keys_ref, hnt_ref, inv_ref, sc_ref):
    x = h_ref[...]
    q, inv_rms = _rms_matmul(x, nw_ref[...], w_ref)
    hnt = (x * inv_rms * nw_ref[...]).T
    amax = jnp.maximum(jnp.max(jnp.abs(hnt), axis=0, keepdims=True), F8_TINY)
    hnt_ref[...] = (hnt * (F8_TARGET / amax)).astype(F8)
    inv_ref[...] = amax * (1.0 / F8_TARGET)
    half = PK_DQ // 2
    for hp in range(2 * PK_HEADS):
        sc_ref[hp] = _mm_nt(keys_ref[hp], q[:, hp * half:(hp + 1) * half])


def _pq(h1, nw, w_pq, keys):
    t, d = h1.shape
    tm = TM_DENSE
    nk = 2 * PK_HEADS
    return pl.pallas_call(
        _pq_body,
        grid=(t // tm,),
        in_specs=[pl.BlockSpec((tm, d), lambda i: (i, 0)),
                  pl.BlockSpec((1, d), lambda i: (0, 0)),
                  pl.BlockSpec((d, PK_HEADS * PK_DQ), lambda i: (0, 0)),
                  pl.BlockSpec((nk, N_KEYS, PK_DQ // 2), lambda i: (0, 0, 0))],
        out_specs=[pl.BlockSpec((d, tm), lambda i: (0, i)),
                   pl.BlockSpec((1, tm), lambda i: (0, i)),
                   pl.BlockSpec((nk, N_KEYS, tm), lambda i: (0, 0, i))],
        out_shape=[jax.ShapeDtypeStruct((d, t), F8),
                   jax.ShapeDtypeStruct((1, t), F32),
                   jax.ShapeDtypeStruct((nk, N_KEYS, t), F32)],
        compiler_params=_cparams(("parallel",)),
        name="pq",
    )(h1, nw, w_pq, keys)


def _route(a, bsc, exact):
    tl = a.shape[1]
    k = PK_TOPK
    keyid = lax.broadcasted_iota(jnp.int32, (N_KEYS, tl), 0)
    slot = lax.broadcasted_iota(jnp.int32, (k, tl), 0)

    def extract(s, ids, n_ids):
        m = jnp.max(s, axis=0, keepdims=True)
        hit = s == m
        if exact:
            hit = ids == jnp.min(jnp.where(hit, ids, n_ids), axis=0, keepdims=True)
        return m, hit

    def top_k(s):
        rank = jnp.full((N_KEYS, tl), float(k), F32)
        vals = jnp.zeros((k, tl), F32)
        for r in range(k):
            m, hit = extract(s, keyid, N_KEYS)
            rank = jnp.where(hit, float(r), rank)
            s = jnp.where(hit, NEG_INF, s)
            vals = jnp.where(slot == r, m, vals)
        return vals, rank

    av, rank_a = top_k(a)
    bv, rank_b = top_k(bsc)

    k2 = k // 2
    assert k2 & (k2 - 1) == 0
    n_cand = k + (k - 1) * k2
    cand = jnp.concatenate([av[0:1, :] + bv] + [av[r:r + 1, :] + bv[0:k2, :] for r in range(1, k)],
                           axis=0)
    row = lax.broadcasted_iota(jnp.int32, (n_cand, tl), 0)
    tail = row - k
    cid = jnp.where(row < k, row,
                    (1 + (tail >> (k2.bit_length() - 1))) * k + (tail & (k2 - 1)))
    work = cand
    for _ in range(k):
        _, hit = extract(work, cid, k * k)
        work = jnp.where(hit, NEG_INF, work)
    sel = (work == NEG_INF).astype(F32)
    zsum = jnp.sum(sel * jnp.exp(cand - cand[0:1, :]), axis=0, keepdims=True)

    cnt = jnp.zeros((N_KEYS, tl), F32)
    for r in range(k):
        lo, hi = (0, k) if r == 0 else (k + (r - 1) * k2, k + r * k2)
        cnt_r = jnp.sum(sel[lo:hi, :], axis=0, keepdims=True)
        cnt = jnp.where(rank_a == float(r), cnt_r, cnt)

    def full_count(x):
        return jnp.sum(x, axis=0, keepdims=True) == float(k)

    ok = (full_count((rank_a < float(k)).astype(F32)) & full_count((rank_b < float(k)).astype(F32))
          & full_count(sel))
    eb = jnp.exp(bsc - bv[0:1, :])
    ea = jnp.exp(a - av[0:1, :]) / zsum
    return rank_b, eb, cnt, ea, ok


def _router_body(sc_ref, rankb_ref, eb_ref, cnt_ref, ea_ref):
    def run(exact):
        rank_b, eb, cnt, ea, ok = _route(sc_ref[0], sc_ref[1], exact)
        rankb_ref[0] = rank_b.astype(BF16)
        eb_ref[0] = eb.astype(BF16)
        cnt_ref[0] = cnt
        ea_ref[0] = ea
        return ok

    ok = run(exact=False)
    n_bad = jnp.sum(jnp.where(ok, 0.0, 1.0), axis=1, keepdims=True)

    @pl.when(n_bad[0, 0] > 0.0)
    def _():
        run(exact=True)


def _router(scores):
    nk, n, t = scores.shape
    tl = TL_ROUTER
    spec = pl.BlockSpec((1, n, tl), lambda h, j: (h, 0, j))
    shape = lambda dt: jax.ShapeDtypeStruct((PK_HEADS, n, t), dt)
    return pl.pallas_call(
        _router_body,
        grid=(PK_HEADS, t // tl),
        in_specs=[pl.BlockSpec((2, n, tl), lambda h, j: (h, 0, j))],
        out_specs=[spec, spec, spec, spec],
        out_shape=[shape(BF16), shape(BF16), shape(F32), shape(F32)],
        compiler_params=_cparams(("parallel", "parallel")),
        name="router",
    )(scores)


def _u_prep_body(u_ref, o_ref, inv_ref):
    u = u_ref[...]
    amax = jnp.max(jnp.max(jnp.abs(u), axis=0, keepdims=True), axis=1, keepdims=True)
    amax = jnp.maximum(amax, F8_TINY)
    o_ref[...] = (u * (F8_TARGET / amax)).astype(F8)
    inv_ref[0] = jnp.broadcast_to(amax * (1.0 / F8_TARGET), inv_ref.shape[1:])


def _u_prep(u):
    ne, d = u.shape
    te, tm = TE_PEER, TM_PEER
    return pl.pallas_call(
        _u_prep_body,
        grid=(ne // te,),
        in_specs=[pl.BlockSpec((te, d), lambda j: (j, 0))],
        out_specs=[pl.BlockSpec((te, d), lambda j: (j, 0)), pl.BlockSpec((1, 1, tm), lambda j: (j, 0, 0))],
        out_shape=[jax.ShapeDtypeStruct((ne, d), F8), jax.ShapeDtypeStruct((ne // te, 1, tm), F32)],
        compiler_params=_cparams(("parallel",)),
        name="u_prep",
    )(u)


def _v_prep_body(v_ref, o_ref):
    o_ref[...] = v_ref[...].T.astype(BF16)


def _v_prep(v):
    ne, d = v.shape
    te = TE_PEER
    return pl.pallas_call(
        _v_prep_body,
        grid=(ne // te,),
        in_specs=[pl.BlockSpec((te, d), lambda j: (j, 0))],
        out_specs=pl.BlockSpec((d, te), lambda j: (0, j)),
        out_shape=jax.ShapeDtypeStruct((d, ne), BF16),
        compiler_params=_cparams(("parallel",)),
        name="v_prep",
    )(v)


def _peer_pre_body(u_ref, hnt_ref, o_ref):
    o_ref[...] = jnp.dot(u_ref[...], hnt_ref[...], preferred_element_type=F32)


def _peer_step(pre_ref, pre_next_ref, hnt_ref, u_ref, vt_ref, rankb_ref, eb_ref, cnt_ref, ea_ref,
               xinv_ref, uinv_ref, acc_ref, act_ref):
    unscale = xinv_ref[...] * uinv_ref[0]
    k_piece = D_MODEL // NI_PEER
    pre_next = None
    for il in range(NI_PEER):
        rs = slice(il * N_KEYS, (il + 1) * N_KEYS)
        gate = jnp.zeros((N_KEYS, TM_PEER), BF16)
        for h in range(PK_HEADS):
            picked = rankb_ref[h] < cnt_ref[h, il:il + 1, :].astype(BF16)
            gate = gate + jnp.where(picked, eb_ref[h], 0.0) * ea_ref[h, il:il + 1, :].astype(BF16)
        act_ref[rs, :] = _gelu(pre_ref[rs, :] * unscale).astype(BF16) * gate
        ks = slice(il * k_piece, (il + 1) * k_piece)
        zero = jnp.minimum(gate[:1, :], 0.0)
        x_piece = (hnt_ref[ks, :].astype(BF16) + zero).astype(F8)
        part = jnp.dot(u_ref[:, ks], x_piece, preferred_element_type=F32)
        pre_next = part if pre_next is None else pre_next + part
    pre_next_ref[...] = pre_next
    acc_ref[...] += jnp.dot(vt_ref[...], act_ref[...], preferred_element_type=F32)


def _peer_body(pre0_ref, hnt_ref, u_ref, vt_ref, rankb_ref, eb_ref, cnt_ref, ea_ref, xinv_ref,
               uinv_ref, o_ref, acc_ref, act_ref, pre_a_ref, pre_b_ref):
    i = pl.program_id(0)
    j = pl.program_id(1)
    args = (hnt_ref, u_ref, vt_ref, rankb_ref, eb_ref, cnt_ref, ea_ref, xinv_ref, uinv_ref, acc_ref,
            act_ref)

    @pl.when((i == 0) & (j == 0))
    def _():
        pre_a_ref[...] = pre0_ref[...]

    @pl.when(j == 0)
    def _():
        acc_ref[...] = jnp.zeros(acc_ref.shape, F32)

    @pl.when(lax.rem(j, 2) == 0)
    def _():
        _peer_step(pre_a_ref, pre_b_ref, *args)

    @pl.when(lax.rem(j, 2) == 1)
    def _():
        _peer_step(pre_b_ref, pre_a_ref, *args)

    @pl.when(j == pl.num_programs(1) - 1)
    def _():
        o_ref[...] = acc_ref[...].T


def _peer(hnt, xinv, u_f8, uinv, vt_bf, rankb, eb, cnt, ea):
    d, t = hnt.shape
    ne = u_f8.shape[0]
    tm, te = TM_PEER, TE_PEER
    ni, nj = t // tm, ne // te
    assert nj % 2 == 0, "pre-activation buffers alternate with the expert-tile index"
    pre0 = pl.pallas_call(
        _peer_pre_body,
        grid=(1,),
        in_specs=[pl.BlockSpec((te, d), lambda i: (0, 0)), pl.BlockSpec((d, tm), lambda i: (0, 0))],
        out_specs=pl.BlockSpec((te, tm), lambda i: (0, 0)),
        out_shape=jax.ShapeDtypeStruct((te, tm), F32),
        compiler_params=_cparams(("arbitrary",)),
        name="peer_pre",
    )(u_f8, hnt)

    def next_i(i, j):
        return jnp.minimum(i + (j + 1) // nj, ni - 1)

    key_spec = pl.BlockSpec((PK_HEADS, N_KEYS, tm), lambda i, j: (0, 0, i))
    blk_spec = pl.BlockSpec((PK_HEADS, NI_PEER, tm), lambda i, j: (0, j, i))
    return pl.pallas_call(
        _peer_body,
        grid=(ni, nj),
        in_specs=[pl.BlockSpec((te, tm), lambda i, j: (0, 0)),
                  pl.BlockSpec((d, tm), lambda i, j: (0, next_i(i, j))),
                  pl.BlockSpec((te, d), lambda i, j: ((j + 1) % nj, 0)),
                  pl.BlockSpec((d, te), lambda i, j: (0, j)),
                  key_spec, key_spec, blk_spec, blk_spec,
                  pl.BlockSpec((1, tm), lambda i, j: (0, i)),
                  pl.BlockSpec((1, 1, tm), lambda i, j: (j, 0, 0))],
        out_specs=pl.BlockSpec((tm, d), lambda i, j: (i, 0)),
        out_shape=jax.ShapeDtypeStruct((t, d), F32),
        scratch_shapes=[pltpu.VMEM((d, tm), F32), pltpu.VMEM((te, tm), BF16),
                        pltpu.VMEM((te, tm), F32), pltpu.VMEM((te, tm), F32)],
        compiler_params=_cparams(("arbitrary", "arbitrary")),
        name="peer",
    )(pre0, hnt, u_f8, vt_bf, rankb, eb, cnt, ea, xinv, uinv)


def _ple_body(h_ref, po_ref, p_ref, nple_ref, wg_ref, wp_ref, nfin_ref, o_ref):
    h2 = h_ref[...] + po_ref[...]
    gate = jax.nn.sigmoid(
        jnp.dot(_rms(h2, nple_ref[...]).astype(BF16), wg_ref[...], preferred_element_type=F32))
    ple = jnp.dot(p_ref[...].astype(BF16), wp_ref[...], preferred_element_type=F32)
    o_ref[...] = _rms(h2 + gate * ple, nfin_ref[...])


def _ple(h1, peer_out, p, nple, wg, wp, nfin):
    t, d = h1.shape
    tm = TM_DENSE
    full = lambda shape: pl.BlockSpec(shape, lambda i: (0, 0))
    return pl.pallas_call(
        _ple_body,
        grid=(t // tm,),
        in_specs=[pl.BlockSpec((tm, d), lambda i: (i, 0)),
                  pl.BlockSpec((tm, d), lambda i: (i, 0)),
                  pl.BlockSpec((tm, PLE_DIM), lambda i: (i, 0)),
                  full((1, d)), full((d, d)), full((PLE_DIM, d)), full((1, d))],
        out_specs=pl.BlockSpec((tm, d), lambda i: (i, 0)),
        out_shape=jax.ShapeDtypeStruct((t, d), F32),
        compiler_params=_cparams(("parallel",)),
        name="ple",
    )(h1, peer_out, p, nple, wg, wp, nfin)


def _lane_row(vec, offset):
    return jnp.zeros((1, LANE), F32).at[0, offset:offset + vec.shape[0]].set(vec.astype(F32))


def _head_selector(offset):
    sel = np.zeros((LANE, DN_V), np.float32)
    for h in range(DN_HEADS):
        sel[offset + h, h * DN_DV:(h + 1) * DN_DV] = 1.0
    return jnp.asarray(sel, dtype=BF16)


def kernel(x, p, positions, norm_mix, w_in, conv_w, a_log, dt_bias, dn_norm, ret_gn_w, ret_gn_b,
           w_out, norm_ffn, w_pq, sub_keys, expert_u, expert_v, norm_ple, w_ple_gate, w_ple_proj,
           norm_final):
    b, s, d = x.shape
    t = b * s
    depth = w_in.shape[0]
    assert depth == 1, "the final rms_norm is fused into the single layer's ple kernel"
    half = RET_DK // 2
    inv_freq = ROPE_BASE ** (-jnp.arange(half, dtype=F32) / half)
    freq_row = jnp.tile(inv_freq, LANE // half).reshape(1, LANE)
    pos = positions.reshape(t, 1)
    selg, selb = _head_selector(0), _head_selector(DN_HEADS)

    h = x.reshape(t, d)
    for i in range(depth):
        w = w_in[i].astype(BF16)
        w_ab = jnp.pad(w[:, AB_OFF:AB_OFF + 2 * DN_HEADS], ((0, 0), (0, LANE - 2 * DN_HEADS)))
        proj, proj_ab = _inproj(h, norm_mix[i].reshape(1, d), w, w[:, AB_OFF + 2 * DN_HEADS:], w_ab)

        qa, kd, u, wv, o0, gend = _dn_local(
            proj, proj_ab, conv_w[i].astype(F32).T, _lane_row(a_log[i], 0), _lane_row(dt_bias[i], 0),
            selg, selb, b, s)
        o_dn = _dn_scan(qa, kd, u, wv, o0, gend, b, s)
        o_r = _ret(proj, pos, freq_row, b, s)
        h1 = _outproj(o_dn, proj, o_r, h, dn_norm[i].reshape(1, DN_DV), ret_gn_w[i].reshape(1, RET_V),
                      ret_gn_b[i].reshape(1, RET_V), w_out[i].astype(BF16))

        keys = sub_keys[i].reshape(2 * PK_HEADS, N_KEYS, PK_DQ // 2).astype(BF16)
        hnt, xinv, scores = _pq(h1, norm_ffn[i].reshape(1, d), w_pq[i].astype(BF16), keys)
        rankb, eb, cnt, ea = _router(scores)
        u_f8, uinv = _u_prep(expert_u[i])
        peer_out = _peer(hnt, xinv, u_f8, uinv, _v_prep(expert_v[i]), rankb, eb, cnt, ea)

        h = _ple(h1, peer_out, p[i].reshape(t, PLE_DIM), norm_ple[i].reshape(1, d),
                 w_ple_gate[i].astype(BF16), w_ple_proj[i].astype(BF16), norm_final.reshape(1, d))
    return h.reshape(b, s, d)
```

```python
import functools
import math

import numpy as np
import jax
import jax.numpy as jnp
from jax import lax
from jax.experimental import pallas as pl
from jax.experimental.pallas import tpu as pltpu

F32 = jnp.float32
BF16 = jnp.bfloat16
F8 = jnp.float8_e4m3fn
F8_TARGET = 224.0
F8_TINY = 1e-30
NEG_INF = float("-inf")

EPS = 1e-6
D_MODEL = 2048
DN_HEADS = 8
DN_DK = 128
DN_DV = 128
CONV_W = 4
RET_HEADS = 8
RET_DK = 64
RET_DV = 128
CHUNK = 64
ROPE_BASE = 10000.0
N_KEYS = 128
PK_HEADS = 8
PK_DQ = 256
PK_TOPK = 16
PLE_DIM = 256

DN_QK = DN_HEADS * DN_DK
DN_V = DN_HEADS * DN_DV
CONV_CH = 2 * DN_QK + DN_V
RET_QK = RET_HEADS * RET_DK
RET_V = RET_HEADS * RET_DV
AB_OFF = CONV_CH + DN_V
MAIN_COLS = CONV_CH + DN_V + 2 * RET_QK + 2 * RET_V
LANE = 128
HALO = 8
CONV_HALO = 16

TM_INPROJ = 1024
TN_INPROJ = 1024
TT_SEQ = 256
TM_DENSE = 256
TL_ROUTER = 512
TM_PEER = 512
NI_PEER = 8
TE_PEER = NI_PEER * N_KEYS
VMEM_LIMIT = 56 * 1024 * 1024


def _cparams(sem, flags=None):
    return pltpu.CompilerParams(dimension_semantics=sem, vmem_limit_bytes=VMEM_LIMIT, flags=flags)


def _rms(x, w):
    return x * lax.rsqrt(jnp.mean(x * x, axis=-1, keepdims=True) + EPS) * w


def _silu(x):
    return x * jax.nn.sigmoid(x)


def _softplus(x):
    return jnp.maximum(x, 0.0) + jnp.log1p(jnp.exp(-jnp.abs(x)))


def _gelu(x):
    return 0.5 * x * (1.0 + lax.erf(x * (2.0 ** -0.5)))


def _mm(a, b):
    return jnp.dot(a.astype(BF16), b.astype(BF16), preferred_element_type=F32)


def _mm_nt(a, b):
    return lax.dot_general(a.astype(BF16), b.astype(BF16), (((1,), (1,)), ((), ())),
                           preferred_element_type=F32)


def _mm_tn(a, b):
    return lax.dot_general(a.astype(BF16), b.astype(BF16), (((0,), (0,)), ((), ())),
                           preferred_element_type=F32)


def _rms_matmul(x, norm_w, w_ref, k_chunk=256):
    acc = None
    sumsq = None
    for c in range(x.shape[1] // k_chunk):
        cs = slice(c * k_chunk, (c + 1) * k_chunk)
        xc = x[:, cs]
        part = jnp.sum(xc * xc, axis=-1, keepdims=True)
        sumsq = part if sumsq is None else sumsq + part
        prod = jnp.dot((xc * norm_w[:, cs]).astype(BF16), w_ref[cs, :], preferred_element_type=F32)
        acc = prod if acc is None else acc + prod
    inv_rms = lax.rsqrt(sumsq * (1.0 / x.shape[1]) + EPS)
    return acc * inv_rms, inv_rms


def _split3(x):
    hi = x.astype(BF16)
    rest = x - hi.astype(F32)
    mid = rest.astype(BF16)
    lo = (rest - mid.astype(F32)).astype(BF16)
    return hi, mid, lo


def _select_mm(a, b, dims=(((1,), (0,)), ((), ()))):
    if a.dtype == BF16:
        parts = [lax.dot_general(a, p, dims, preferred_element_type=F32) for p in _split3(b)]
    else:
        parts = [lax.dot_general(p, b, dims, preferred_element_type=F32) for p in _split3(a)]
    return parts[0] + parts[1] + parts[2]


def _inproj_body(x_ref, nw_ref, w_head_ref, w_tail_ref, wab_ref, o_ref, oab_ref, hn_ref):
    j = pl.program_id(1)
    n_head = AB_OFF // TN_INPROJ

    @pl.when(j == 0)
    def _():
        hn_ref[...] = _rms(x_ref[...], nw_ref[...]).astype(BF16)
        oab_ref[...] = jnp.dot(hn_ref[...], wab_ref[...], preferred_element_type=F32)

    @pl.when(j < n_head)
    def _():
        o_ref[...] = jnp.dot(hn_ref[...], w_head_ref[...], preferred_element_type=F32).astype(BF16)

    @pl.when(j >= n_head)
    def _():
        o_ref[...] = jnp.dot(hn_ref[...], w_tail_ref[...], preferred_element_type=F32).astype(BF16)


def _inproj(x, nw, w_all, w_tail, w_ab):
    t, d = x.shape
    n_head = AB_OFF // TN_INPROJ
    n = AB_OFF + w_tail.shape[1]
    return pl.pallas_call(
        _inproj_body,
        grid=(t // TM_INPROJ, n // TN_INPROJ),
        in_specs=[
            pl.BlockSpec((TM_INPROJ, d), lambda i, j: (i, 0)),
            pl.BlockSpec((1, d), lambda i, j: (0, 0)),
            pl.BlockSpec((d, TN_INPROJ), lambda i, j: (0, jnp.minimum(j, n_head - 1))),
            pl.BlockSpec((d, TN_INPROJ), lambda i, j: (0, jnp.maximum(j - n_head, 0))),
            pl.BlockSpec((d, LANE), lambda i, j: (0, 0)),
        ],
        out_specs=[
            pl.BlockSpec((TM_INPROJ, TN_INPROJ), lambda i, j: (i, j)),
            pl.BlockSpec((TM_INPROJ, LANE), lambda i, j: (i, 0)),
        ],
        out_shape=[jax.ShapeDtypeStruct((t, n), BF16), jax.ShapeDtypeStruct((t, LANE), F32)],
        scratch_shapes=[pltpu.VMEM((TM_INPROJ, d), BF16)],
        compiler_params=_cparams(("parallel", "arbitrary")),
        name="inproj",
    )(x, nw, w_all, w_tail, w_ab)


def _dn_local_body(q_ref, k_ref, v_ref, qh_ref, kh_ref, vh_ref, ab_ref, cw_ref, alog_ref, dtb_ref,
                   selg_ref, selb_ref, qa_ref, kd_ref, u_ref, w_ref, o0_ref, gend_ref, buf_ref):
    tt = TT_SEQ
    first_tile = pl.program_id(1) == 0

    def conv_silu(main_ref, halo_ref, c0):
        halo = halo_ref[...].astype(F32)
        buf_ref[0:CONV_HALO, :] = jnp.where(first_tile, jnp.zeros_like(halo), halo)
        buf_ref[CONV_HALO:CONV_HALO + tt, :] = main_ref[...].astype(F32)
        y = None
        for j in range(CONV_W):
            off = CONV_HALO - (CONV_W - 1) + j
            term = buf_ref[off:off + tt, :] * cw_ref[j:j + 1, c0:c0 + DN_QK]
            y = term if y is None else y + term
        return _silu(y)

    q = conv_silu(q_ref, qh_ref, 0)
    k = conv_silu(k_ref, kh_ref, DN_QK)
    v = conv_silu(v_ref, vh_ref, 2 * DN_QK)

    row = lax.broadcasted_iota(jnp.int32, (tt, tt), 0)
    col = lax.broadcasted_iota(jnp.int32, (tt, tt), 1)

    def same_block(bits):
        return (row >> bits) == (col >> bits)

    blk8, blk16, blk32, blk64 = same_block(3), same_block(4), same_block(5), same_block(6)
    causal = blk64 & (col <= row)
    strict = blk64 & (col < row)
    eye = (row == col).astype(F32)

    ab = ab_ref[...]
    g = -jnp.exp(alog_ref[...]) * _softplus(ab + dtb_ref[...])
    beta = jax.nn.sigmoid(ab)
    def ones_where(mask):
        return jnp.where(mask, 1.0, 0.0).astype(BF16)

    gc = _select_mm(ones_where(causal), g)
    gtot = _select_mm(ones_where(blk64), g)
    e_gc = _select_mm(gc, selg_ref[...])
    e_gt = _select_mm(gtot, selg_ref[...])
    e_beta = _select_mm(beta, selb_ref[...])
    r128 = lax.broadcasted_iota(jnp.int32, (LANE, LANE), 0)
    c128 = lax.broadcasted_iota(jnp.int32, (LANE, LANE), 1)
    gc_t = _select_mm(ones_where(r128 == c128), gc, (((1,), (1,)), ((), ())))
    eg = jnp.exp(e_gc)
    kdec = jnp.exp(e_gt - e_gc)

    for c in range(tt // CHUNK):
        gend_ref[0, c:c + 1, :] = jnp.exp(e_gt[c * CHUNK:c * CHUNK + 1, :])
    gend_ref[0, tt // CHUNK:, :] = jnp.zeros((HALO - tt // CHUNK, DN_V), F32)

    heads = range(DN_HEADS)
    hsl = [slice(h * DN_DK, (h + 1) * DN_DK) for h in heads]

    def per_head(fn):
        return [fn(h) for h in heads]

    def l2n(x):
        return x * lax.rsqrt(jnp.sum(x * x, axis=-1, keepdims=True) + EPS)

    qh = per_head(lambda h: l2n(q[:, hsl[h]]) * (DN_DK ** -0.5))
    kh = per_head(lambda h: l2n(k[:, hsl[h]]))
    kb = per_head(lambda h: kh[h] * e_beta[:, hsl[h]])
    dmat = per_head(lambda h: jnp.exp(jnp.where(
        causal, jnp.concatenate([e_gc[:, hsl[h]]] * (tt // LANE), axis=1) - gc_t[h:h + 1, :], NEG_INF)))
    scores = per_head(lambda h: _mm_nt(jnp.concatenate([kb[h], qh[h]], axis=0), kh[h]))
    nmat = per_head(lambda h: jnp.where(strict, scores[h][:tt] * dmat[h], 0.0))
    attn = per_head(lambda h: scores[h][tt:] * dmat[h])

    n0 = per_head(lambda h: jnp.where(blk8, nmat[h], 0.0))
    n2 = per_head(lambda h: _mm(n0[h], n0[h]))
    n4 = per_head(lambda h: _mm(n2[h], n2[h]))
    inv = per_head(lambda h: eye - n0[h])
    inv = per_head(lambda h: inv[h] + _mm(inv[h], n2[h]))
    inv = per_head(lambda h: inv[h] + _mm(inv[h], n4[h]))
    for inner, outer in ((blk8, blk16), (blk16, blk32), (blk32, blk64)):
        level = outer & jnp.logical_not(inner)
        prod = per_head(lambda h: _mm(inv[h], jnp.where(level, nmat[h], 0.0)))
        inv = per_head(lambda h: inv[h] - _mm(prod[h], inv[h]))

    sol = per_head(lambda h: _mm(inv[h], jnp.concatenate(
        [v[:, hsl[h]] * e_beta[:, hsl[h]], kb[h] * eg[:, hsl[h]]], axis=1)))
    asol = per_head(lambda h: _mm(attn[h], sol[h]))
    for h in heads:
        hs = hsl[h]
        u_ref[:, hs] = sol[h][:, :DN_DV]
        w_ref[:, hs] = sol[h][:, DN_DV:].astype(BF16)
        o0_ref[:, hs] = asol[h][:, :DN_DV]
        qa_ref[:, hs] = (qh[h] * eg[:, hs] - asol[h][:, DN_DV:]).astype(BF16)
        kd_ref[:, hs] = (kh[h] * kdec[:, hs]).astype(BF16)


def _dn_local(proj, proj_ab, cw, alog_row, dtb_row, selg, selb, b, s):
    t = b * s
    nt = s // TT_SEQ
    rows_per_halo = TT_SEQ // CONV_HALO

    def main_spec(cb):
        return pl.BlockSpec((TT_SEQ, DN_QK), lambda bi, i: (bi * nt + i, cb))

    def halo_spec(cb):
        return pl.BlockSpec(
            (CONV_HALO, DN_QK), lambda bi, i: (jnp.maximum((bi * nt + i) * rows_per_halo - 1, 0), cb))

    def full(shape):
        return pl.BlockSpec(shape, lambda bi, i: (0,) * len(shape))

    tok_spec = pl.BlockSpec((TT_SEQ, DN_V), lambda bi, i: (bi * nt + i, 0))
    return pl.pallas_call(
        _dn_local_body,
        grid=(b, nt),
        in_specs=[main_spec(0), main_spec(1), main_spec(2), halo_spec(0), halo_spec(1), halo_spec(2),
                  pl.BlockSpec((TT_SEQ, LANE), lambda bi, i: (bi * nt + i, 0)),
                  full((CONV_W, CONV_CH)), full((1, LANE)), full((1, LANE)),
                  full((LANE, DN_V)), full((LANE, DN_V))],
        out_specs=[tok_spec, tok_spec, tok_spec, tok_spec, tok_spec,
                   pl.BlockSpec((1, HALO, DN_V), lambda bi, i: (bi * nt + i, 0, 0))],
        out_shape=[jax.ShapeDtypeStruct((t, DN_V), BF16),
                   jax.ShapeDtypeStruct((t, DN_V), BF16),
                   jax.ShapeDtypeStruct((t, DN_V), F32),
                   jax.ShapeDtypeStruct((t, DN_V), BF16),
                   jax.ShapeDtypeStruct((t, DN_V), F32),
                   jax.ShapeDtypeStruct((t // TT_SEQ, HALO, DN_V), F32)],
        scratch_shapes=[pltpu.VMEM((CONV_HALO + TT_SEQ, DN_QK), F32)],
        compiler_params=_cparams(("parallel", "parallel")),
        name="dn_local",
    )(proj, proj, proj, proj, proj, proj, proj_ab, cw, alog_row, dtb_row, selg, selb)


def _dn_scan_body(qa_ref, kd_ref, u_ref, w_ref, o0_ref, gend_ref, o_ref, s_ref):
    nb = qa_ref.shape[0]

    @pl.when(pl.program_id(0) == 0)
    def _():
        s_ref[...] = jnp.zeros(s_ref.shape, F32)

    chains = [(b, h, slice(h * DN_DK, (h + 1) * DN_DK)) for b in range(nb) for h in range(DN_HEADS)]
    states = [s_ref[n] for n in range(len(chains))]
    for c in range(TT_SEQ // CHUNK):
        rs = slice(c * CHUNK, (c + 1) * CHUNK)
        both = [jnp.dot(jnp.concatenate([w_ref[b, rs, hs], qa_ref[b, rs, hs]], axis=0),
                        states[n].astype(BF16), preferred_element_type=F32)
                for n, (b, h, hs) in enumerate(chains)]
        for n, (b, h, hs) in enumerate(chains):
            o_ref[b, rs, hs] = both[n][CHUNK:] + o0_ref[b, rs, hs]
        states = [states[n] * gend_ref[b, 0, c:c + 1, hs]
                  + _mm_tn(kd_ref[b, rs, hs], u_ref[b, rs, hs] - both[n][:CHUNK])
                  for n, (b, h, hs) in enumerate(chains)]
    for n in range(len(chains)):
        s_ref[n] = states[n]


def _dn_scan(qa, kd, u, w, o0, gend, b, s):
    nt = s // TT_SEQ
    tok = lambda a: a.reshape(b, s, DN_V)
    tok_spec = pl.BlockSpec((b, TT_SEQ, DN_V), lambda i: (0, i, 0))
    out = pl.pallas_call(
        _dn_scan_body,
        grid=(nt,),
        in_specs=[tok_spec, tok_spec, tok_spec, tok_spec, tok_spec,
                  pl.BlockSpec((b, 1, HALO, DN_V), lambda i: (0, i, 0, 0))],
        out_specs=tok_spec,
        out_shape=jax.ShapeDtypeStruct((b, s, DN_V), F32),
        scratch_shapes=[pltpu.VMEM((b * DN_HEADS, DN_DK, DN_DV), F32)],
        compiler_params=_cparams(("arbitrary",)),
        name="dn_scan",
    )(tok(qa), tok(kd), tok(u), tok(w), tok(o0), gend.reshape(b, nt, HALO, DN_V))
    return out.reshape(b * s, DN_V)


def _log_gamma(h):
    return math.log1p(-(2.0 ** (-5.0 - h)))


def _ret_body(qk_ref, v_ref, pos_ref, freq_ref, o_ref, s_ref):
    tt = TT_SEQ

    @pl.when(pl.program_id(1) == 0)
    def _():
        s_ref[...] = jnp.zeros(s_ref.shape, F32)

    ang = pos_ref[...].astype(F32) * freq_ref[...]
    lane = lax.broadcasted_iota(jnp.int32, (tt, LANE), 1)
    first_half = (lane & (RET_DK // 2)) == 0
    cos = jnp.cos(ang)
    sin = jnp.sin(ang)
    ssin = jnp.where(first_half, -sin, sin)

    def rotary(x):
        swapped = jnp.where(first_half, pltpu.roll(x, LANE - RET_DK // 2, axis=1),
                            pltpu.roll(x, RET_DK // 2, axis=1))
        return x * cos + swapped * ssin

    row = lax.broadcasted_iota(jnp.int32, (tt, tt), 0)
    col = lax.broadcasted_iota(jnp.int32, (tt, tt), 1)
    rel = (row - col).astype(F32)
    causal = row >= col
    trow = lax.broadcasted_iota(jnp.int32, (tt, LANE), 0).astype(F32)
    srow = lax.broadcasted_iota(jnp.int32, (LANE, LANE), 0)

    pairs = range(RET_HEADS // 2)
    heads = range(RET_HEADS)
    qp = [rotary(qk_ref[:, pr * LANE:(pr + 1) * LANE].astype(F32)) for pr in pairs]
    kp = [rotary(qk_ref[:, RET_QK + pr * LANE:RET_QK + (pr + 1) * LANE].astype(F32)) * (RET_DK ** -0.5)
          for pr in pairs]
    state = [s_ref[pr] for pr in pairs]
    mine = [(lane < RET_DK) if h % 2 == 0 else (lane >= RET_DK) for h in heads]
    lg = [_log_gamma(h) for h in heads]
    qm = [jnp.where(mine[h], qp[h // 2], 0.0).astype(BF16) for h in heads]
    vh = [v_ref[:, h * RET_DV:(h + 1) * RET_DV].astype(BF16) for h in heads]
    att = [_mm_nt(qm[h], kp[h // 2]) * jnp.exp(jnp.where(causal, rel * lg[h], NEG_INF)) for h in heads]
    inner = [jnp.dot(att[h].astype(BF16), vh[h], preferred_element_type=F32) for h in heads]
    cross = [jnp.dot(qm[h], state[h // 2].astype(BF16), preferred_element_type=F32)
             * jnp.exp((trow + 1.0) * lg[h]) for h in heads]
    for h in heads:
        o_ref[:, h * RET_DV:(h + 1) * RET_DV] = inner[h] + cross[h]
    update = [_mm_tn(jnp.where(mine[h], kp[h // 2], 0.0) * jnp.exp((tt - 1.0 - trow) * lg[h]), vh[h])
              for h in heads]
    for pr in pairs:
        decay = jnp.where(srow < RET_DK, math.exp(tt * lg[2 * pr]), math.exp(tt * lg[2 * pr + 1]))
        s_ref[pr] = state[pr] * decay + update[2 * pr] + update[2 * pr + 1]


def _ret(proj, pos, freq_row, b, s):
    t = b * s
    nt = s // TT_SEQ
    return pl.pallas_call(
        _ret_body,
        grid=(b, nt),
        in_specs=[pl.BlockSpec((TT_SEQ, 2 * RET_QK), lambda bi, i: (bi * nt + i, 4)),
                  pl.BlockSpec((TT_SEQ, RET_V), lambda bi, i: (bi * nt + i, 5)),
                  pl.BlockSpec((TT_SEQ, 1), lambda bi, i: (bi * nt + i, 0)),
                  pl.BlockSpec((1, LANE), lambda bi, i: (0, 0))],
        out_specs=pl.BlockSpec((TT_SEQ, RET_V), lambda bi, i: (bi * nt + i, 0)),
        out_shape=jax.ShapeDtypeStruct((t, RET_V), F32),
        scratch_shapes=[pltpu.VMEM((RET_HEADS // 2, LANE, RET_DV), F32)],
        compiler_params=_cparams(("parallel", "arbitrary")),
        name="ret",
    )(proj, proj, pos, freq_row)


def _outproj_body(odn_ref, z_ref, or_ref, rg_ref, x_ref, dnw_ref, gnw_ref, gnb_ref, w_ref, o_ref):
    def dn_head(h):
        hs = slice(h * DN_DV, (h + 1) * DN_DV)
        return _rms(odn_ref[:, hs], dnw_ref[...]) * _silu(z_ref[:, hs].astype(F32))

    def ret_head(h):
        hs = slice(h * RET_DV, (h + 1) * RET_DV)
        o = or_ref[:, hs]
        cen = o - jnp.mean(o, axis=-1, keepdims=True)
        y = cen * lax.rsqrt(jnp.mean(cen * cen, axis=-1, keepdims=True) + EPS)
        return (y * gnw_ref[:, hs] + gnb_ref[:, hs]) * _silu(rg_ref[:, hs].astype(F32))

    acc = x_ref[...]
    pair = 2 * DN_DV
    heads = [(dn_head, h) for h in range(DN_HEADS)] + [(ret_head, h) for h in range(RET_HEADS)]
    for c in range(len(heads) // 2):
        (f0, h0), (f1, h1) = heads[2 * c], heads[2 * c + 1]
        mix = jnp.concatenate([f0(h0), f1(h1)], axis=1).astype(BF16)
        acc = acc + jnp.dot(mix, w_ref[c * pair:(c + 1) * pair, :], preferred_element_type=F32)
    o_ref[...] = acc


def _outproj(o_dn, proj, o_r, x, dnw, gnw, gnb, w_out):
    t, d = x.shape
    tm = TM_DENSE
    full = lambda shape: pl.BlockSpec(shape, lambda i: (0, 0))
    return pl.pallas_call(
        _outproj_body,
        grid=(t // tm,),
        in_specs=[pl.BlockSpec((tm, DN_V), lambda i: (i, 0)),
                  pl.BlockSpec((tm, DN_V), lambda i: (i, 3)),
                  pl.BlockSpec((tm, RET_V), lambda i: (i, 0)),
                  pl.BlockSpec((tm, RET_V), lambda i: (i, 6)),
                  pl.BlockSpec((tm, d), lambda i: (i, 0)),
                  full((1, DN_DV)), full((1, RET_V)), full((1, RET_V)), full((DN_V + RET_V, d))],
        out_specs=pl.BlockSpec((tm, d), lambda i: (i, 0)),
        out_shape=jax.ShapeDtypeStruct((t, d), F32),
        compiler_params=_cparams(("parallel",)),
        name="outproj",
    )(o_dn, proj, o_r, proj, x, dnw, gnw, gnb, w_out)


def _pq_body(h_ref, nw_ref, w_ref, keys_ref, hnt_ref, inv_ref, sc_ref):
    x = h_ref[...]
    q, inv_rms = _rms_matmul(x, nw_ref[...], w_ref)
    hnt = (x * inv_rms * nw_ref[...]).T
    amax = jnp.maximum(jnp.max(jnp.abs(hnt), axis=0, keepdims=True), F8_TINY)
    hnt_ref[...] = (hnt * (F8_TARGET / amax)).astype(F8)
    inv_ref[...] = amax * (1.0 / F8_TARGET)
    half = PK_DQ // 2
    for hp in range(2 * PK_HEADS):
        sc_ref[hp] = _mm_nt(keys_ref[hp], q[:, hp * half:(hp + 1) * half])


def _pq(h1, nw, w_pq, keys):
    t, d = h1.shape
    tm = TM_DENSE
    nk = 2 * PK_HEADS
    return pl.pallas_call(
        _pq_body,
        grid=(t // tm,),
        in_specs=[pl.BlockSpec((tm, d), lambda i: (i, 0)),
                  pl.BlockSpec((1, d), lambda i: (0, 0)),
                  pl.BlockSpec((d, PK_HEADS * PK_DQ), lambda i: (0, 0)),
                  pl.BlockSpec((nk, N_KEYS, PK_DQ // 2), lambda i: (0, 0, 0))],
        out_specs=[pl.BlockSpec((d, tm), lambda i: (0, i)),
                   pl.BlockSpec((1, tm), lambda i: (0, i)),
                   pl.BlockSpec((nk, N_KEYS, tm), lambda i: (0, 0, i))],
        out_shape=[jax.ShapeDtypeStruct((d, t), F8),
                   jax.ShapeDtypeStruct((1, t), F32),
                   jax.ShapeDtypeStruct((nk, N_KEYS, t), F32)],
        compiler_params=_cparams(("parallel",)),
        name="pq",
    )(h1, nw, w_pq, keys)


def _route(a, bsc, exact):
    tl = a.shape[1]
    k = PK_TOPK
    keyid = lax.broadcasted_iota(jnp.int32, (N_KEYS, tl), 0)
    slot = lax.broadcasted_iota(jnp.int32, (k, tl), 0)

    def extract(s, ids, n_ids):
        m = jnp.max(s, axis=0, keepdims=True)
        hit = s == m
        if exact:
            hit = ids == jnp.min(jnp.where(hit, ids, n_ids), axis=0, keepdims=True)
        return m, hit

    def top_k(s):
        rank = jnp.full((N_KEYS, tl), float(k), F32)
        vals = jnp.zeros((k, tl), F32)
        for r in range(k):
            m, hit = extract(s, keyid, N_KEYS)
            rank = jnp.where(hit, float(r), rank)
            s = jnp.where(hit, NEG_INF, s)
            vals = jnp.where(slot == r, m, vals)
        return vals, rank

    av, rank_a = top_k(a)
    bv, rank_b = top_k(bsc)

    k2 = k // 2
    assert k2 & (k2 - 1) == 0
    n_cand = k + (k - 1) * k2
    cand = jnp.concatenate([av[0:1, :] + bv] + [av[r:r + 1, :] + bv[0:k2, :] for r in range(1, k)],
                           axis=0)
    row = lax.broadcasted_iota(jnp.int32, (n_cand, tl), 0)
    tail = row - k
    cid = jnp.where(row < k, row,
                    (1 + (tail >> (k2.bit_length() - 1))) * k + (tail & (k2 - 1)))
    work = cand
    for _ in range(k):
        _, hit = extract(work, cid, k * k)
        work = jnp.where(hit, NEG_INF, work)
    sel = (work == NEG_INF).astype(F32)
    zsum = jnp.sum(sel * jnp.exp(cand - cand[0:1, :]), axis=0, keepdims=True)

    cnt = jnp.zeros((N_KEYS, tl), F32)
    for r in range(k):
        lo, hi = (0, k) if r == 0 else (k + (r - 1) * k2, k + r * k2)
        cnt_r = jnp.sum(sel[lo:hi, :], axis=0, keepdims=True)
        cnt = jnp.where(rank_a == float(r), cnt_r, cnt)

    def full_count(x):
        return jnp.sum(x, axis=0, keepdims=True) == float(k)

    ok = (full_count((rank_a < float(k)).astype(F32)) & full_count((rank_b < float(k)).astype(F32))
          & full_count(sel))
    eb = jnp.exp(bsc - bv[0:1, :])
    ea = jnp.exp(a - av[0:1, :]) / zsum
    return rank_b, eb, cnt, ea, ok


def _router_body(sc_ref, u_ref, v_ref, rankb_ref, eb_ref, cnt_ref, ea_ref, u8_ref, uinv_ref, vt_ref):
    u = u_ref[...]
    amax = jnp.max(jnp.max(jnp.abs(u), axis=0, keepdims=True), axis=1, keepdims=True)
    amax = jnp.maximum(amax, F8_TINY)
    u8_ref[...] = (u * (F8_TARGET / amax)).astype(F8)
    uinv_ref[0] = jnp.broadcast_to(amax * (1.0 / F8_TARGET), uinv_ref.shape[1:])
    vt_ref[...] = v_ref[...].T.astype(BF16)

    def run(exact):
        rank_b, eb, cnt, ea, ok = _route(sc_ref[0], sc_ref[1], exact)
        rankb_ref[0] = rank_b.astype(BF16)
        eb_ref[0] = eb.astype(BF16)
        cnt_ref[0] = cnt
        ea_ref[0] = ea
        return ok

    ok = run(exact=False)
    n_bad = jnp.sum(jnp.where(ok, 0.0, 1.0), axis=1, keepdims=True)

    @pl.when(n_bad[0, 0] > 0.0)
    def _():
        run(exact=True)


def _router(scores, expert_u, expert_v):
    nk, n, t = scores.shape
    ne, d = expert_u.shape
    tl = TL_ROUTER
    nt = t // tl
    assert PK_HEADS * nt * N_KEYS == ne, "one 128-expert block is prepared per routing step"
    spec = pl.BlockSpec((1, n, tl), lambda h, j: (h, 0, j))
    shape = lambda dt: jax.ShapeDtypeStruct((PK_HEADS, n, t), dt)
    blk = lambda h, j: h * nt + j
    return pl.pallas_call(
        _router_body,
        grid=(PK_HEADS, nt),
        in_specs=[pl.BlockSpec((2, n, tl), lambda h, j: (h, 0, j)),
                  pl.BlockSpec((N_KEYS, d), lambda h, j: (blk(h, j), 0)),
                  pl.BlockSpec((N_KEYS, d), lambda h, j: (blk(h, j), 0))],
        out_specs=[spec, spec, spec, spec,
                   pl.BlockSpec((N_KEYS, d), lambda h, j: (blk(h, j), 0)),
                   pl.BlockSpec((1, 1, TM_PEER), lambda h, j: (blk(h, j), 0, 0)),
                   pl.BlockSpec((d, N_KEYS), lambda h, j: (0, blk(h, j)))],
        out_shape=[shape(BF16), shape(BF16), shape(F32), shape(F32),
                   jax.ShapeDtypeStruct((ne, d), F8),
                   jax.ShapeDtypeStruct((ne // N_KEYS, 1, TM_PEER), F32),
                   jax.ShapeDtypeStruct((d, ne), BF16)],
        compiler_params=_cparams(("parallel", "parallel")),
        name="router",
    )(scores, expert_u, expert_v)


def _peer_pre_body(u_ref, hnt_ref, o_ref):
    o_ref[...] = jnp.dot(u_ref[...], hnt_ref[...], preferred_element_type=F32)


def _peer_step(pre_ref, pre_next_ref, hnt_ref, u_ref, vt_ref, rankb_ref, eb_ref, cnt_ref, ea_ref,
               xinv_ref, uinv_ref, acc_ref, act_ref):
    k_piece = D_MODEL // NI_PEER
    pre_next = None
    for il in range(NI_PEER):
        rs = slice(il * N_KEYS, (il + 1) * N_KEYS)
        gate = jnp.zeros((N_KEYS, TM_PEER), BF16)
        for h in range(PK_HEADS):
            picked = rankb_ref[h] < cnt_ref[h, il:il + 1, :].astype(BF16)
            gate = gate + jnp.where(picked, eb_ref[h], 0.0) * ea_ref[h, il:il + 1, :].astype(BF16)
        unscale = xinv_ref[...] * uinv_ref[il]
        act_ref[rs, :] = _gelu(pre_ref[rs, :] * unscale).astype(BF16) * gate
        ks = slice(il * k_piece, (il + 1) * k_piece)
        zero = jnp.minimum(gate[:1, :], 0.0)
        x_piece = (hnt_ref[ks, :].astype(BF16) + zero).astype(F8)
        part = jnp.dot(u_ref[:, ks], x_piece, preferred_element_type=F32)
        pre_next = part if pre_next is None else pre_next + part
    pre_next_ref[...] = pre_next
    acc_ref[...] += jnp.dot(vt_ref[...], act_ref[...], preferred_element_type=F32)


def _peer_body(pre0_ref, hnt_ref, u_ref, vt_ref, rankb_ref, eb_ref, cnt_ref, ea_ref, xinv_ref,
               uinv_ref, o_ref, acc_ref, act_ref, pre_a_ref, pre_b_ref):
    i = pl.program_id(0)
    j = pl.program_id(1)
    args = (hnt_ref, u_ref, vt_ref, rankb_ref, eb_ref, cnt_ref, ea_ref, xinv_ref, uinv_ref, acc_ref,
            act_ref)

    @pl.when((i == 0) & (j == 0))
    def _():
        pre_a_ref[...] = pre0_ref[...]

    @pl.when(j == 0)
    def _():
        acc_ref[...] = jnp.zeros(acc_ref.shape, F32)

    @pl.when(lax.rem(j, 2) == 0)
    def _():
        _peer_step(pre_a_ref, pre_b_ref, *args)

    @pl.when(lax.rem(j, 2) == 1)
    def _():
        _peer_step(pre_b_ref, pre_a_ref, *args)

    @pl.when(j == pl.num_programs(1) - 1)
    def _():
        o_ref[...] = acc_ref[...].T


def _peer(hnt, xinv, u_f8, uinv, vt_bf, rankb, eb, cnt, ea):
    d, t = hnt.shape
    ne = u_f8.shape[0]
    tm, te = TM_PEER, TE_PEER
    ni, nj = t // tm, ne // te
    assert nj % 2 == 0, "pre-activation buffers alternate with the expert-tile index"
    pre0 = pl.pallas_call(
        _peer_pre_body,
        grid=(1,),
        in_specs=[pl.BlockSpec((te, d), lambda i: (0, 0)), pl.BlockSpec((d, tm), lambda i: (0, 0))],
        out_specs=pl.BlockSpec((te, tm), lambda i: (0, 0)),
        out_shape=jax.ShapeDtypeStruct((te, tm), F32),
        compiler_params=_cparams(("arbitrary",)),
        name="peer_pre",
    )(u_f8, hnt)

    def next_i(i, j):
        return jnp.minimum(i + (j + 1) // nj, ni - 1)

    key_spec = pl.BlockSpec((PK_HEADS, N_KEYS, tm), lambda i, j: (0, 0, i))
    blk_spec = pl.BlockSpec((PK_HEADS, NI_PEER, tm), lambda i, j: (0, j, i))
    return pl.pallas_call(
        _peer_body,
        grid=(ni, nj),
        in_specs=[pl.BlockSpec((te, tm), lambda i, j: (0, 0)),
                  pl.BlockSpec((d, tm), lambda i, j: (0, next_i(i, j))),
                  pl.BlockSpec((te, d), lambda i, j: ((j + 1) % nj, 0)),
                  pl.BlockSpec((d, te), lambda i, j: (0, j)),
                  key_spec, key_spec, blk_spec, blk_spec,
                  pl.BlockSpec((1, tm), lambda i, j: (0, i)),
                  pl.BlockSpec((NI_PEER, 1, tm), lambda i, j: (j, 0, 0))],
        out_specs=pl.BlockSpec((tm, d), lambda i, j: (i, 0)),
        out_shape=jax.ShapeDtypeStruct((t, d), F32),
        scratch_shapes=[pltpu.VMEM((d, tm), F32), pltpu.VMEM((te, tm), BF16),
                        pltpu.VMEM((te, tm), F32), pltpu.VMEM((te, tm), F32)],
        compiler_params=_cparams(("arbitrary", "arbitrary")),
        name="peer",
    )(pre0, hnt, u_f8, vt_bf, rankb, eb, cnt, ea, xinv, uinv)


def _ple_body(h_ref, po_ref, p_ref, nple_ref, wg_ref, wp_ref, nfin_ref, o_ref):
    h2 = h_ref[...] + po_ref[...]
    gate = jax.nn.sigmoid(
        jnp.dot(_rms(h2, nple_ref[...]).astype(BF16), wg_ref[...], preferred_element_type=F32))
    ple = jnp.dot(p_ref[...].astype(BF16), wp_ref[...], preferred_element_type=F32)
    o_ref[...] = _rms(h2 + gate * ple, nfin_ref[...])


def _ple(h1, peer_out, p, nple, wg, wp, nfin):
    t, d = h1.shape
    tm = TM_DENSE
    full = lambda shape: pl.BlockSpec(shape, lambda i: (0, 0))
    return pl.pallas_call(
        _ple_body,
        grid=(t // tm,),
        in_specs=[pl.BlockSpec((tm, d), lambda i: (i, 0)),
                  pl.BlockSpec((tm, d), lambda i: (i, 0)),
                  pl.BlockSpec((tm, PLE_DIM), lambda i: (i, 0)),
                  full((1, d)), full((d, d)), full((PLE_DIM, d)), full((1, d))],
        out_specs=pl.BlockSpec((tm, d), lambda i: (i, 0)),
        out_shape=jax.ShapeDtypeStruct((t, d), F32),
        compiler_params=_cparams(("parallel",)),
        name="ple",
    )(h1, peer_out, p, nple, wg, wp, nfin)


def _lane_row(vec, offset):
    return jnp.zeros((1, LANE), F32).at[0, offset:offset + vec.shape[0]].set(vec.astype(F32))


def _head_selector(offset):
    sel = np.zeros((LANE, DN_V), np.float32)
    for h in range(DN_HEADS):
        sel[offset + h, h * DN_DV:(h + 1) * DN_DV] = 1.0
    return jnp.asarray(sel, dtype=BF16)


def kernel(x, p, positions, norm_mix, w_in, conv_w, a_log, dt_bias, dn_norm, ret_gn_w, ret_gn_b,
           w_out, norm_ffn, w_pq, sub_keys, expert_u, expert_v, norm_ple, w_ple_gate, w_ple_proj,
           norm_final):
    b, s, d = x.shape
    t = b * s
    depth = w_in.shape[0]
    assert depth == 1, "the final rms_norm is fused into the single layer's ple kernel"
    half = RET_DK // 2
    inv_freq = ROPE_BASE ** (-jnp.arange(half, dtype=F32) / half)
    freq_row = jnp.tile(inv_freq, LANE // half).reshape(1, LANE)
    pos = positions.reshape(t, 1)
    selg, selb = _head_selector(0), _head_selector(DN_HEADS)

    h = x.reshape(t, d)
    for i in range(depth):
        w = w_in[i].astype(BF16)
        w_ab = jnp.pad(w[:, AB_OFF:AB_OFF + 2 * DN_HEADS], ((0, 0), (0, LANE - 2 * DN_HEADS)))
        proj, proj_ab = _inproj(h, norm_mix[i].reshape(1, d), w, w[:, AB_OFF + 2 * DN_HEADS:], w_ab)

        qa, kd, u, wv, o0, gend = _dn_local(
            proj, proj_ab, conv_w[i].astype(F32).T, _lane_row(a_log[i], 0), _lane_row(dt_bias[i], 0),
            selg, selb, b, s)
        o_dn = _dn_scan(qa, kd, u, wv, o0, gend, b, s)
        o_r = _ret(proj, pos, freq_row, b, s)
        h1 = _outproj(o_dn, proj, o_r, h, dn_norm[i].reshape(1, DN_DV), ret_gn_w[i].reshape(1, RET_V),
                      ret_gn_b[i].reshape(1, RET_V), w_out[i].astype(BF16))

        keys = sub_keys[i].reshape(2 * PK_HEADS, N_KEYS, PK_DQ // 2).astype(BF16)
        hnt, xinv, scores = _pq(h1, norm_ffn[i].reshape(1, d), w_pq[i].astype(BF16), keys)
        rankb, eb, cnt, ea, u_f8, uinv, vt_bf = _router(scores, expert_u[i], expert_v[i])
        peer_out = _peer(hnt, xinv, u_f8, uinv, vt_bf, rankb, eb, cnt, ea)

        h = _ple(h1, peer_out, p[i].reshape(t, PLE_DIM), norm_ple[i].reshape(1, d),
                 w_ple_gate[i].astype(BF16), w_ple_proj[i].astype(BF16), norm_final.reshape(1, d))
    return h.reshape(b, s, d)
```

```python
import functools
import math

import numpy as np
import jax
import jax.numpy as jnp
from jax import lax
from jax.experimental import pallas as pl
from jax.experimental.pallas import tpu as pltpu

F32 = jnp.float32
BF16 = jnp.bfloat16
F8 = jnp.float8_e4m3fn
F8_TARGET = 224.0
F8_TINY = 1e-30
NEG_INF = float("-inf")

EPS = 1e-6
D_MODEL = 2048
DN_HEADS = 8
DN_DK = 128
DN_DV = 128
CONV_W = 4
RET_HEADS = 8
RET_DK = 64
RET_DV = 128
CHUNK = 64
ROPE_BASE = 10000.0
N_KEYS = 128
PK_HEADS = 8
PK_DQ = 256
PK_TOPK = 16
PLE_DIM = 256

DN_QK = DN_HEADS * DN_DK
DN_V = DN_HEADS * DN_DV
CONV_CH = 2 * DN_QK + DN_V
RET_QK = RET_HEADS * RET_DK
RET_V = RET_HEADS * RET_DV
AB_OFF = CONV_CH + DN_V
MAIN_COLS = CONV_CH + DN_V + 2 * RET_QK + 2 * RET_V
LANE = 128
HALO = 8
CONV_HALO = 16

TM_INPROJ = 1024
TN_INPROJ = 1024
TT_SEQ = 256
TM_DENSE = 256
TL_ROUTER = 512
TM_PEER = 512
NI_PEER = 8
TE_PEER = NI_PEER * N_KEYS
VMEM_LIMIT = 56 * 1024 * 1024


def _cparams(sem, flags=None):
    return pltpu.CompilerParams(dimension_semantics=sem, vmem_limit_bytes=VMEM_LIMIT, flags=flags)


def _rms(x, w):
    return x * lax.rsqrt(jnp.mean(x * x, axis=-1, keepdims=True) + EPS) * w


def _silu(x):
    return x * jax.nn.sigmoid(x)


def _softplus(x):
    return jnp.maximum(x, 0.0) + jnp.log1p(jnp.exp(-jnp.abs(x)))


def _gelu(x):
    return 0.5 * x * (1.0 + lax.erf(x * (2.0 ** -0.5)))


def _mm(a, b):
    return jnp.dot(a.astype(BF16), b.astype(BF16), preferred_element_type=F32)


def _mm_nt(a, b):
    return lax.dot_general(a.astype(BF16), b.astype(BF16), (((1,), (1,)), ((), ())),
                           preferred_element_type=F32)


def _mm_tn(a, b):
    return lax.dot_general(a.astype(BF16), b.astype(BF16), (((0,), (0,)), ((), ())),
                           preferred_element_type=F32)


def _rms_matmul(x, norm_w, w_ref, k_chunk=256):
    acc = None
    sumsq = None
    for c in range(x.shape[1] // k_chunk):
        cs = slice(c * k_chunk, (c + 1) * k_chunk)
        xc = x[:, cs]
        part = jnp.sum(xc * xc, axis=-1, keepdims=True)
        sumsq = part if sumsq is None else sumsq + part
        prod = jnp.dot((xc * norm_w[:, cs]).astype(BF16), w_ref[cs, :], preferred_element_type=F32)
        acc = prod if acc is None else acc + prod
    inv_rms = lax.rsqrt(sumsq * (1.0 / x.shape[1]) + EPS)
    return acc * inv_rms, inv_rms


def _split3(x):
    hi = x.astype(BF16)
    rest = x - hi.astype(F32)
    mid = rest.astype(BF16)
    lo = (rest - mid.astype(F32)).astype(BF16)
    return hi, mid, lo


def _select_mm(a, b, dims=(((1,), (0,)), ((), ()))):
    if a.dtype == BF16:
        parts = [lax.dot_general(a, p, dims, preferred_element_type=F32) for p in _split3(b)]
    else:
        parts = [lax.dot_general(p, b, dims, preferred_element_type=F32) for p in _split3(a)]
    return parts[0] + parts[1] + parts[2]


def _inproj_body(x_ref, nw_ref, w_head_ref, w_tail_ref, wab_ref, o_ref, oab_ref, hn_ref):
    j = pl.program_id(1)
    n_head = AB_OFF // TN_INPROJ

    @pl.when(j == 0)
    def _():
        hn_ref[...] = _rms(x_ref[...], nw_ref[...]).astype(BF16)
        oab_ref[...] = jnp.dot(hn_ref[...], wab_ref[...], preferred_element_type=F32)

    @pl.when(j < n_head)
    def _():
        o_ref[...] = jnp.dot(hn_ref[...], w_head_ref[...], preferred_element_type=F32).astype(BF16)

    @pl.when(j >= n_head)
    def _():
        o_ref[...] = jnp.dot(hn_ref[...], w_tail_ref[...], preferred_element_type=F32).astype(BF16)


def _inproj(x, nw, w_all, w_tail, w_ab):
    t, d = x.shape
    n_head = AB_OFF // TN_INPROJ
    n = AB_OFF + w_tail.shape[1]
    return pl.pallas_call(
        _inproj_body,
        grid=(t // TM_INPROJ, n // TN_INPROJ),
        in_specs=[
            pl.BlockSpec((TM_INPROJ, d), lambda i, j: (i, 0)),
            pl.BlockSpec((1, d), lambda i, j: (0, 0)),
            pl.BlockSpec((d, TN_INPROJ), lambda i, j: (0, jnp.minimum(j, n_head - 1))),
            pl.BlockSpec((d, TN_INPROJ), lambda i, j: (0, jnp.maximum(j - n_head, 0))),
            pl.BlockSpec((d, LANE), lambda i, j: (0, 0)),
        ],
        out_specs=[
            pl.BlockSpec((TM_INPROJ, TN_INPROJ), lambda i, j: (i, j)),
            pl.BlockSpec((TM_INPROJ, LANE), lambda i, j: (i, 0)),
        ],
        out_shape=[jax.ShapeDtypeStruct((t, n), BF16), jax.ShapeDtypeStruct((t, LANE), F32)],
        scratch_shapes=[pltpu.VMEM((TM_INPROJ, d), BF16)],
        compiler_params=_cparams(("parallel", "arbitrary")),
        name="inproj",
    )(x, nw, w_all, w_tail, w_ab)


def _dn_local_body(q_ref, k_ref, v_ref, qh_ref, kh_ref, vh_ref, ab_ref, cw_ref, alog_ref, dtb_ref,
                   selg_ref, selb_ref, wa_ref, wb_ref, wc_ref, qa_ref, kd_ref, u_ref, w_ref, o0_ref,
                   gend_ref, wa16_ref, wb16_ref, wc16_ref, buf_ref):
    tt = TT_SEQ
    first_tile = pl.program_id(1) == 0
    for src, dst in ((wa_ref, wa16_ref), (wb_ref, wb16_ref), (wc_ref, wc16_ref)):
        dst[...] = src[...].astype(BF16)

    def conv_silu(main_ref, halo_ref, c0):
        halo = halo_ref[...].astype(F32)
        buf_ref[0:CONV_HALO, :] = jnp.where(first_tile, jnp.zeros_like(halo), halo)
        buf_ref[CONV_HALO:CONV_HALO + tt, :] = main_ref[...].astype(F32)
        y = None
        for j in range(CONV_W):
            off = CONV_HALO - (CONV_W - 1) + j
            term = buf_ref[off:off + tt, :] * cw_ref[j:j + 1, c0:c0 + DN_QK]
            y = term if y is None else y + term
        return _silu(y)

    q = conv_silu(q_ref, qh_ref, 0)
    k = conv_silu(k_ref, kh_ref, DN_QK)
    v = conv_silu(v_ref, vh_ref, 2 * DN_QK)

    row = lax.broadcasted_iota(jnp.int32, (tt, tt), 0)
    col = lax.broadcasted_iota(jnp.int32, (tt, tt), 1)

    def same_block(bits):
        return (row >> bits) == (col >> bits)

    blk8, blk16, blk32, blk64 = same_block(3), same_block(4), same_block(5), same_block(6)
    causal = blk64 & (col <= row)
    strict = blk64 & (col < row)
    eye = (row == col).astype(F32)

    ab = ab_ref[...]
    g = -jnp.exp(alog_ref[...]) * _softplus(ab + dtb_ref[...])
    beta = jax.nn.sigmoid(ab)
    def ones_where(mask):
        return jnp.where(mask, 1.0, 0.0).astype(BF16)

    gc = _select_mm(ones_where(causal), g)
    gtot = _select_mm(ones_where(blk64), g)
    e_gc = _select_mm(gc, selg_ref[...])
    e_gt = _select_mm(gtot, selg_ref[...])
    e_beta = _select_mm(beta, selb_ref[...])
    r128 = lax.broadcasted_iota(jnp.int32, (LANE, LANE), 0)
    c128 = lax.broadcasted_iota(jnp.int32, (LANE, LANE), 1)
    gc_t = _select_mm(ones_where(r128 == c128), gc, (((1,), (1,)), ((), ())))
    eg = jnp.exp(e_gc)
    kdec = jnp.exp(e_gt - e_gc)

    for c in range(tt // CHUNK):
        gend_ref[0, c:c + 1, :] = jnp.exp(e_gt[c * CHUNK:c * CHUNK + 1, :])
    gend_ref[0, tt // CHUNK:, :] = jnp.zeros((HALO - tt // CHUNK, DN_V), F32)

    heads = range(DN_HEADS)
    hsl = [slice(h * DN_DK, (h + 1) * DN_DK) for h in heads]

    def per_head(fn):
        return [fn(h) for h in heads]

    def l2n(x):
        return x * lax.rsqrt(jnp.sum(x * x, axis=-1, keepdims=True) + EPS)

    qh = per_head(lambda h: l2n(q[:, hsl[h]]) * (DN_DK ** -0.5))
    kh = per_head(lambda h: l2n(k[:, hsl[h]]))
    kb = per_head(lambda h: kh[h] * e_beta[:, hsl[h]])
    dmat = per_head(lambda h: jnp.exp(jnp.where(
        causal, jnp.concatenate([e_gc[:, hsl[h]]] * (tt // LANE), axis=1) - gc_t[h:h + 1, :], NEG_INF)))
    scores = per_head(lambda h: _mm_nt(jnp.concatenate([kb[h], qh[h]], axis=0), kh[h]))
    nmat = per_head(lambda h: jnp.where(strict, scores[h][:tt] * dmat[h], 0.0))
    attn = per_head(lambda h: scores[h][tt:] * dmat[h])

    n0 = per_head(lambda h: jnp.where(blk8, nmat[h], 0.0))
    n2 = per_head(lambda h: _mm(n0[h], n0[h]))
    n4 = per_head(lambda h: _mm(n2[h], n2[h]))
    inv = per_head(lambda h: eye - n0[h])
    inv = per_head(lambda h: inv[h] + _mm(inv[h], n2[h]))
    inv = per_head(lambda h: inv[h] + _mm(inv[h], n4[h]))
    for inner, outer in ((blk8, blk16), (blk16, blk32), (blk32, blk64)):
        level = outer & jnp.logical_not(inner)
        prod = per_head(lambda h: _mm(inv[h], jnp.where(level, nmat[h], 0.0)))
        inv = per_head(lambda h: inv[h] - _mm(prod[h], inv[h]))

    sol = per_head(lambda h: _mm(inv[h], jnp.concatenate(
        [v[:, hsl[h]] * e_beta[:, hsl[h]], kb[h] * eg[:, hsl[h]]], axis=1)))
    asol = per_head(lambda h: _mm(attn[h], sol[h]))
    for h in heads:
        hs = hsl[h]
        u_ref[:, hs] = sol[h][:, :DN_DV].astype(BF16)
        w_ref[:, hs] = sol[h][:, DN_DV:].astype(BF16)
        o0_ref[:, hs] = asol[h][:, :DN_DV].astype(BF16)
        qa_ref[:, hs] = (qh[h] * eg[:, hs] - asol[h][:, DN_DV:]).astype(BF16)
        kd_ref[:, hs] = (kh[h] * kdec[:, hs]).astype(BF16)


def _dn_local(proj, proj_ab, cw, alog_row, dtb_row, selg, selb, later_weights, b, s):
    t = b * s
    nt = s // TT_SEQ
    rows_per_halo = TT_SEQ // CONV_HALO
    w_rows, w_cols = later_weights[0].shape
    assert all(w.shape == (w_rows, w_cols) for w in later_weights) and w_rows % (b * nt) == 0
    slab = pl.BlockSpec((w_rows // (b * nt), w_cols), lambda bi, i: (bi * nt + i, 0))

    def main_spec(cb):
        return pl.BlockSpec((TT_SEQ, DN_QK), lambda bi, i: (bi * nt + i, cb))

    def halo_spec(cb):
        return pl.BlockSpec(
            (CONV_HALO, DN_QK), lambda bi, i: (jnp.maximum((bi * nt + i) * rows_per_halo - 1, 0), cb))

    def full(shape):
        return pl.BlockSpec(shape, lambda bi, i: (0,) * len(shape))

    tok_spec = pl.BlockSpec((TT_SEQ, DN_V), lambda bi, i: (bi * nt + i, 0))
    return pl.pallas_call(
        _dn_local_body,
        grid=(b, nt),
        in_specs=[main_spec(0), main_spec(1), main_spec(2), halo_spec(0), halo_spec(1), halo_spec(2),
                  pl.BlockSpec((TT_SEQ, LANE), lambda bi, i: (bi * nt + i, 0)),
                  full((CONV_W, CONV_CH)), full((1, LANE)), full((1, LANE)),
                  full((LANE, DN_V)), full((LANE, DN_V)), slab, slab, slab],
        out_specs=[tok_spec, tok_spec, tok_spec, tok_spec, tok_spec,
                   pl.BlockSpec((1, HALO, DN_V), lambda bi, i: (bi * nt + i, 0, 0)), slab, slab, slab],
        out_shape=[jax.ShapeDtypeStruct((t, DN_V), BF16),
                   jax.ShapeDtypeStruct((t, DN_V), BF16),
                   jax.ShapeDtypeStruct((t, DN_V), BF16),
                   jax.ShapeDtypeStruct((t, DN_V), BF16),
                   jax.ShapeDtypeStruct((t, DN_V), BF16),
                   jax.ShapeDtypeStruct((t // TT_SEQ, HALO, DN_V), F32)]
                  + [jax.ShapeDtypeStruct((w_rows, w_cols), BF16)] * 3,
        scratch_shapes=[pltpu.VMEM((CONV_HALO + TT_SEQ, DN_QK), F32)],
        compiler_params=_cparams(("parallel", "parallel")),
        name="dn_local",
    )(proj, proj, proj, proj, proj, proj, proj_ab, cw, alog_row, dtb_row, selg, selb, *later_weights)


def _dn_scan_body(qa_ref, kd_ref, u_ref, w_ref, o0_ref, gend_ref, o_ref, s_ref):
    nb = qa_ref.shape[0]

    @pl.when(pl.program_id(0) == 0)
    def _():
        s_ref[...] = jnp.zeros(s_ref.shape, F32)

    chains = [(b, h, slice(h * DN_DK, (h + 1) * DN_DK)) for b in range(nb) for h in range(DN_HEADS)]
    states = [s_ref[n] for n in range(len(chains))]
    for c in range(TT_SEQ // CHUNK):
        rs = slice(c * CHUNK, (c + 1) * CHUNK)
        both = [jnp.dot(jnp.concatenate([w_ref[b, rs, hs], qa_ref[b, rs, hs]], axis=0),
                        states[n].astype(BF16), preferred_element_type=F32)
                for n, (b, h, hs) in enumerate(chains)]
        for n, (b, h, hs) in enumerate(chains):
            o_ref[b, rs, hs] = (both[n][CHUNK:] + o0_ref[b, rs, hs]).astype(o_ref.dtype)
        states = [states[n] * gend_ref[b, 0, c:c + 1, hs]
                  + _mm_tn(kd_ref[b, rs, hs], u_ref[b, rs, hs] - both[n][:CHUNK])
                  for n, (b, h, hs) in enumerate(chains)]
    for n in range(len(chains)):
        s_ref[n] = states[n]


def _dn_scan(qa, kd, u, w, o0, gend, b, s):
    nt = s // TT_SEQ
    tok = lambda a: a.reshape(b, s, DN_V)
    tok_spec = pl.BlockSpec((b, TT_SEQ, DN_V), lambda i: (0, i, 0))
    out = pl.pallas_call(
        _dn_scan_body,
        grid=(nt,),
        in_specs=[tok_spec, tok_spec, tok_spec, tok_spec, tok_spec,
                  pl.BlockSpec((b, 1, HALO, DN_V), lambda i: (0, i, 0, 0))],
        out_specs=tok_spec,
        out_shape=jax.ShapeDtypeStruct((b, s, DN_V), BF16),
        scratch_shapes=[pltpu.VMEM((b * DN_HEADS, DN_DK, DN_DV), F32)],
        compiler_params=_cparams(("arbitrary",)),
        name="dn_scan",
    )(tok(qa), tok(kd), tok(u), tok(w), tok(o0), gend.reshape(b, nt, HALO, DN_V))
    return out.reshape(b * s, DN_V)


def _log_gamma(h):
    return math.log1p(-(2.0 ** (-5.0 - h)))


def _ret_body(qk_ref, v_ref, pos_ref, freq_ref, o_ref, s_ref):
    tt = TT_SEQ

    @pl.when(pl.program_id(1) == 0)
    def _():
        s_ref[...] = jnp.zeros(s_ref.shape, F32)

    ang = pos_ref[...].astype(F32) * freq_ref[...]
    lane = lax.broadcasted_iota(jnp.int32, (tt, LANE), 1)
    first_half = (lane & (RET_DK // 2)) == 0
    cos = jnp.cos(ang)
    sin = jnp.sin(ang)
    ssin = jnp.where(first_half, -sin, sin)

    def rotary(x):
        swapped = jnp.where(first_half, pltpu.roll(x, LANE - RET_DK // 2, axis=1),
                            pltpu.roll(x, RET_DK // 2, axis=1))
        return x * cos + swapped * ssin

    row = lax.broadcasted_iota(jnp.int32, (tt, tt), 0)
    col = lax.broadcasted_iota(jnp.int32, (tt, tt), 1)
    rel = (row - col).astype(F32)
    causal = row >= col
    trow = lax.broadcasted_iota(jnp.int32, (tt, LANE), 0).astype(F32)
    srow = lax.broadcasted_iota(jnp.int32, (LANE, LANE), 0)

    pairs = range(RET_HEADS // 2)
    heads = range(RET_HEADS)
    qp = [rotary(qk_ref[:, pr * LANE:(pr + 1) * LANE].astype(F32)) for pr in pairs]
    kp = [rotary(qk_ref[:, RET_QK + pr * LANE:RET_QK + (pr + 1) * LANE].astype(F32)) * (RET_DK ** -0.5)
          for pr in pairs]
    state = [s_ref[pr] for pr in pairs]
    mine = [(lane < RET_DK) if h % 2 == 0 else (lane >= RET_DK) for h in heads]
    lg = [_log_gamma(h) for h in heads]
    qm = [jnp.where(mine[h], qp[h // 2], 0.0).astype(BF16) for h in heads]
    vh = [v_ref[:, h * RET_DV:(h + 1) * RET_DV].astype(BF16) for h in heads]
    att = [_mm_nt(qm[h], kp[h // 2]) * jnp.exp(jnp.where(causal, rel * lg[h], NEG_INF)) for h in heads]
    inner = [jnp.dot(att[h].astype(BF16), vh[h], preferred_element_type=F32) for h in heads]
    cross = [jnp.dot(qm[h], state[h // 2].astype(BF16), preferred_element_type=F32)
             * jnp.exp((trow + 1.0) * lg[h]) for h in heads]
    for h in heads:
        o_ref[:, h * RET_DV:(h + 1) * RET_DV] = inner[h] + cross[h]
    update = [_mm_tn(jnp.where(mine[h], kp[h // 2], 0.0) * jnp.exp((tt - 1.0 - trow) * lg[h]), vh[h])
              for h in heads]
    for pr in pairs:
        decay = jnp.where(srow < RET_DK, math.exp(tt * lg[2 * pr]), math.exp(tt * lg[2 * pr + 1]))
        s_ref[pr] = state[pr] * decay + update[2 * pr] + update[2 * pr + 1]


def _ret(proj, pos, freq_row, b, s):
    t = b * s
    nt = s // TT_SEQ
    return pl.pallas_call(
        _ret_body,
        grid=(b, nt),
        in_specs=[pl.BlockSpec((TT_SEQ, 2 * RET_QK), lambda bi, i: (bi * nt + i, 4)),
                  pl.BlockSpec((TT_SEQ, RET_V), lambda bi, i: (bi * nt + i, 5)),
                  pl.BlockSpec((TT_SEQ, 1), lambda bi, i: (bi * nt + i, 0)),
                  pl.BlockSpec((1, LANE), lambda bi, i: (0, 0))],
        out_specs=pl.BlockSpec((TT_SEQ, RET_V), lambda bi, i: (bi * nt + i, 0)),
        out_shape=jax.ShapeDtypeStruct((t, RET_V), F32),
        scratch_shapes=[pltpu.VMEM((RET_HEADS // 2, LANE, RET_DV), F32)],
        compiler_params=_cparams(("parallel", "arbitrary")),
        name="ret",
    )(proj, proj, pos, freq_row)


def _outproj_body(odn_ref, z_ref, or_ref, rg_ref, x_ref, dnw_ref, gnw_ref, gnb_ref, w_ref, o_ref):
    def dn_head(h):
        hs = slice(h * DN_DV, (h + 1) * DN_DV)
        return _rms(odn_ref[:, hs].astype(F32), dnw_ref[...]) * _silu(z_ref[:, hs].astype(F32))

    def ret_head(h):
        hs = slice(h * RET_DV, (h + 1) * RET_DV)
        o = or_ref[:, hs]
        cen = o - jnp.mean(o, axis=-1, keepdims=True)
        y = cen * lax.rsqrt(jnp.mean(cen * cen, axis=-1, keepdims=True) + EPS)
        return (y * gnw_ref[:, hs] + gnb_ref[:, hs]) * _silu(rg_ref[:, hs].astype(F32))

    acc = x_ref[...]
    pair = 2 * DN_DV
    heads = [(dn_head, h) for h in range(DN_HEADS)] + [(ret_head, h) for h in range(RET_HEADS)]
    for c in range(len(heads) // 2):
        (f0, h0), (f1, h1) = heads[2 * c], heads[2 * c + 1]
        mix = jnp.concatenate([f0(h0), f1(h1)], axis=1).astype(BF16)
        acc = acc + jnp.dot(mix, w_ref[c * pair:(c + 1) * pair, :], preferred_element_type=F32)
    o_ref[...] = acc


def _outproj(o_dn, proj, o_r, x, dnw, gnw, gnb, w_out):
    t, d = x.shape
    tm = TM_DENSE
    full = lambda shape: pl.BlockSpec(shape, lambda i: (0, 0))
    return pl.pallas_call(
        _outproj_body,
        grid=(t // tm,),
        in_specs=[pl.BlockSpec((tm, DN_V), lambda i: (i, 0)),
                  pl.BlockSpec((tm, DN_V), lambda i: (i, 3)),
                  pl.BlockSpec((tm, RET_V), lambda i: (i, 0)),
                  pl.BlockSpec((tm, RET_V), lambda i: (i, 6)),
                  pl.BlockSpec((tm, d), lambda i: (i, 0)),
                  full((1, DN_DV)), full((1, RET_V)), full((1, RET_V)), full((DN_V + RET_V, d))],
        out_specs=pl.BlockSpec((tm, d), lambda i: (i, 0)),
        out_shape=jax.ShapeDtypeStruct((t, d), F32),
        compiler_params=_cparams(("parallel",)),
        name="outproj",
    )(o_dn, proj, o_r, proj, x, dnw, gnw, gnb, w_out)


def _pq_body(h_ref, nw_ref, w_ref, keys_ref, hnt_ref, inv_ref, sc_ref):
    x = h_ref[...]
    q, inv_rms = _rms_matmul(x, nw_ref[...], w_ref)
    hnt = (x * inv_rms * nw_ref[...]).T
    amax = jnp.maximum(jnp.max(jnp.abs(hnt), axis=0, keepdims=True), F8_TINY)
    hnt_ref[...] = (hnt * (F8_TARGET / amax)).astype(F8)
    inv_ref[...] = amax * (1.0 / F8_TARGET)
    half = PK_DQ // 2
    for hp in range(2 * PK_HEADS):
        sc_ref[hp] = _mm_nt(keys_ref[hp], q[:, hp * half:(hp + 1) * half])


def _pq(h1, nw, w_pq, keys):
    t, d = h1.shape
    tm = TM_DENSE
    nk = 2 * PK_HEADS
    return pl.pallas_call(
        _pq_body,
        grid=(t // tm,),
        in_specs=[pl.BlockSpec((tm, d), lambda i: (i, 0)),
                  pl.BlockSpec((1, d), lambda i: (0, 0)),
                  pl.BlockSpec((d, PK_HEADS * PK_DQ), lambda i: (0, 0)),
                  pl.BlockSpec((nk, N_KEYS, PK_DQ // 2), lambda i: (0, 0, 0))],
        out_specs=[pl.BlockSpec((d, tm), lambda i: (0, i)),
                   pl.BlockSpec((1, tm), lambda i: (0, i)),
                   pl.BlockSpec((nk, N_KEYS, tm), lambda i: (0, 0, i))],
        out_shape=[jax.ShapeDtypeStruct((d, t), F8),
                   jax.ShapeDtypeStruct((1, t), F32),
                   jax.ShapeDtypeStruct((nk, N_KEYS, t), F32)],
        compiler_params=_cparams(("parallel",)),
        name="pq",
    )(h1, nw, w_pq, keys)


def _route(a, bsc, exact):
    tl = a.shape[1]
    k = PK_TOPK
    keyid = lax.broadcasted_iota(jnp.int32, (N_KEYS, tl), 0)
    slot = lax.broadcasted_iota(jnp.int32, (k, tl), 0)

    def extract(s, ids, n_ids):
        m = jnp.max(s, axis=0, keepdims=True)
        hit = s == m
        if exact:
            hit = ids == jnp.min(jnp.where(hit, ids, n_ids), axis=0, keepdims=True)
        return m, hit

    def top_k(s):
        rank = jnp.full((N_KEYS, tl), float(k), F32)
        vals = jnp.zeros((k, tl), F32)
        for r in range(k):
            m, hit = extract(s, keyid, N_KEYS)
            rank = jnp.where(hit, float(r), rank)
            s = jnp.where(hit, NEG_INF, s)
            vals = jnp.where(slot == r, m, vals)
        return vals, rank

    av, rank_a = top_k(a)
    bv, rank_b = top_k(bsc)

    k2 = k // 2
    assert k2 & (k2 - 1) == 0
    n_cand = k + (k - 1) * k2
    cand = jnp.concatenate([av[0:1, :] + bv] + [av[r:r + 1, :] + bv[0:k2, :] for r in range(1, k)],
                           axis=0)
    row = lax.broadcasted_iota(jnp.int32, (n_cand, tl), 0)
    tail = row - k
    cid = jnp.where(row < k, row,
                    (1 + (tail >> (k2.bit_length() - 1))) * k + (tail & (k2 - 1)))
    work = cand
    for _ in range(k):
        _, hit = extract(work, cid, k * k)
        work = jnp.where(hit, NEG_INF, work)
    sel = (work == NEG_INF).astype(F32)
    zsum = jnp.sum(sel * jnp.exp(cand - cand[0:1, :]), axis=0, keepdims=True)

    cnt = jnp.zeros((N_KEYS, tl), F32)
    for r in range(k):
        lo, hi = (0, k) if r == 0 else (k + (r - 1) * k2, k + r * k2)
        cnt_r = jnp.sum(sel[lo:hi, :], axis=0, keepdims=True)
        cnt = jnp.where(rank_a == float(r), cnt_r, cnt)

    def full_count(x):
        return jnp.sum(x, axis=0, keepdims=True) == float(k)

    ok = (full_count((rank_a < float(k)).astype(F32)) & full_count((rank_b < float(k)).astype(F32))
          & full_count(sel))
    eb = jnp.exp(bsc - bv[0:1, :])
    ea = jnp.exp(a - av[0:1, :]) / zsum
    return rank_b, eb, cnt, ea, ok


def _router_body(sc_ref, u_ref, v_ref, rankb_ref, eb_ref, cnt_ref, ea_ref, u8_ref, uinv_ref, vt_ref):
    u = u_ref[...]
    amax = jnp.max(jnp.max(jnp.abs(u), axis=0, keepdims=True), axis=1, keepdims=True)
    amax = jnp.maximum(amax, F8_TINY)
    u8_ref[...] = (u * (F8_TARGET / amax)).astype(F8)
    uinv_ref[0] = jnp.broadcast_to(amax * (1.0 / F8_TARGET), uinv_ref.shape[1:])
    vt_ref[...] = v_ref[...].T.astype(BF16)

    def run(exact):
        rank_b, eb, cnt, ea, ok = _route(sc_ref[0], sc_ref[1], exact)
        rankb_ref[0] = rank_b.astype(BF16)
        eb_ref[0] = eb.astype(BF16)
        cnt_ref[0] = cnt
        ea_ref[0] = ea
        return ok

    ok = run(exact=False)
    n_bad = jnp.sum(jnp.where(ok, 0.0, 1.0), axis=1, keepdims=True)

    @pl.when(n_bad[0, 0] > 0.0)
    def _():
        run(exact=True)


def _router(scores, expert_u, expert_v):
    nk, n, t = scores.shape
    ne, d = expert_u.shape
    tl = TL_ROUTER
    nt = t // tl
    assert PK_HEADS * nt * N_KEYS == ne, "one 128-expert block is prepared per routing step"
    spec = pl.BlockSpec((1, n, tl), lambda h, j: (h, 0, j))
    shape = lambda dt: jax.ShapeDtypeStruct((PK_HEADS, n, t), dt)
    blk = lambda h, j: h * nt + j
    return pl.pallas_call(
        _router_body,
        grid=(PK_HEADS, nt),
        in_specs=[pl.BlockSpec((2, n, tl), lambda h, j: (h, 0, j)),
                  pl.BlockSpec((N_KEYS, d), lambda h, j: (blk(h, j), 0)),
                  pl.BlockSpec((N_KEYS, d), lambda h, j: (blk(h, j), 0))],
        out_specs=[spec, spec, spec, spec,
                   pl.BlockSpec((N_KEYS, d), lambda h, j: (blk(h, j), 0)),
                   pl.BlockSpec((1, 1, TM_PEER), lambda h, j: (blk(h, j), 0, 0)),
                   pl.BlockSpec((d, N_KEYS), lambda h, j: (0, blk(h, j)))],
        out_shape=[shape(BF16), shape(BF16), shape(F32), shape(F32),
                   jax.ShapeDtypeStruct((ne, d), F8),
                   jax.ShapeDtypeStruct((ne // N_KEYS, 1, TM_PEER), F32),
                   jax.ShapeDtypeStruct((d, ne), BF16)],
        compiler_params=_cparams(("parallel", "parallel")),
        name="router",
    )(scores, expert_u, expert_v)


def _peer_pre_body(u_ref, hnt_ref, o_ref):
    o_ref[...] = jnp.dot(u_ref[...], hnt_ref[...], preferred_element_type=F32)


def _peer_step(pre_ref, pre_next_ref, hnt_ref, u_ref, vt_ref, rankb_ref, eb_ref, cnt_ref, ea_ref,
               xinv_ref, uinv_ref, acc_ref, act_ref):
    k_piece = D_MODEL // NI_PEER
    pre_next = None
    for il in range(NI_PEER):
        rs = slice(il * N_KEYS, (il + 1) * N_KEYS)
        gate = jnp.zeros((N_KEYS, TM_PEER), BF16)
        for h in range(PK_HEADS):
            picked = rankb_ref[h] < cnt_ref[h, il:il + 1, :].astype(BF16)
            gate = gate + jnp.where(picked, eb_ref[h], 0.0) * ea_ref[h, il:il + 1, :].astype(BF16)
        unscale = xinv_ref[...] * uinv_ref[il]
        act_ref[rs, :] = _gelu(pre_ref[rs, :] * unscale).astype(BF16) * gate
        ks = slice(il * k_piece, (il + 1) * k_piece)
        zero = jnp.minimum(gate[:1, :], 0.0)
        x_piece = (hnt_ref[ks, :].astype(BF16) + zero).astype(F8)
        part = jnp.dot(u_ref[:, ks], x_piece, preferred_element_type=F32)
        pre_next = part if pre_next is None else pre_next + part
    pre_next_ref[...] = pre_next
    acc_ref[...] += jnp.dot(vt_ref[...], act_ref[...], preferred_element_type=F32)


def _peer_body(pre0_ref, hnt_ref, u_ref, vt_ref, rankb_ref, eb_ref, cnt_ref, ea_ref, xinv_ref,
               uinv_ref, o_ref, acc_ref, act_ref, pre_a_ref, pre_b_ref):
    i = pl.program_id(0)
    j = pl.program_id(1)
    args = (hnt_ref, u_ref, vt_ref, rankb_ref, eb_ref, cnt_ref, ea_ref, xinv_ref, uinv_ref, acc_ref,
            act_ref)

    @pl.when((i == 0) & (j == 0))
    def _():
        pre_a_ref[...] = pre0_ref[...]

    @pl.when(j == 0)
    def _():
        acc_ref[...] = jnp.zeros(acc_ref.shape, F32)

    @pl.when(lax.rem(j, 2) == 0)
    def _():
        _peer_step(pre_a_ref, pre_b_ref, *args)

    @pl.when(lax.rem(j, 2) == 1)
    def _():
        _peer_step(pre_b_ref, pre_a_ref, *args)

    @pl.when(j == pl.num_programs(1) - 1)
    def _():
        o_ref[...] = acc_ref[...].T


def _peer(hnt, xinv, u_f8, uinv, vt_bf, rankb, eb, cnt, ea):
    d, t = hnt.shape
    ne = u_f8.shape[0]
    tm, te = TM_PEER, TE_PEER
    ni, nj = t // tm, ne // te
    assert nj % 2 == 0, "pre-activation buffers alternate with the expert-tile index"
    pre0 = pl.pallas_call(
        _peer_pre_body,
        grid=(1,),
        in_specs=[pl.BlockSpec((te, d), lambda i: (0, 0)), pl.BlockSpec((d, tm), lambda i: (0, 0))],
        out_specs=pl.BlockSpec((te, tm), lambda i: (0, 0)),
        out_shape=jax.ShapeDtypeStruct((te, tm), F32),
        compiler_params=_cparams(("arbitrary",)),
        name="peer_pre",
    )(u_f8, hnt)

    def next_i(i, j):
        return jnp.minimum(i + (j + 1) // nj, ni - 1)

    key_spec = pl.BlockSpec((PK_HEADS, N_KEYS, tm), lambda i, j: (0, 0, i))
    blk_spec = pl.BlockSpec((PK_HEADS, NI_PEER, tm), lambda i, j: (0, j, i))
    return pl.pallas_call(
        _peer_body,
        grid=(ni, nj),
        in_specs=[pl.BlockSpec((te, tm), lambda i, j: (0, 0)),
                  pl.BlockSpec((d, tm), lambda i, j: (0, next_i(i, j))),
                  pl.BlockSpec((te, d), lambda i, j: ((j + 1) % nj, 0)),
                  pl.BlockSpec((d, te), lambda i, j: (0, j)),
                  key_spec, key_spec, blk_spec, blk_spec,
                  pl.BlockSpec((1, tm), lambda i, j: (0, i)),
                  pl.BlockSpec((NI_PEER, 1, tm), lambda i, j: (j, 0, 0))],
        out_specs=pl.BlockSpec((tm, d), lambda i, j: (i, 0)),
        out_shape=jax.ShapeDtypeStruct((t, d), F32),
        scratch_shapes=[pltpu.VMEM((d, tm), F32), pltpu.VMEM((te, tm), BF16),
                        pltpu.VMEM((te, tm), F32), pltpu.VMEM((te, tm), F32)],
        compiler_params=_cparams(("arbitrary", "arbitrary")),
        name="peer",
    )(pre0, hnt, u_f8, vt_bf, rankb, eb, cnt, ea, xinv, uinv)


def _ple_body(h_ref, po_ref, p_ref, nple_ref, wg_ref, wp_ref, nfin_ref, o_ref):
    h2 = h_ref[...] + po_ref[...]
    gate = jax.nn.sigmoid(
        jnp.dot(_rms(h2, nple_ref[...]).astype(BF16), wg_ref[...], preferred_element_type=F32))
    ple = jnp.dot(p_ref[...].astype(BF16), wp_ref[...], preferred_element_type=F32)
    o_ref[...] = _rms(h2 + gate * ple, nfin_ref[...])


def _ple(h1, peer_out, p, nple, wg, wp, nfin):
    t, d = h1.shape
    tm = TM_DENSE
    full = lambda shape: pl.BlockSpec(shape, lambda i: (0, 0))
    return pl.pallas_call(
        _ple_body,
        grid=(t // tm,),
        in_specs=[pl.BlockSpec((tm, d), lambda i: (i, 0)),
                  pl.BlockSpec((tm, d), lambda i: (i, 0)),
                  pl.BlockSpec((tm, PLE_DIM), lambda i: (i, 0)),
                  full((1, d)), full((d, d)), full((PLE_DIM, d)), full((1, d))],
        out_specs=pl.BlockSpec((tm, d), lambda i: (i, 0)),
        out_shape=jax.ShapeDtypeStruct((t, d), F32),
        compiler_params=_cparams(("parallel",)),
        name="ple",
    )(h1, peer_out, p, nple, wg, wp, nfin)


def _lane_row(vec, offset):
    return jnp.zeros((1, LANE), F32).at[0, offset:offset + vec.shape[0]].set(vec.astype(F32))


def _head_selector(offset):
    sel = np.zeros((LANE, DN_V), np.float32)
    for h in range(DN_HEADS):
        sel[offset + h, h * DN_DV:(h + 1) * DN_DV] = 1.0
    return jnp.asarray(sel, dtype=BF16)


def kernel(x, p, positions, norm_mix, w_in, conv_w, a_log, dt_bias, dn_norm, ret_gn_w, ret_gn_b,
           w_out, norm_ffn, w_pq, sub_keys, expert_u, expert_v, norm_ple, w_ple_gate, w_ple_proj,
           norm_final):
    b, s, d = x.shape
    t = b * s
    depth = w_in.shape[0]
    assert depth == 1, "the final rms_norm is fused into the single layer's ple kernel"
    half = RET_DK // 2
    inv_freq = ROPE_BASE ** (-jnp.arange(half, dtype=F32) / half)
    freq_row = jnp.tile(inv_freq, LANE // half).reshape(1, LANE)
    pos = positions.reshape(t, 1)
    selg, selb = _head_selector(0), _head_selector(DN_HEADS)

    h = x.reshape(t, d)
    for i in range(depth):
        w = w_in[i].astype(BF16)
        w_ab = jnp.pad(w[:, AB_OFF:AB_OFF + 2 * DN_HEADS], ((0, 0), (0, LANE - 2 * DN_HEADS)))
        proj, proj_ab = _inproj(h, norm_mix[i].reshape(1, d), w, w[:, AB_OFF + 2 * DN_HEADS:], w_ab)

        qa, kd, u, wv, o0, gend, w_out16, w_pq16, w_gate16 = _dn_local(
            proj, proj_ab, conv_w[i].astype(F32).T, _lane_row(a_log[i], 0), _lane_row(dt_bias[i], 0),
            selg, selb, (w_out[i], w_pq[i], w_ple_gate[i]), b, s)
        o_dn = _dn_scan(qa, kd, u, wv, o0, gend, b, s)
        o_r = _ret(proj, pos, freq_row, b, s)
        h1 = _outproj(o_dn, proj, o_r, h, dn_norm[i].reshape(1, DN_DV), ret_gn_w[i].reshape(1, RET_V),
                      ret_gn_b[i].reshape(1, RET_V), w_out16)

        keys = sub_keys[i].reshape(2 * PK_HEADS, N_KEYS, PK_DQ // 2).astype(BF16)
        hnt, xinv, scores = _pq(h1, norm_ffn[i].reshape(1, d), w_pq16, keys)
        rankb, eb, cnt, ea, u_f8, uinv, vt_bf = _router(scores, expert_u[i], expert_v[i])
        peer_out = _peer(hnt, xinv, u_f8, uinv, vt_bf, rankb, eb, cnt, ea)

        h = _ple(h1, peer_out, p[i].reshape(t, PLE_DIM), norm_ple[i].reshape(1, d),
                 w_gate16, w_ple_proj[i].astype(BF16), norm_final.reshape(1, d))
    return h.reshape(b, s, d)
```

```python
import math

import numpy as np
import jax
import jax.numpy as jnp
from jax import lax
from jax.experimental import pallas as pl
from jax.experimental.pallas import tpu as pltpu

F32 = jnp.float32
BF16 = jnp.bfloat16
F8 = jnp.float8_e4m3fn
F8_TARGET = 224.0
F8_TINY = 1e-30
NEG_INF = float("-inf")

EPS = 1e-6
D_MODEL = 2048
DN_HEADS = 8
DN_DK = 128
DN_DV = 128
CONV_W = 4
RET_HEADS = 8
RET_DK = 64
RET_DV = 128
CHUNK = 64
ROPE_BASE = 10000.0
N_KEYS = 128
PK_HEADS = 8
PK_DQ = 256
PK_TOPK = 16
PLE_DIM = 256

DN_QK = DN_HEADS * DN_DK
DN_V = DN_HEADS * DN_DV
CONV_CH = 2 * DN_QK + DN_V
RET_QK = RET_HEADS * RET_DK
RET_V = RET_HEADS * RET_DV
AB_OFF = CONV_CH + DN_V
LANE = 128
MXU_DEPTH = 256
HALO = 8
CONV_HALO = 16

TM_INPROJ = 1024
TN_INPROJ = 1024
TT_SEQ = 256
TM_DENSE = 256
TL_ROUTER = 512
TM_PEER = 512
NI_PEER = 8
TE_PEER = NI_PEER * N_KEYS
VMEM_LIMIT = 56 * 1024 * 1024


def _cparams(sem, flags=None):
    return pltpu.CompilerParams(dimension_semantics=sem, vmem_limit_bytes=VMEM_LIMIT, flags=flags)


def _rms(x, w):
    return x * lax.rsqrt(jnp.mean(x * x, axis=-1, keepdims=True) + EPS) * w


def _silu(x):
    return x * jax.nn.sigmoid(x)


def _softplus(x):
    return jnp.maximum(x, 0.0) + jnp.log1p(jnp.exp(-jnp.abs(x)))


def _gelu(x):
    return 0.5 * x * (1.0 + lax.erf(x * (2.0 ** -0.5)))


def _mm(a, b):
    return jnp.dot(a.astype(BF16), b.astype(BF16), preferred_element_type=F32)


def _mm_nt(a, b):
    return lax.dot_general(a.astype(BF16), b.astype(BF16), (((1,), (1,)), ((), ())),
                           preferred_element_type=F32)


def _mm_tn(a, b):
    return lax.dot_general(a.astype(BF16), b.astype(BF16), (((0,), (0,)), ((), ())),
                           preferred_element_type=F32)


def _rms_matmul(x, norm_w, w_ref, k_chunk=MXU_DEPTH):
    acc = None
    sumsq = None
    for c in range(x.shape[1] // k_chunk):
        cs = slice(c * k_chunk, (c + 1) * k_chunk)
        xc = x[:, cs]
        part = jnp.sum(xc * xc, axis=-1, keepdims=True)
        sumsq = part if sumsq is None else sumsq + part
        prod = jnp.dot((xc * norm_w[:, cs]).astype(BF16), w_ref[cs, :], preferred_element_type=F32)
        acc = prod if acc is None else acc + prod
    inv_rms = lax.rsqrt(sumsq * (1.0 / x.shape[1]) + EPS)
    return acc * inv_rms, inv_rms


def _split3(x):
    hi = x.astype(BF16)
    rest = x - hi.astype(F32)
    mid = rest.astype(BF16)
    lo = (rest - mid.astype(F32)).astype(BF16)
    return hi, mid, lo


def _select_mm(a, b, dims=(((1,), (0,)), ((), ()))):
    if a.dtype == BF16:
        parts = [lax.dot_general(a, p, dims, preferred_element_type=F32) for p in _split3(b)]
    else:
        parts = [lax.dot_general(p, b, dims, preferred_element_type=F32) for p in _split3(a)]
    return parts[0] + parts[1] + parts[2]


def _inproj_body(x_ref, nw_ref, w_head_ref, w_tail_ref, wab_ref, o_ref, oab_ref, hn_ref):
    j = pl.program_id(1)
    n_head = AB_OFF // TN_INPROJ

    @pl.when(j == 0)
    def _():
        hn_ref[...] = _rms(x_ref[...], nw_ref[...]).astype(BF16)
        oab_ref[...] = jnp.dot(hn_ref[...], wab_ref[...], preferred_element_type=F32)

    @pl.when(j < n_head)
    def _():
        o_ref[...] = jnp.dot(hn_ref[...], w_head_ref[...], preferred_element_type=F32).astype(BF16)

    @pl.when(j >= n_head)
    def _():
        o_ref[...] = jnp.dot(hn_ref[...], w_tail_ref[...], preferred_element_type=F32).astype(BF16)


def _inproj(x, nw, w_all, w_tail, w_ab):
    t, d = x.shape
    n_head = AB_OFF // TN_INPROJ
    n = AB_OFF + w_tail.shape[1]
    return pl.pallas_call(
        _inproj_body,
        grid=(t // TM_INPROJ, n // TN_INPROJ),
        in_specs=[
            pl.BlockSpec((TM_INPROJ, d), lambda i, j: (i, 0)),
            pl.BlockSpec((1, d), lambda i, j: (0, 0)),
            pl.BlockSpec((d, TN_INPROJ), lambda i, j: (0, jnp.minimum(j, n_head - 1))),
            pl.BlockSpec((d, TN_INPROJ), lambda i, j: (0, jnp.maximum(j - n_head, 0))),
            pl.BlockSpec((d, LANE), lambda i, j: (0, 0)),
        ],
        out_specs=[
            pl.BlockSpec((TM_INPROJ, TN_INPROJ), lambda i, j: (i, j)),
            pl.BlockSpec((TM_INPROJ, LANE), lambda i, j: (i, 0)),
        ],
        out_shape=[jax.ShapeDtypeStruct((t, n), BF16), jax.ShapeDtypeStruct((t, LANE), F32)],
        scratch_shapes=[pltpu.VMEM((TM_INPROJ, d), BF16)],
        compiler_params=_cparams(("parallel", "arbitrary")),
        name="inproj",
    )(x, nw, w_all, w_tail, w_ab)


def _dn_local_body(q_ref, k_ref, v_ref, qh_ref, kh_ref, vh_ref, ab_ref, cw_ref, alog_ref, dtb_ref,
                   selg_ref, selb_ref, wa_ref, wb_ref, wc_ref, qa_ref, kd_ref, u_ref, w_ref, o0_ref,
                   gend_ref, wa16_ref, wb16_ref, wc16_ref, buf_ref):
    tt = TT_SEQ
    first_tile = pl.program_id(1) == 0
    for src, dst in ((wa_ref, wa16_ref), (wb_ref, wb16_ref), (wc_ref, wc16_ref)):
        dst[...] = src[...].astype(BF16)

    def conv_silu(main_ref, halo_ref, c0):
        halo = halo_ref[...].astype(F32)
        buf_ref[0:CONV_HALO, :] = jnp.where(first_tile, jnp.zeros_like(halo), halo)
        buf_ref[CONV_HALO:CONV_HALO + tt, :] = main_ref[...].astype(F32)
        y = None
        for j in range(CONV_W):
            off = CONV_HALO - (CONV_W - 1) + j
            term = buf_ref[off:off + tt, :] * cw_ref[j:j + 1, c0:c0 + DN_QK]
            y = term if y is None else y + term
        return _silu(y)

    q = conv_silu(q_ref, qh_ref, 0)
    k = conv_silu(k_ref, kh_ref, DN_QK)
    v = conv_silu(v_ref, vh_ref, 2 * DN_QK)

    row = lax.broadcasted_iota(jnp.int32, (tt, tt), 0)
    col = lax.broadcasted_iota(jnp.int32, (tt, tt), 1)

    def same_block(bits):
        return (row >> bits) == (col >> bits)

    blk8, blk16, blk32, blk64 = same_block(3), same_block(4), same_block(5), same_block(6)
    causal = blk64 & (col <= row)
    strict = blk64 & (col < row)
    eye = (row == col).astype(F32)

    ab = ab_ref[...]
    g = -jnp.exp(alog_ref[...]) * _softplus(ab + dtb_ref[...])
    beta = jax.nn.sigmoid(ab)
    def ones_where(mask):
        return jnp.where(mask, 1.0, 0.0).astype(BF16)

    gc = _select_mm(ones_where(causal), g)
    gtot = _select_mm(ones_where(blk64), g)
    e_gc = _select_mm(gc, selg_ref[...])
    e_gt = _select_mm(gtot, selg_ref[...])
    e_beta = _select_mm(beta, selb_ref[...])
    r128 = lax.broadcasted_iota(jnp.int32, (LANE, LANE), 0)
    c128 = lax.broadcasted_iota(jnp.int32, (LANE, LANE), 1)
    gc_t = _select_mm(ones_where(r128 == c128), gc, (((1,), (1,)), ((), ())))
    eg = jnp.exp(e_gc)
    kdec = jnp.exp(e_gt - e_gc)

    for c in range(tt // CHUNK):
        gend_ref[0, c:c + 1, :] = jnp.exp(e_gt[c * CHUNK:c * CHUNK + 1, :])
    gend_ref[0, tt // CHUNK:, :] = jnp.zeros((HALO - tt // CHUNK, DN_V), F32)

    heads = range(DN_HEADS)
    hsl = [slice(h * DN_DK, (h + 1) * DN_DK) for h in heads]

    def per_head(fn):
        return [fn(h) for h in heads]

    def l2n(x):
        return x * lax.rsqrt(jnp.sum(x * x, axis=-1, keepdims=True) + EPS)

    qh = per_head(lambda h: l2n(q[:, hsl[h]]) * (DN_DK ** -0.5))
    kh = per_head(lambda h: l2n(k[:, hsl[h]]))
    kb = per_head(lambda h: kh[h] * e_beta[:, hsl[h]])
    dmat = per_head(lambda h: jnp.exp(jnp.where(
        causal, jnp.concatenate([e_gc[:, hsl[h]]] * (tt // LANE), axis=1) - gc_t[h:h + 1, :], NEG_INF)))
    scores = per_head(lambda h: _mm_nt(jnp.concatenate([kb[h], qh[h]], axis=0), kh[h]))
    nmat = per_head(lambda h: jnp.where(strict, scores[h][:tt] * dmat[h], 0.0))
    attn = per_head(lambda h: scores[h][tt:] * dmat[h])

    n0 = per_head(lambda h: jnp.where(blk8, nmat[h], 0.0))
    n2 = per_head(lambda h: _mm(n0[h], n0[h]))
    n4 = per_head(lambda h: _mm(n2[h], n2[h]))
    inv = per_head(lambda h: eye - n0[h])
    inv = per_head(lambda h: inv[h] + _mm(inv[h], n2[h]))
    inv = per_head(lambda h: inv[h] + _mm(inv[h], n4[h]))
    for inner, outer in ((blk8, blk16), (blk16, blk32), (blk32, blk64)):
        level = outer & jnp.logical_not(inner)
        prod = per_head(lambda h: _mm(inv[h], jnp.where(level, nmat[h], 0.0)))
        inv = per_head(lambda h: inv[h] - _mm(prod[h], inv[h]))

    sol = per_head(lambda h: _mm(inv[h], jnp.concatenate(
        [v[:, hsl[h]] * e_beta[:, hsl[h]], kb[h] * eg[:, hsl[h]]], axis=1)))
    asol = per_head(lambda h: _mm(attn[h], sol[h]))
    for h in heads:
        hs = hsl[h]
        u_ref[:, hs] = sol[h][:, :DN_DV].astype(BF16)
        w_ref[:, hs] = sol[h][:, DN_DV:].astype(BF16)
        o0_ref[:, hs] = asol[h][:, :DN_DV].astype(BF16)
        qa_ref[:, hs] = (qh[h] * eg[:, hs] - asol[h][:, DN_DV:]).astype(BF16)
        kd_ref[:, hs] = (kh[h] * kdec[:, hs]).astype(BF16)


def _dn_local(proj, proj_ab, cw, alog_row, dtb_row, selg, selb, later_weights, b, s):
    t = b * s
    nt = s // TT_SEQ
    rows_per_halo = TT_SEQ // CONV_HALO
    w_rows, w_cols = later_weights[0].shape
    assert all(w.shape == (w_rows, w_cols) for w in later_weights) and w_rows % (b * nt) == 0
    slab = pl.BlockSpec((w_rows // (b * nt), w_cols), lambda bi, i: (bi * nt + i, 0))

    def main_spec(cb):
        return pl.BlockSpec((TT_SEQ, DN_QK), lambda bi, i: (bi * nt + i, cb))

    def halo_spec(cb):
        return pl.BlockSpec(
            (CONV_HALO, DN_QK), lambda bi, i: (jnp.maximum((bi * nt + i) * rows_per_halo - 1, 0), cb))

    def full(shape):
        return pl.BlockSpec(shape, lambda bi, i: (0,) * len(shape))

    tok_spec = pl.BlockSpec((TT_SEQ, DN_V), lambda bi, i: (bi * nt + i, 0))
    return pl.pallas_call(
        _dn_local_body,
        grid=(b, nt),
        in_specs=[main_spec(0), main_spec(1), main_spec(2), halo_spec(0), halo_spec(1), halo_spec(2),
                  pl.BlockSpec((TT_SEQ, LANE), lambda bi, i: (bi * nt + i, 0)),
                  full((CONV_W, CONV_CH)), full((1, LANE)), full((1, LANE)),
                  full((LANE, DN_V)), full((LANE, DN_V)), slab, slab, slab],
        out_specs=[tok_spec, tok_spec, tok_spec, tok_spec, tok_spec,
                   pl.BlockSpec((1, HALO, DN_V), lambda bi, i: (bi * nt + i, 0, 0)), slab, slab, slab],
        out_shape=[jax.ShapeDtypeStruct((t, DN_V), BF16),
                   jax.ShapeDtypeStruct((t, DN_V), BF16),
                   jax.ShapeDtypeStruct((t, DN_V), BF16),
                   jax.ShapeDtypeStruct((t, DN_V), BF16),
                   jax.ShapeDtypeStruct((t, DN_V), BF16),
                   jax.ShapeDtypeStruct((t // TT_SEQ, HALO, DN_V), F32)]
                  + [jax.ShapeDtypeStruct((w_rows, w_cols), BF16)] * 3,
        scratch_shapes=[pltpu.VMEM((CONV_HALO + TT_SEQ, DN_QK), F32)],
        compiler_params=_cparams(("parallel", "parallel")),
        name="dn_local",
    )(proj, proj, proj, proj, proj, proj, proj_ab, cw, alog_row, dtb_row, selg, selb, *later_weights)


def _dn_scan_body(qa_ref, kd_ref, u_ref, w_ref, o0_ref, gend_ref, o_ref, s_ref):
    nb = qa_ref.shape[0]

    @pl.when(pl.program_id(0) == 0)
    def _():
        s_ref[...] = jnp.zeros(s_ref.shape, F32)

    chains = [(b, h, slice(h * DN_DK, (h + 1) * DN_DK)) for b in range(nb) for h in range(DN_HEADS)]
    states = [s_ref[n] for n in range(len(chains))]
    for c in range(TT_SEQ // CHUNK):
        rs = slice(c * CHUNK, (c + 1) * CHUNK)
        both = [jnp.dot(jnp.concatenate([w_ref[b, rs, hs], qa_ref[b, rs, hs]], axis=0),
                        states[n].astype(BF16), preferred_element_type=F32)
                for n, (b, h, hs) in enumerate(chains)]
        for n, (b, h, hs) in enumerate(chains):
            o_ref[b, rs, hs] = (both[n][CHUNK:] + o0_ref[b, rs, hs]).astype(o_ref.dtype)
        states = [states[n] * gend_ref[b, 0, c:c + 1, hs]
                  + _mm_tn(kd_ref[b, rs, hs], u_ref[b, rs, hs] - both[n][:CHUNK])
                  for n, (b, h, hs) in enumerate(chains)]
    for n in range(len(chains)):
        s_ref[n] = states[n]


def _dn_scan(qa, kd, u, w, o0, gend, b, s):
    nt = s // TT_SEQ
    tok = lambda a: a.reshape(b, s, DN_V)
    tok_spec = pl.BlockSpec((b, TT_SEQ, DN_V), lambda i: (0, i, 0))
    out = pl.pallas_call(
        _dn_scan_body,
        grid=(nt,),
        in_specs=[tok_spec, tok_spec, tok_spec, tok_spec, tok_spec,
                  pl.BlockSpec((b, 1, HALO, DN_V), lambda i: (0, i, 0, 0))],
        out_specs=tok_spec,
        out_shape=jax.ShapeDtypeStruct((b, s, DN_V), BF16),
        scratch_shapes=[pltpu.VMEM((b * DN_HEADS, DN_DK, DN_DV), F32)],
        compiler_params=_cparams(("arbitrary",)),
        name="dn_scan",
    )(tok(qa), tok(kd), tok(u), tok(w), tok(o0), gend.reshape(b, nt, HALO, DN_V))
    return out.reshape(b * s, DN_V)


def _log_gamma(h):
    return math.log1p(-(2.0 ** (-5.0 - h)))


def _ret_body(qk_ref, v_ref, pos_ref, freq_ref, o_ref, s_ref):
    tt = TT_SEQ

    @pl.when(pl.program_id(1) == 0)
    def _():
        s_ref[...] = jnp.zeros(s_ref.shape, F32)

    ang = pos_ref[...].astype(F32) * freq_ref[...]
    lane = lax.broadcasted_iota(jnp.int32, (tt, LANE), 1)
    first_half = (lane & (RET_DK // 2)) == 0
    cos = jnp.cos(ang)
    sin = jnp.sin(ang)
    ssin = jnp.where(first_half, -sin, sin)

    def rotary(x):
        swapped = jnp.where(first_half, pltpu.roll(x, LANE - RET_DK // 2, axis=1),
                            pltpu.roll(x, RET_DK // 2, axis=1))
        return x * cos + swapped * ssin

    row = lax.broadcasted_iota(jnp.int32, (tt, tt), 0)
    col = lax.broadcasted_iota(jnp.int32, (tt, tt), 1)
    rel = (row - col).astype(F32)
    causal = row >= col
    trow = lax.broadcasted_iota(jnp.int32, (tt, LANE), 0).astype(F32)
    srow = lax.broadcasted_iota(jnp.int32, (LANE, LANE), 0)

    pairs = range(RET_HEADS // 2)
    heads = range(RET_HEADS)
    qp = [rotary(qk_ref[:, pr * LANE:(pr + 1) * LANE].astype(F32)) for pr in pairs]
    kp = [rotary(qk_ref[:, RET_QK + pr * LANE:RET_QK + (pr + 1) * LANE].astype(F32)) * (RET_DK ** -0.5)
          for pr in pairs]
    state = [s_ref[pr] for pr in pairs]
    mine = [(lane < RET_DK) if h % 2 == 0 else (lane >= RET_DK) for h in heads]
    lg = [_log_gamma(h) for h in heads]
    qm = [jnp.where(mine[h], qp[h // 2], 0.0).astype(BF16) for h in heads]
    vh = [v_ref[:, h * RET_DV:(h + 1) * RET_DV].astype(BF16) for h in heads]
    att = [_mm_nt(qm[h], kp[h // 2]) * jnp.exp(jnp.where(causal, rel * lg[h], NEG_INF)) for h in heads]
    inner = [jnp.dot(att[h].astype(BF16), vh[h], preferred_element_type=F32) for h in heads]
    cross = [jnp.dot(qm[h], state[h // 2].astype(BF16), preferred_element_type=F32)
             * jnp.exp((trow + 1.0) * lg[h]) for h in heads]
    for h in heads:
        o_ref[:, h * RET_DV:(h + 1) * RET_DV] = inner[h] + cross[h]
    update = [_mm_tn(jnp.where(mine[h], kp[h // 2], 0.0) * jnp.exp((tt - 1.0 - trow) * lg[h]), vh[h])
              for h in heads]
    for pr in pairs:
        decay = jnp.where(srow < RET_DK, math.exp(tt * lg[2 * pr]), math.exp(tt * lg[2 * pr + 1]))
        s_ref[pr] = state[pr] * decay + update[2 * pr] + update[2 * pr + 1]


def _ret(proj, pos, freq_row, b, s):
    t = b * s
    nt = s // TT_SEQ
    return pl.pallas_call(
        _ret_body,
        grid=(b, nt),
        in_specs=[pl.BlockSpec((TT_SEQ, 2 * RET_QK), lambda bi, i: (bi * nt + i, 4)),
                  pl.BlockSpec((TT_SEQ, RET_V), lambda bi, i: (bi * nt + i, 5)),
                  pl.BlockSpec((TT_SEQ, 1), lambda bi, i: (bi * nt + i, 0)),
                  pl.BlockSpec((1, LANE), lambda bi, i: (0, 0))],
        out_specs=pl.BlockSpec((TT_SEQ, RET_V), lambda bi, i: (bi * nt + i, 0)),
        out_shape=jax.ShapeDtypeStruct((t, RET_V), F32),
        scratch_shapes=[pltpu.VMEM((RET_HEADS // 2, LANE, RET_DV), F32)],
        compiler_params=_cparams(("parallel", "arbitrary")),
        name="ret",
    )(proj, proj, pos, freq_row)


def _outproj_body(odn_ref, z_ref, or_ref, rg_ref, x_ref, dnw_ref, gnw_ref, gnb_ref, w_ref, o_ref):
    def dn_head(h):
        hs = slice(h * DN_DV, (h + 1) * DN_DV)
        return _rms(odn_ref[:, hs].astype(F32), dnw_ref[...]) * _silu(z_ref[:, hs].astype(F32))

    def ret_head(h):
        hs = slice(h * RET_DV, (h + 1) * RET_DV)
        o = or_ref[:, hs]
        cen = o - jnp.mean(o, axis=-1, keepdims=True)
        y = cen * lax.rsqrt(jnp.mean(cen * cen, axis=-1, keepdims=True) + EPS)
        return (y * gnw_ref[:, hs] + gnb_ref[:, hs]) * _silu(rg_ref[:, hs].astype(F32))

    acc = x_ref[...]
    pair = 2 * DN_DV
    heads = [(dn_head, h) for h in range(DN_HEADS)] + [(ret_head, h) for h in range(RET_HEADS)]
    for c in range(len(heads) // 2):
        (f0, h0), (f1, h1) = heads[2 * c], heads[2 * c + 1]
        mix = jnp.concatenate([f0(h0), f1(h1)], axis=1).astype(BF16)
        acc = acc + jnp.dot(mix, w_ref[c * pair:(c + 1) * pair, :], preferred_element_type=F32)
    o_ref[...] = acc


def _outproj(o_dn, proj, o_r, x, dnw, gnw, gnb, w_out):
    t, d = x.shape
    tm = TM_DENSE
    full = lambda shape: pl.BlockSpec(shape, lambda i: (0, 0))
    return pl.pallas_call(
        _outproj_body,
        grid=(t // tm,),
        in_specs=[pl.BlockSpec((tm, DN_V), lambda i: (i, 0)),
                  pl.BlockSpec((tm, DN_V), lambda i: (i, 3)),
                  pl.BlockSpec((tm, RET_V), lambda i: (i, 0)),
                  pl.BlockSpec((tm, RET_V), lambda i: (i, 6)),
                  pl.BlockSpec((tm, d), lambda i: (i, 0)),
                  full((1, DN_DV)), full((1, RET_V)), full((1, RET_V)), full((DN_V + RET_V, d))],
        out_specs=pl.BlockSpec((tm, d), lambda i: (i, 0)),
        out_shape=jax.ShapeDtypeStruct((t, d), F32),
        compiler_params=_cparams(("parallel",)),
        name="outproj",
    )(o_dn, proj, o_r, proj, x, dnw, gnw, gnb, w_out)


def _pq_body(h_ref, nw_ref, w_ref, keys_ref, hnt_ref, inv_ref, sc_ref):
    x = h_ref[...]
    q, inv_rms = _rms_matmul(x, nw_ref[...], w_ref)
    hnt = (x * inv_rms * nw_ref[...]).T
    amax = jnp.maximum(jnp.max(jnp.abs(hnt), axis=0, keepdims=True), F8_TINY)
    hnt_ref[...] = (hnt * (F8_TARGET / amax)).astype(F8)
    inv_ref[...] = amax * (1.0 / F8_TARGET)
    half = PK_DQ // 2
    for hp in range(2 * PK_HEADS):
        sc_ref[hp] = _mm_nt(keys_ref[hp], q[:, hp * half:(hp + 1) * half])


def _pq(h1, nw, w_pq, keys):
    t, d = h1.shape
    tm = TM_DENSE
    nk = 2 * PK_HEADS
    return pl.pallas_call(
        _pq_body,
        grid=(t // tm,),
        in_specs=[pl.BlockSpec((tm, d), lambda i: (i, 0)),
                  pl.BlockSpec((1, d), lambda i: (0, 0)),
                  pl.BlockSpec((d, PK_HEADS * PK_DQ), lambda i: (0, 0)),
                  pl.BlockSpec((nk, N_KEYS, PK_DQ // 2), lambda i: (0, 0, 0))],
        out_specs=[pl.BlockSpec((d, tm), lambda i: (0, i)),
                   pl.BlockSpec((1, tm), lambda i: (0, i)),
                   pl.BlockSpec((nk, N_KEYS, tm), lambda i: (0, 0, i))],
        out_shape=[jax.ShapeDtypeStruct((d, t), F8),
                   jax.ShapeDtypeStruct((1, t), F32),
                   jax.ShapeDtypeStruct((nk, N_KEYS, t), F32)],
        compiler_params=_cparams(("parallel",)),
        name="pq",
    )(h1, nw, w_pq, keys)


def _route(a, bsc, exact):
    tl = a.shape[1]
    k = PK_TOPK
    keyid = lax.broadcasted_iota(jnp.int32, (N_KEYS, tl), 0)
    slot = lax.broadcasted_iota(jnp.int32, (k, tl), 0)

    def extract(s, ids, n_ids):
        m = jnp.max(s, axis=0, keepdims=True)
        hit = s == m
        if exact:
            hit = ids == jnp.min(jnp.where(hit, ids, n_ids), axis=0, keepdims=True)
        return m, hit

    def top_k(s):
        rank = jnp.full((N_KEYS, tl), float(k), F32)
        vals = jnp.zeros((k, tl), F32)
        for r in range(k):
            m, hit = extract(s, keyid, N_KEYS)
            rank = jnp.where(hit, float(r), rank)
            s = jnp.where(hit, NEG_INF, s)
            vals = jnp.where(slot == r, m, vals)
        return vals, rank

    av, rank_a = top_k(a)
    bv, rank_b = top_k(bsc)

    k2 = k // 2
    assert k2 & (k2 - 1) == 0
    n_cand = k + (k - 1) * k2
    cand = jnp.concatenate([av[0:1, :] + bv] + [av[r:r + 1, :] + bv[0:k2, :] for r in range(1, k)],
                           axis=0)
    row = lax.broadcasted_iota(jnp.int32, (n_cand, tl), 0)
    tail = row - k
    cid = jnp.where(row < k, row,
                    (1 + (tail >> (k2.bit_length() - 1))) * k + (tail & (k2 - 1)))
    work = cand
    for _ in range(k):
        _, hit = extract(work, cid, k * k)
        work = jnp.where(hit, NEG_INF, work)
    sel = (work == NEG_INF).astype(F32)
    zsum = jnp.sum(sel * jnp.exp(cand - cand[0:1, :]), axis=0, keepdims=True)

    cnt = jnp.zeros((N_KEYS, tl), F32)
    for r in range(k):
        lo, hi = (0, k) if r == 0 else (k + (r - 1) * k2, k + r * k2)
        cnt_r = jnp.sum(sel[lo:hi, :], axis=0, keepdims=True)
        cnt = jnp.where(rank_a == float(r), cnt_r, cnt)

    def full_count(x):
        return jnp.sum(x, axis=0, keepdims=True) == float(k)

    ok = (full_count((rank_a < float(k)).astype(F32)) & full_count((rank_b < float(k)).astype(F32))
          & full_count(sel))
    eb = jnp.exp(bsc - bv[0:1, :])
    ea = jnp.exp(a - av[0:1, :]) / zsum
    return rank_b, eb, cnt, ea, ok


def _router_body(sc_ref, u_ref, v_ref, rankb_ref, eb_ref, cnt_ref, ea_ref, u8_ref, uinv_ref, vt_ref):
    u = u_ref[...]
    amax = jnp.max(jnp.max(jnp.abs(u), axis=0, keepdims=True), axis=1, keepdims=True)
    amax = jnp.maximum(amax, F8_TINY)
    u8_ref[...] = (u * (F8_TARGET / amax)).astype(F8)
    uinv_ref[0] = jnp.broadcast_to(amax * (1.0 / F8_TARGET), uinv_ref.shape[1:])
    vt_ref[...] = v_ref[...].T.astype(BF16)

    def run(exact):
        rank_b, eb, cnt, ea, ok = _route(sc_ref[0], sc_ref[1], exact)
        rankb_ref[0] = rank_b.astype(BF16)
        eb_ref[0] = eb.astype(BF16)
        cnt_ref[0] = cnt
        ea_ref[0] = ea
        return ok

    ok = run(exact=False)
    n_bad = jnp.sum(jnp.where(ok, 0.0, 1.0), axis=1, keepdims=True)

    @pl.when(n_bad[0, 0] > 0.0)
    def _():
        run(exact=True)


def _router(scores, expert_u, expert_v):
    nk, n, t = scores.shape
    ne, d = expert_u.shape
    tl = TL_ROUTER
    nt = t // tl
    assert PK_HEADS * nt * N_KEYS == ne, "one 128-expert block is prepared per routing step"
    spec = pl.BlockSpec((1, n, tl), lambda h, j: (h, 0, j))
    shape = lambda dt: jax.ShapeDtypeStruct((PK_HEADS, n, t), dt)
    blk = lambda h, j: h * nt + j
    return pl.pallas_call(
        _router_body,
        grid=(PK_HEADS, nt),
        in_specs=[pl.BlockSpec((2, n, tl), lambda h, j: (h, 0, j)),
                  pl.BlockSpec((N_KEYS, d), lambda h, j: (blk(h, j), 0)),
                  pl.BlockSpec((N_KEYS, d), lambda h, j: (blk(h, j), 0))],
        out_specs=[spec, spec, spec, spec,
                   pl.BlockSpec((N_KEYS, d), lambda h, j: (blk(h, j), 0)),
                   pl.BlockSpec((1, 1, TM_PEER), lambda h, j: (blk(h, j), 0, 0)),
                   pl.BlockSpec((d, N_KEYS), lambda h, j: (0, blk(h, j)))],
        out_shape=[shape(BF16), shape(BF16), shape(F32), shape(F32),
                   jax.ShapeDtypeStruct((ne, d), F8),
                   jax.ShapeDtypeStruct((ne // N_KEYS, 1, TM_PEER), F32),
                   jax.ShapeDtypeStruct((d, ne), BF16)],
        compiler_params=_cparams(("parallel", "parallel")),
        name="router",
    )(scores, expert_u, expert_v)


def _peer_pre_body(u_ref, hnt_ref, o_ref):
    o_ref[...] = jnp.dot(u_ref[...], hnt_ref[...], preferred_element_type=F32)


def _peer_step(pre_ref, pre_next_ref, hnt_ref, u_ref, vt_ref, rankb_ref, eb_ref, cnt_ref, ea_ref,
               xinv_ref, uinv_ref, acc_ref, act_ref):
    k_piece = D_MODEL // NI_PEER
    pre_next = None
    for il in range(NI_PEER):
        rs = slice(il * N_KEYS, (il + 1) * N_KEYS)
        gate = jnp.zeros((N_KEYS, TM_PEER), BF16)
        for h in range(PK_HEADS):
            picked = rankb_ref[h] < cnt_ref[h, il:il + 1, :].astype(BF16)
            gate = gate + jnp.where(picked, eb_ref[h], 0.0) * ea_ref[h, il:il + 1, :].astype(BF16)
        unscale = xinv_ref[...] * uinv_ref[il]
        act_ref[rs, :] = _gelu(pre_ref[rs, :] * unscale).astype(BF16) * gate
        ks = slice(il * k_piece, (il + 1) * k_piece)
        zero = jnp.minimum(gate[:1, :], 0.0)
        x_piece = (hnt_ref[ks, :].astype(BF16) + zero).astype(F8)
        part = jnp.dot(u_ref[:, ks], x_piece, preferred_element_type=F32)
        pre_next = part if pre_next is None else pre_next + part
    pre_next_ref[...] = pre_next
    acc_ref[...] += jnp.dot(vt_ref[...], act_ref[...], preferred_element_type=F32)


def _peer_body(pre0_ref, hnt_ref, u_ref, vt_ref, rankb_ref, eb_ref, cnt_ref, ea_ref, xinv_ref,
               uinv_ref, o_ref, acc_ref, act_ref, pre_a_ref, pre_b_ref):
    i = pl.program_id(0)
    j = pl.program_id(1)
    args = (hnt_ref, u_ref, vt_ref, rankb_ref, eb_ref, cnt_ref, ea_ref, xinv_ref, uinv_ref, acc_ref,
            act_ref)

    @pl.when((i == 0) & (j == 0))
    def _():
        pre_a_ref[...] = pre0_ref[...]

    @pl.when(j == 0)
    def _():
        acc_ref[...] = jnp.zeros(acc_ref.shape, F32)

    @pl.when(lax.rem(j, 2) == 0)
    def _():
        _peer_step(pre_a_ref, pre_b_ref, *args)

    @pl.when(lax.rem(j, 2) == 1)
    def _():
        _peer_step(pre_b_ref, pre_a_ref, *args)

    @pl.when(j == pl.num_programs(1) - 1)
    def _():
        o_ref[...] = acc_ref[...].T


def _peer(hnt, xinv, u_f8, uinv, vt_bf, rankb, eb, cnt, ea):
    d, t = hnt.shape
    ne = u_f8.shape[0]
    tm, te = TM_PEER, TE_PEER
    ni, nj = t // tm, ne // te
    assert nj % 2 == 0, "pre-activation buffers alternate with the expert-tile index"
    pre0 = pl.pallas_call(
        _peer_pre_body,
        grid=(1,),
        in_specs=[pl.BlockSpec((te, d), lambda i: (0, 0)), pl.BlockSpec((d, tm), lambda i: (0, 0))],
        out_specs=pl.BlockSpec((te, tm), lambda i: (0, 0)),
        out_shape=jax.ShapeDtypeStruct((te, tm), F32),
        compiler_params=_cparams(("arbitrary",)),
        name="peer_pre",
    )(u_f8, hnt)

    def next_i(i, j):
        return jnp.minimum(i + (j + 1) // nj, ni - 1)

    key_spec = pl.BlockSpec((PK_HEADS, N_KEYS, tm), lambda i, j: (0, 0, i))
    blk_spec = pl.BlockSpec((PK_HEADS, NI_PEER, tm), lambda i, j: (0, j, i))
    return pl.pallas_call(
        _peer_body,
        grid=(ni, nj),
        in_specs=[pl.BlockSpec((te, tm), lambda i, j: (0, 0)),
                  pl.BlockSpec((d, tm), lambda i, j: (0, next_i(i, j))),
                  pl.BlockSpec((te, d), lambda i, j: ((j + 1) % nj, 0)),
                  pl.BlockSpec((d, te), lambda i, j: (0, j)),
                  key_spec, key_spec, blk_spec, blk_spec,
                  pl.BlockSpec((1, tm), lambda i, j: (0, i)),
                  pl.BlockSpec((NI_PEER, 1, tm), lambda i, j: (j, 0, 0))],
        out_specs=pl.BlockSpec((tm, d), lambda i, j: (i, 0)),
        out_shape=jax.ShapeDtypeStruct((t, d), F32),
        scratch_shapes=[pltpu.VMEM((d, tm), F32), pltpu.VMEM((te, tm), BF16),
                        pltpu.VMEM((te, tm), F32), pltpu.VMEM((te, tm), F32)],
        compiler_params=_cparams(("arbitrary", "arbitrary")),
        name="peer",
    )(pre0, hnt, u_f8, vt_bf, rankb, eb, cnt, ea, xinv, uinv)


def _ple_body(h_ref, po_ref, p_ref, nple_ref, wg_ref, wp_ref, nfin_ref, o_ref):
    h2 = h_ref[...] + po_ref[...]
    gate = jax.nn.sigmoid(
        jnp.dot(_rms(h2, nple_ref[...]).astype(BF16), wg_ref[...], preferred_element_type=F32))
    ple = jnp.dot(p_ref[...].astype(BF16), wp_ref[...], preferred_element_type=F32)
    o_ref[...] = _rms(h2 + gate * ple, nfin_ref[...])


def _ple(h1, peer_out, p, nple, wg, wp, nfin):
    t, d = h1.shape
    tm = TM_DENSE
    full = lambda shape: pl.BlockSpec(shape, lambda i: (0, 0))
    return pl.pallas_call(
        _ple_body,
        grid=(t // tm,),
        in_specs=[pl.BlockSpec((tm, d), lambda i: (i, 0)),
                  pl.BlockSpec((tm, d), lambda i: (i, 0)),
                  pl.BlockSpec((tm, PLE_DIM), lambda i: (i, 0)),
                  full((1, d)), full((d, d)), full((PLE_DIM, d)), full((1, d))],
        out_specs=pl.BlockSpec((tm, d), lambda i: (i, 0)),
        out_shape=jax.ShapeDtypeStruct((t, d), F32),
        compiler_params=_cparams(("parallel",)),
        name="ple",
    )(h1, peer_out, p, nple, wg, wp, nfin)


def _lane_row(vec, offset):
    return jnp.zeros((1, LANE), F32).at[0, offset:offset + vec.shape[0]].set(vec.astype(F32))


def _head_selector(offset):
    sel = np.zeros((LANE, DN_V), np.float32)
    for h in range(DN_HEADS):
        sel[offset + h, h * DN_DV:(h + 1) * DN_DV] = 1.0
    return jnp.asarray(sel, dtype=BF16)


def kernel(x, p, positions, norm_mix, w_in, conv_w, a_log, dt_bias, dn_norm, ret_gn_w, ret_gn_b,
           w_out, norm_ffn, w_pq, sub_keys, expert_u, expert_v, norm_ple, w_ple_gate, w_ple_proj,
           norm_final):
    b, s, d = x.shape
    t = b * s
    depth = w_in.shape[0]
    assert depth == 1, "the final rms_norm is fused into the single layer's ple kernel"
    half = RET_DK // 2
    inv_freq = ROPE_BASE ** (-jnp.arange(half, dtype=F32) / half)
    freq_row = jnp.tile(inv_freq, LANE // half).reshape(1, LANE)
    pos = positions.reshape(t, 1)
    selg, selb = _head_selector(0), _head_selector(DN_HEADS)

    h = x.reshape(t, d)
    for i in range(depth):
        w = w_in[i].astype(BF16)
        w_ab = jnp.pad(w[:, AB_OFF:AB_OFF + 2 * DN_HEADS], ((0, 0), (0, LANE - 2 * DN_HEADS)))
        proj, proj_ab = _inproj(h, norm_mix[i].reshape(1, d), w, w[:, AB_OFF + 2 * DN_HEADS:], w_ab)

        qa, kd, u, wv, o0, gend, w_out16, w_pq16, w_gate16 = _dn_local(
            proj, proj_ab, conv_w[i].astype(F32).T, _lane_row(a_log[i], 0), _lane_row(dt_bias[i], 0),
            selg, selb, (w_out[i], w_pq[i], w_ple_gate[i]), b, s)
        o_dn = _dn_scan(qa, kd, u, wv, o0, gend, b, s)
        o_r = _ret(proj, pos, freq_row, b, s)
        h1 = _outproj(o_dn, proj, o_r, h, dn_norm[i].reshape(1, DN_DV), ret_gn_w[i].reshape(1, RET_V),
                      ret_gn_b[i].reshape(1, RET_V), w_out16)

        keys = sub_keys[i].reshape(2 * PK_HEADS, N_KEYS, PK_DQ // 2).astype(BF16)
        hnt, xinv, scores = _pq(h1, norm_ffn[i].reshape(1, d), w_pq16, keys)
        rankb, eb, cnt, ea, u_f8, uinv, vt_bf = _router(scores, expert_u[i], expert_v[i])
        peer_out = _peer(hnt, xinv, u_f8, uinv, vt_bf, rankb, eb, cnt, ea)

        h = _ple(h1, peer_out, p[i].reshape(t, PLE_DIM), norm_ple[i].reshape(1, d),
                 w_gate16, w_ple_proj[i].astype(BF16), norm_final.reshape(1, d))
    return h.reshape(b, s, d)
```

```python
import math

import numpy as np
import jax
import jax.numpy as jnp
from jax import lax
from jax.experimental import pallas as pl
from jax.experimental.pallas import tpu as pltpu

F32 = jnp.float32
BF16 = jnp.bfloat16
F8 = jnp.float8_e4m3fn
F8_TARGET = 224.0
F8_TINY = 1e-30
NEG_INF = float("-inf")

EPS = 1e-6
D_MODEL = 2048
DN_HEADS = 8
DN_DK = 128
DN_DV = 128
CONV_W = 4
RET_HEADS = 8
RET_DK = 64
RET_DV = 128
CHUNK = 64
ROPE_BASE = 10000.0
N_KEYS = 128
PK_HEADS = 8
PK_DQ = 256
PK_TOPK = 16
PLE_DIM = 256

DN_QK = DN_HEADS * DN_DK
DN_V = DN_HEADS * DN_DV
CONV_CH = 2 * DN_QK + DN_V
RET_QK = RET_HEADS * RET_DK
RET_V = RET_HEADS * RET_DV
AB_OFF = CONV_CH + DN_V
LANE = 128
MXU_DEPTH = 256
HALO = 8
CONV_HALO = 16

TM_INPROJ = 1024
TN_INPROJ = 1024
TT_SEQ = 256
TM_DENSE = 256
TL_ROUTER = 512
TM_PEER = 512
NI_PEER = 8
TE_PEER = NI_PEER * N_KEYS
VMEM_LIMIT = 56 * 1024 * 1024


def _cparams(sem, flags=None):
    return pltpu.CompilerParams(dimension_semantics=sem, vmem_limit_bytes=VMEM_LIMIT, flags=flags)


def _rms(x, w):
    return x * lax.rsqrt(jnp.mean(x * x, axis=-1, keepdims=True) + EPS) * w


def _silu(x):
    return x * jax.nn.sigmoid(x)


def _softplus(x):
    return jnp.maximum(x, 0.0) + jnp.log1p(jnp.exp(-jnp.abs(x)))


def _gelu(x):
    return 0.5 * x * (1.0 + lax.erf(x * (2.0 ** -0.5)))


def _mm(a, b):
    return jnp.dot(a.astype(BF16), b.astype(BF16), preferred_element_type=F32)


def _mm_nt(a, b):
    return lax.dot_general(a.astype(BF16), b.astype(BF16), (((1,), (1,)), ((), ())),
                           preferred_element_type=F32)


def _mm_tn(a, b):
    return lax.dot_general(a.astype(BF16), b.astype(BF16), (((0,), (0,)), ((), ())),
                           preferred_element_type=F32)


def _rms_matmul(x, norm_w, w_ref, k_chunk=MXU_DEPTH):
    acc = None
    sumsq = None
    for c in range(x.shape[1] // k_chunk):
        cs = slice(c * k_chunk, (c + 1) * k_chunk)
        xc = x[:, cs]
        part = jnp.sum(xc * xc, axis=-1, keepdims=True)
        sumsq = part if sumsq is None else sumsq + part
        prod = jnp.dot((xc * norm_w[:, cs]).astype(BF16), w_ref[cs, :], preferred_element_type=F32)
        acc = prod if acc is None else acc + prod
    inv_rms = lax.rsqrt(sumsq * (1.0 / x.shape[1]) + EPS)
    return acc * inv_rms, inv_rms


def _split3(x):
    hi = x.astype(BF16)
    rest = x - hi.astype(F32)
    mid = rest.astype(BF16)
    lo = (rest - mid.astype(F32)).astype(BF16)
    return hi, mid, lo


def _select_mm(a, b, dims=(((1,), (0,)), ((), ()))):
    if a.dtype == BF16:
        parts = [lax.dot_general(a, p, dims, preferred_element_type=F32) for p in _split3(b)]
    else:
        parts = [lax.dot_general(p, b, dims, preferred_element_type=F32) for p in _split3(a)]
    return parts[0] + parts[1] + parts[2]


def _inproj_body(x_ref, nw_ref, w_head_ref, w_tail_ref, wab_ref, o_ref, oab_ref, hn_ref):
    j = pl.program_id(1)
    n_head = AB_OFF // TN_INPROJ

    @pl.when(j == 0)
    def _():
        hn_ref[...] = _rms(x_ref[...], nw_ref[...]).astype(BF16)
        oab_ref[...] = jnp.dot(hn_ref[...], wab_ref[...], preferred_element_type=F32)

    @pl.when(j < n_head)
    def _():
        o_ref[...] = jnp.dot(hn_ref[...], w_head_ref[...], preferred_element_type=F32).astype(BF16)

    @pl.when(j >= n_head)
    def _():
        o_ref[...] = jnp.dot(hn_ref[...], w_tail_ref[...], preferred_element_type=F32).astype(BF16)


def _inproj(x, nw, w_head, w_tail, w_ab):
    t, d = x.shape
    assert w_head.shape[1] == AB_OFF
    n_head = AB_OFF // TN_INPROJ
    n = AB_OFF + w_tail.shape[1]
    return pl.pallas_call(
        _inproj_body,
        grid=(t // TM_INPROJ, n // TN_INPROJ),
        in_specs=[
            pl.BlockSpec((TM_INPROJ, d), lambda i, j: (i, 0)),
            pl.BlockSpec((1, d), lambda i, j: (0, 0)),
            pl.BlockSpec((d, TN_INPROJ), lambda i, j: (0, jnp.minimum(j, n_head - 1))),
            pl.BlockSpec((d, TN_INPROJ), lambda i, j: (0, jnp.maximum(j - n_head, 0))),
            pl.BlockSpec((d, LANE), lambda i, j: (0, 0)),
        ],
        out_specs=[
            pl.BlockSpec((TM_INPROJ, TN_INPROJ), lambda i, j: (i, j)),
            pl.BlockSpec((TM_INPROJ, LANE), lambda i, j: (i, 0)),
        ],
        out_shape=[jax.ShapeDtypeStruct((t, n), BF16), jax.ShapeDtypeStruct((t, LANE), F32)],
        scratch_shapes=[pltpu.VMEM((TM_INPROJ, d), BF16)],
        compiler_params=_cparams(("parallel", "arbitrary")),
        name="inproj",
    )(x, nw, w_head, w_tail, w_ab)


def _dn_local_body(q_ref, k_ref, v_ref, qh_ref, kh_ref, vh_ref, ab_ref, cw_ref, alog_ref, dtb_ref,
                   selg_ref, selb_ref, wa_ref, wb_ref, wc_ref, qa_ref, kd_ref, u_ref, w_ref, o0_ref,
                   gend_ref, wa16_ref, wb16_ref, wc16_ref, buf_ref):
    tt = TT_SEQ
    first_tile = pl.program_id(1) == 0
    for src, dst in ((wa_ref, wa16_ref), (wb_ref, wb16_ref), (wc_ref, wc16_ref)):
        dst[...] = src[...].astype(BF16)

    def conv_silu(main_ref, halo_ref, c0):
        halo = halo_ref[...].astype(F32)
        buf_ref[0:CONV_HALO, :] = jnp.where(first_tile, jnp.zeros_like(halo), halo)
        buf_ref[CONV_HALO:CONV_HALO + tt, :] = main_ref[...].astype(F32)
        y = None
        for j in range(CONV_W):
            off = CONV_HALO - (CONV_W - 1) + j
            term = buf_ref[off:off + tt, :] * cw_ref[j:j + 1, c0:c0 + DN_QK]
            y = term if y is None else y + term
        return _silu(y)

    q = conv_silu(q_ref, qh_ref, 0)
    k = conv_silu(k_ref, kh_ref, DN_QK)
    v = conv_silu(v_ref, vh_ref, 2 * DN_QK)

    row = lax.broadcasted_iota(jnp.int32, (tt, tt), 0)
    col = lax.broadcasted_iota(jnp.int32, (tt, tt), 1)

    def same_block(bits):
        return (row >> bits) == (col >> bits)

    blk8, blk16, blk32, blk64 = same_block(3), same_block(4), same_block(5), same_block(6)
    causal = blk64 & (col <= row)
    strict = blk64 & (col < row)
    eye = (row == col).astype(F32)

    ab = ab_ref[...]
    g = -jnp.exp(alog_ref[...]) * _softplus(ab + dtb_ref[...])
    beta = jax.nn.sigmoid(ab)
    def ones_where(mask):
        return jnp.where(mask, 1.0, 0.0).astype(BF16)

    gc = _select_mm(ones_where(causal), g)
    gtot = _select_mm(ones_where(blk64), g)
    e_gc = _select_mm(gc, selg_ref[...])
    e_gt = _select_mm(gtot, selg_ref[...])
    e_beta = _select_mm(beta, selb_ref[...])
    r128 = lax.broadcasted_iota(jnp.int32, (LANE, LANE), 0)
    c128 = lax.broadcasted_iota(jnp.int32, (LANE, LANE), 1)
    gc_t = _select_mm(ones_where(r128 == c128), gc, (((1,), (1,)), ((), ())))
    eg = jnp.exp(e_gc)
    kdec = jnp.exp(e_gt - e_gc)

    for c in range(tt // CHUNK):
        gend_ref[0, c:c + 1, :] = jnp.exp(e_gt[c * CHUNK:c * CHUNK + 1, :])
    gend_ref[0, tt // CHUNK:, :] = jnp.zeros((HALO - tt // CHUNK, DN_V), F32)

    heads = range(DN_HEADS)
    hsl = [slice(h * DN_DK, (h + 1) * DN_DK) for h in heads]

    def per_head(fn):
        return [fn(h) for h in heads]

    def l2n(x):
        return x * lax.rsqrt(jnp.sum(x * x, axis=-1, keepdims=True) + EPS)

    qh = per_head(lambda h: l2n(q[:, hsl[h]]) * (DN_DK ** -0.5))
    kh = per_head(lambda h: l2n(k[:, hsl[h]]))
    kb = per_head(lambda h: kh[h] * e_beta[:, hsl[h]])
    dmat = per_head(lambda h: jnp.exp(jnp.where(
        causal, jnp.concatenate([e_gc[:, hsl[h]]] * (tt // LANE), axis=1) - gc_t[h:h + 1, :], NEG_INF)))
    scores = per_head(lambda h: _mm_nt(jnp.concatenate([kb[h], qh[h]], axis=0), kh[h]))
    nmat = per_head(lambda h: jnp.where(strict, scores[h][:tt] * dmat[h], 0.0))
    attn = per_head(lambda h: scores[h][tt:] * dmat[h])

    n0 = per_head(lambda h: jnp.where(blk8, nmat[h], 0.0))
    n2 = per_head(lambda h: _mm(n0[h], n0[h]))
    n4 = per_head(lambda h: _mm(n2[h], n2[h]))
    inv = per_head(lambda h: eye - n0[h])
    inv = per_head(lambda h: inv[h] + _mm(inv[h], n2[h]))
    inv = per_head(lambda h: inv[h] + _mm(inv[h], n4[h]))
    for inner, outer in ((blk8, blk16), (blk16, blk32), (blk32, blk64)):
        level = outer & jnp.logical_not(inner)
        prod = per_head(lambda h: _mm(inv[h], jnp.where(level, nmat[h], 0.0)))
        inv = per_head(lambda h: inv[h] - _mm(prod[h], inv[h]))

    sol = per_head(lambda h: _mm(inv[h], jnp.concatenate(
        [v[:, hsl[h]] * e_beta[:, hsl[h]], kb[h] * eg[:, hsl[h]]], axis=1)))
    asol = per_head(lambda h: _mm(attn[h], sol[h]))
    for h in heads:
        hs = hsl[h]
        u_ref[:, hs] = sol[h][:, :DN_DV].astype(BF16)
        w_ref[:, hs] = sol[h][:, DN_DV:].astype(BF16)
        o0_ref[:, hs] = asol[h][:, :DN_DV].astype(BF16)
        qa_ref[:, hs] = (qh[h] * eg[:, hs] - asol[h][:, DN_DV:]).astype(BF16)
        kd_ref[:, hs] = (kh[h] * kdec[:, hs]).astype(BF16)


def _dn_local(proj, proj_ab, cw, alog_row, dtb_row, selg, selb, later_weights, b, s):
    t = b * s
    nt = s // TT_SEQ
    rows_per_halo = TT_SEQ // CONV_HALO
    w_rows, w_cols = later_weights[0].shape
    assert all(w.shape == (w_rows, w_cols) for w in later_weights) and w_rows % (b * nt) == 0
    slab = pl.BlockSpec((w_rows // (b * nt), w_cols), lambda bi, i: (bi * nt + i, 0))

    def main_spec(cb):
        return pl.BlockSpec((TT_SEQ, DN_QK), lambda bi, i: (bi * nt + i, cb))

    def halo_spec(cb):
        return pl.BlockSpec(
            (CONV_HALO, DN_QK), lambda bi, i: (jnp.maximum((bi * nt + i) * rows_per_halo - 1, 0), cb))

    def full(shape):
        return pl.BlockSpec(shape, lambda bi, i: (0,) * len(shape))

    tok_spec = pl.BlockSpec((TT_SEQ, DN_V), lambda bi, i: (bi * nt + i, 0))
    return pl.pallas_call(
        _dn_local_body,
        grid=(b, nt),
        in_specs=[main_spec(0), main_spec(1), main_spec(2), halo_spec(0), halo_spec(1), halo_spec(2),
                  pl.BlockSpec((TT_SEQ, LANE), lambda bi, i: (bi * nt + i, 0)),
                  full((CONV_W, CONV_CH)), full((1, LANE)), full((1, LANE)),
                  full((LANE, DN_V)), full((LANE, DN_V)), slab, slab, slab],
        out_specs=[tok_spec, tok_spec, tok_spec, tok_spec, tok_spec,
                   pl.BlockSpec((1, HALO, DN_V), lambda bi, i: (bi * nt + i, 0, 0)), slab, slab, slab],
        out_shape=[jax.ShapeDtypeStruct((t, DN_V), BF16),
                   jax.ShapeDtypeStruct((t, DN_V), BF16),
                   jax.ShapeDtypeStruct((t, DN_V), BF16),
                   jax.ShapeDtypeStruct((t, DN_V), BF16),
                   jax.ShapeDtypeStruct((t, DN_V), BF16),
                   jax.ShapeDtypeStruct((t // TT_SEQ, HALO, DN_V), F32)]
                  + [jax.ShapeDtypeStruct((w_rows, w_cols), BF16)] * 3,
        scratch_shapes=[pltpu.VMEM((CONV_HALO + TT_SEQ, DN_QK), F32)],
        compiler_params=_cparams(("parallel", "parallel")),
        name="dn_local",
    )(proj, proj, proj, proj, proj, proj, proj_ab, cw, alog_row, dtb_row, selg, selb, *later_weights)


def _dn_scan_body(qa_ref, kd_ref, u_ref, w_ref, o0_ref, gend_ref, o_ref, s_ref):
    nb = qa_ref.shape[0]

    @pl.when(pl.program_id(0) == 0)
    def _():
        s_ref[...] = jnp.zeros(s_ref.shape, F32)

    chains = [(b, h, slice(h * DN_DK, (h + 1) * DN_DK)) for b in range(nb) for h in range(DN_HEADS)]
    states = [s_ref[n] for n in range(len(chains))]
    for c in range(TT_SEQ // CHUNK):
        rs = slice(c * CHUNK, (c + 1) * CHUNK)
        both = [jnp.dot(jnp.concatenate([w_ref[b, rs, hs], qa_ref[b, rs, hs]], axis=0),
                        states[n].astype(BF16), preferred_element_type=F32)
                for n, (b, h, hs) in enumerate(chains)]
        for n, (b, h, hs) in enumerate(chains):
            o_ref[b, rs, hs] = (both[n][CHUNK:] + o0_ref[b, rs, hs]).astype(o_ref.dtype)
        states = [states[n] * gend_ref[b, 0, c:c + 1, hs]
                  + _mm_tn(kd_ref[b, rs, hs], u_ref[b, rs, hs] - both[n][:CHUNK])
                  for n, (b, h, hs) in enumerate(chains)]
    for n in range(len(chains)):
        s_ref[n] = states[n]


def _dn_scan(qa, kd, u, w, o0, gend, b, s):
    nt = s // TT_SEQ
    tok = lambda a: a.reshape(b, s, DN_V)
    tok_spec = pl.BlockSpec((b, TT_SEQ, DN_V), lambda i: (0, i, 0))
    out = pl.pallas_call(
        _dn_scan_body,
        grid=(nt,),
        in_specs=[tok_spec, tok_spec, tok_spec, tok_spec, tok_spec,
                  pl.BlockSpec((b, 1, HALO, DN_V), lambda i: (0, i, 0, 0))],
        out_specs=tok_spec,
        out_shape=jax.ShapeDtypeStruct((b, s, DN_V), BF16),
        scratch_shapes=[pltpu.VMEM((b * DN_HEADS, DN_DK, DN_DV), F32)],
        compiler_params=_cparams(("arbitrary",)),
        name="dn_scan",
    )(tok(qa), tok(kd), tok(u), tok(w), tok(o0), gend.reshape(b, nt, HALO, DN_V))
    return out.reshape(b * s, DN_V)


def _log_gamma(h):
    return math.log1p(-(2.0 ** (-5.0 - h)))


def _ret_body(qk_ref, v_ref, pos_ref, freq_ref, o_ref, s_ref):
    tt = TT_SEQ

    @pl.when(pl.program_id(1) == 0)
    def _():
        s_ref[...] = jnp.zeros(s_ref.shape, F32)

    ang = pos_ref[...].astype(F32) * freq_ref[...]
    lane = lax.broadcasted_iota(jnp.int32, (tt, LANE), 1)
    first_half = (lane & (RET_DK // 2)) == 0
    cos = jnp.cos(ang)
    sin = jnp.sin(ang)
    ssin = jnp.where(first_half, -sin, sin)

    def rotary(x):
        swapped = jnp.where(first_half, pltpu.roll(x, LANE - RET_DK // 2, axis=1),
                            pltpu.roll(x, RET_DK // 2, axis=1))
        return x * cos + swapped * ssin

    row = lax.broadcasted_iota(jnp.int32, (tt, tt), 0)
    col = lax.broadcasted_iota(jnp.int32, (tt, tt), 1)
    rel = (row - col).astype(F32)
    causal = row >= col
    trow = lax.broadcasted_iota(jnp.int32, (tt, LANE), 0).astype(F32)
    srow = lax.broadcasted_iota(jnp.int32, (LANE, LANE), 0)

    pairs = range(RET_HEADS // 2)
    heads = range(RET_HEADS)
    qp = [rotary(qk_ref[:, pr * LANE:(pr + 1) * LANE].astype(F32)) for pr in pairs]
    kp = [rotary(qk_ref[:, RET_QK + pr * LANE:RET_QK + (pr + 1) * LANE].astype(F32)) * (RET_DK ** -0.5)
          for pr in pairs]
    state = [s_ref[pr] for pr in pairs]
    mine = [(lane < RET_DK) if h % 2 == 0 else (lane >= RET_DK) for h in heads]
    lg = [_log_gamma(h) for h in heads]
    qm = [jnp.where(mine[h], qp[h // 2], 0.0).astype(BF16) for h in heads]
    vh = [v_ref[:, h * RET_DV:(h + 1) * RET_DV].astype(BF16) for h in heads]
    att = [_mm_nt(qm[h], kp[h // 2]) * jnp.exp(jnp.where(causal, rel * lg[h], NEG_INF)) for h in heads]
    inner = [jnp.dot(att[h].astype(BF16), vh[h], preferred_element_type=F32) for h in heads]
    cross = [jnp.dot(qm[h], state[h // 2].astype(BF16), preferred_element_type=F32)
             * jnp.exp((trow + 1.0) * lg[h]) for h in heads]
    for h in heads:
        o_ref[:, h * RET_DV:(h + 1) * RET_DV] = inner[h] + cross[h]
    update = [_mm_tn(jnp.where(mine[h], kp[h // 2], 0.0) * jnp.exp((tt - 1.0 - trow) * lg[h]), vh[h])
              for h in heads]
    for pr in pairs:
        decay = jnp.where(srow < RET_DK, math.exp(tt * lg[2 * pr]), math.exp(tt * lg[2 * pr + 1]))
        s_ref[pr] = state[pr] * decay + update[2 * pr] + update[2 * pr + 1]


def _ret(proj, pos, freq_row, b, s):
    t = b * s
    nt = s // TT_SEQ
    return pl.pallas_call(
        _ret_body,
        grid=(b, nt),
        in_specs=[pl.BlockSpec((TT_SEQ, 2 * RET_QK), lambda bi, i: (bi * nt + i, 4)),
                  pl.BlockSpec((TT_SEQ, RET_V), lambda bi, i: (bi * nt + i, 5)),
                  pl.BlockSpec((TT_SEQ, 1), lambda bi, i: (bi * nt + i, 0)),
                  pl.BlockSpec((1, LANE), lambda bi, i: (0, 0))],
        out_specs=pl.BlockSpec((TT_SEQ, RET_V), lambda bi, i: (bi * nt + i, 0)),
        out_shape=jax.ShapeDtypeStruct((t, RET_V), F32),
        scratch_shapes=[pltpu.VMEM((RET_HEADS // 2, LANE, RET_DV), F32)],
        compiler_params=_cparams(("parallel", "arbitrary")),
        name="ret",
    )(proj, proj, pos, freq_row)


def _outproj_body(odn_ref, z_ref, or_ref, rg_ref, x_ref, dnw_ref, gnw_ref, gnb_ref, w_ref, o_ref):
    def dn_head(h):
        hs = slice(h * DN_DV, (h + 1) * DN_DV)
        return _rms(odn_ref[:, hs].astype(F32), dnw_ref[...]) * _silu(z_ref[:, hs].astype(F32))

    def ret_head(h):
        hs = slice(h * RET_DV, (h + 1) * RET_DV)
        o = or_ref[:, hs]
        cen = o - jnp.mean(o, axis=-1, keepdims=True)
        y = cen * lax.rsqrt(jnp.mean(cen * cen, axis=-1, keepdims=True) + EPS)
        return (y * gnw_ref[:, hs] + gnb_ref[:, hs]) * _silu(rg_ref[:, hs].astype(F32))

    acc = x_ref[...]
    pair = 2 * DN_DV
    heads = [(dn_head, h) for h in range(DN_HEADS)] + [(ret_head, h) for h in range(RET_HEADS)]
    for c in range(len(heads) // 2):
        (f0, h0), (f1, h1) = heads[2 * c], heads[2 * c + 1]
        mix = jnp.concatenate([f0(h0), f1(h1)], axis=1).astype(BF16)
        acc = acc + jnp.dot(mix, w_ref[c * pair:(c + 1) * pair, :], preferred_element_type=F32)
    o_ref[...] = acc


def _outproj(o_dn, proj, o_r, x, dnw, gnw, gnb, w_out):
    t, d = x.shape
    tm = TM_DENSE
    full = lambda shape: pl.BlockSpec(shape, lambda i: (0, 0))
    return pl.pallas_call(
        _outproj_body,
        grid=(t // tm,),
        in_specs=[pl.BlockSpec((tm, DN_V), lambda i: (i, 0)),
                  pl.BlockSpec((tm, DN_V), lambda i: (i, 3)),
                  pl.BlockSpec((tm, RET_V), lambda i: (i, 0)),
                  pl.BlockSpec((tm, RET_V), lambda i: (i, 6)),
                  pl.BlockSpec((tm, d), lambda i: (i, 0)),
                  full((1, DN_DV)), full((1, RET_V)), full((1, RET_V)), full((DN_V + RET_V, d))],
        out_specs=pl.BlockSpec((tm, d), lambda i: (i, 0)),
        out_shape=jax.ShapeDtypeStruct((t, d), F32),
        compiler_params=_cparams(("parallel",)),
        name="outproj",
    )(o_dn, proj, o_r, proj, x, dnw, gnw, gnb, w_out)


def _pq_body(h_ref, nw_ref, w_ref, keys_ref, hnt_ref, inv_ref, sc_ref):
    x = h_ref[...]
    q, inv_rms = _rms_matmul(x, nw_ref[...], w_ref)
    hnt = (x * inv_rms * nw_ref[...]).T
    amax = jnp.maximum(jnp.max(jnp.abs(hnt), axis=0, keepdims=True), F8_TINY)
    hnt_ref[...] = (hnt * (F8_TARGET / amax)).astype(F8)
    inv_ref[...] = amax * (1.0 / F8_TARGET)
    half = PK_DQ // 2
    for hp in range(2 * PK_HEADS):
        sc_ref[hp] = _mm_nt(keys_ref[hp], q[:, hp * half:(hp + 1) * half])


def _pq(h1, nw, w_pq, keys):
    t, d = h1.shape
    tm = TM_DENSE
    nk = 2 * PK_HEADS
    return pl.pallas_call(
        _pq_body,
        grid=(t // tm,),
        in_specs=[pl.BlockSpec((tm, d), lambda i: (i, 0)),
                  pl.BlockSpec((1, d), lambda i: (0, 0)),
                  pl.BlockSpec((d, PK_HEADS * PK_DQ), lambda i: (0, 0)),
                  pl.BlockSpec((nk, N_KEYS, PK_DQ // 2), lambda i: (0, 0, 0))],
        out_specs=[pl.BlockSpec((d, tm), lambda i: (0, i)),
                   pl.BlockSpec((1, tm), lambda i: (0, i)),
                   pl.BlockSpec((nk, N_KEYS, tm), lambda i: (0, 0, i))],
        out_shape=[jax.ShapeDtypeStruct((d, t), F8),
                   jax.ShapeDtypeStruct((1, t), F32),
                   jax.ShapeDtypeStruct((nk, N_KEYS, t), F32)],
        compiler_params=_cparams(("parallel",)),
        name="pq",
    )(h1, nw, w_pq, keys)


def _route(a, bsc, exact):
    tl = a.shape[1]
    k = PK_TOPK
    keyid = lax.broadcasted_iota(jnp.int32, (N_KEYS, tl), 0)
    slot = lax.broadcasted_iota(jnp.int32, (k, tl), 0)

    def extract(s, ids, n_ids):
        m = jnp.max(s, axis=0, keepdims=True)
        hit = s == m
        if exact:
            hit = ids == jnp.min(jnp.where(hit, ids, n_ids), axis=0, keepdims=True)
        return m, hit

    def top_k(s):
        rank = jnp.full((N_KEYS, tl), float(k), F32)
        vals = jnp.zeros((k, tl), F32)
        for r in range(k):
            m, hit = extract(s, keyid, N_KEYS)
            rank = jnp.where(hit, float(r), rank)
            s = jnp.where(hit, NEG_INF, s)
            vals = jnp.where(slot == r, m, vals)
        return vals, rank

    av, rank_a = top_k(a)
    bv, rank_b = top_k(bsc)

    k2 = k // 2
    assert k2 & (k2 - 1) == 0
    n_cand = k + (k - 1) * k2
    cand = jnp.concatenate([av[0:1, :] + bv] + [av[r:r + 1, :] + bv[0:k2, :] for r in range(1, k)],
                           axis=0)
    row = lax.broadcasted_iota(jnp.int32, (n_cand, tl), 0)
    tail = row - k
    cid = jnp.where(row < k, row,
                    (1 + (tail >> (k2.bit_length() - 1))) * k + (tail & (k2 - 1)))
    work = cand
    for _ in range(k):
        _, hit = extract(work, cid, k * k)
        work = jnp.where(hit, NEG_INF, work)
    sel = (work == NEG_INF).astype(F32)
    zsum = jnp.sum(sel * jnp.exp(cand - cand[0:1, :]), axis=0, keepdims=True)

    cnt = jnp.zeros((N_KEYS, tl), F32)
    for r in range(k):
        lo, hi = (0, k) if r == 0 else (k + (r - 1) * k2, k + r * k2)
        cnt_r = jnp.sum(sel[lo:hi, :], axis=0, keepdims=True)
        cnt = jnp.where(rank_a == float(r), cnt_r, cnt)

    def full_count(x):
        return jnp.sum(x, axis=0, keepdims=True) == float(k)

    ok = (full_count((rank_a < float(k)).astype(F32)) & full_count((rank_b < float(k)).astype(F32))
          & full_count(sel))
    eb = jnp.exp(bsc - bv[0:1, :])
    ea = jnp.exp(a - av[0:1, :]) / zsum
    return rank_b, eb, cnt, ea, ok


def _router_body(sc_ref, u_ref, v_ref, rankb_ref, eb_ref, cnt_ref, ea_ref, u8_ref, uinv_ref, vt_ref):
    u = u_ref[...]
    amax = jnp.max(jnp.max(jnp.abs(u), axis=0, keepdims=True), axis=1, keepdims=True)
    amax = jnp.maximum(amax, F8_TINY)
    u8_ref[...] = (u * (F8_TARGET / amax)).astype(F8)
    uinv_ref[0] = jnp.broadcast_to(amax * (1.0 / F8_TARGET), uinv_ref.shape[1:])
    vt_ref[...] = v_ref[...].T.astype(BF16)

    def run(exact):
        rank_b, eb, cnt, ea, ok = _route(sc_ref[0], sc_ref[1], exact)
        rankb_ref[0] = rank_b.astype(BF16)
        eb_ref[0] = eb.astype(BF16)
        cnt_ref[0] = cnt
        ea_ref[0] = ea
        return ok

    ok = run(exact=False)
    n_bad = jnp.sum(jnp.where(ok, 0.0, 1.0), axis=1, keepdims=True)

    @pl.when(n_bad[0, 0] > 0.0)
    def _():
        run(exact=True)


def _router(scores, expert_u, expert_v):
    nk, n, t = scores.shape
    ne, d = expert_u.shape
    tl = TL_ROUTER
    nt = t // tl
    assert PK_HEADS * nt * N_KEYS == ne, "one 128-expert block is prepared per routing step"
    spec = pl.BlockSpec((1, n, tl), lambda h, j: (h, 0, j))
    shape = lambda dt: jax.ShapeDtypeStruct((PK_HEADS, n, t), dt)
    blk = lambda h, j: h * nt + j
    return pl.pallas_call(
        _router_body,
        grid=(PK_HEADS, nt),
        in_specs=[pl.BlockSpec((2, n, tl), lambda h, j: (h, 0, j)),
                  pl.BlockSpec((N_KEYS, d), lambda h, j: (blk(h, j), 0)),
                  pl.BlockSpec((N_KEYS, d), lambda h, j: (blk(h, j), 0))],
        out_specs=[spec, spec, spec, spec,
                   pl.BlockSpec((N_KEYS, d), lambda h, j: (blk(h, j), 0)),
                   pl.BlockSpec((1, 1, TM_PEER), lambda h, j: (blk(h, j), 0, 0)),
                   pl.BlockSpec((d, N_KEYS), lambda h, j: (0, blk(h, j)))],
        out_shape=[shape(BF16), shape(BF16), shape(F32), shape(F32),
                   jax.ShapeDtypeStruct((ne, d), F8),
                   jax.ShapeDtypeStruct((ne // N_KEYS, 1, TM_PEER), F32),
                   jax.ShapeDtypeStruct((d, ne), BF16)],
        compiler_params=_cparams(("parallel", "parallel")),
        name="router",
    )(scores, expert_u, expert_v)


def _peer_pre_body(u_ref, hnt_ref, o_ref):
    o_ref[...] = jnp.dot(u_ref[...], hnt_ref[...], preferred_element_type=F32)


def _peer_step(pre_ref, pre_next_ref, hnt_ref, u_ref, vt_ref, rankb_ref, eb_ref, cnt_ref, ea_ref,
               xinv_ref, uinv_ref, acc_ref, act_ref):
    k_piece = D_MODEL // NI_PEER
    pre_next = None
    for il in range(NI_PEER):
        rs = slice(il * N_KEYS, (il + 1) * N_KEYS)
        gate = jnp.zeros((N_KEYS, TM_PEER), BF16)
        for h in range(PK_HEADS):
            picked = rankb_ref[h] < cnt_ref[h, il:il + 1, :].astype(BF16)
            gate = gate + jnp.where(picked, eb_ref[h], 0.0) * ea_ref[h, il:il + 1, :].astype(BF16)
        unscale = xinv_ref[...] * uinv_ref[il]
        act_ref[rs, :] = _gelu(pre_ref[rs, :] * unscale).astype(BF16) * gate
        ks = slice(il * k_piece, (il + 1) * k_piece)
        zero = jnp.minimum(gate[:1, :], 0.0)
        x_piece = (hnt_ref[ks, :].astype(BF16) + zero).astype(F8)
        part = jnp.dot(u_ref[:, ks], x_piece, preferred_element_type=F32)
        pre_next = part if pre_next is None else pre_next + part
    pre_next_ref[...] = pre_next
    acc_ref[...] += jnp.dot(vt_ref[...], act_ref[...], preferred_element_type=F32)


def _peer_body(pre0_ref, hnt_ref, u_ref, vt_ref, rankb_ref, eb_ref, cnt_ref, ea_ref, xinv_ref,
               uinv_ref, o_ref, acc_ref, act_ref, pre_a_ref, pre_b_ref):
    i = pl.program_id(0)
    j = pl.program_id(1)
    args = (hnt_ref, u_ref, vt_ref, rankb_ref, eb_ref, cnt_ref, ea_ref, xinv_ref, uinv_ref, acc_ref,
            act_ref)

    @pl.when((i == 0) & (j == 0))
    def _():
        pre_a_ref[...] = pre0_ref[...]

    @pl.when(j == 0)
    def _():
        acc_ref[...] = jnp.zeros(acc_ref.shape, F32)

    @pl.when(lax.rem(j, 2) == 0)
    def _():
        _peer_step(pre_a_ref, pre_b_ref, *args)

    @pl.when(lax.rem(j, 2) == 1)
    def _():
        _peer_step(pre_b_ref, pre_a_ref, *args)

    @pl.when(j == pl.num_programs(1) - 1)
    def _():
        o_ref[...] = acc_ref[...].T


def _peer(hnt, xinv, u_f8, uinv, vt_bf, rankb, eb, cnt, ea):
    d, t = hnt.shape
    ne = u_f8.shape[0]
    tm, te = TM_PEER, TE_PEER
    ni, nj = t // tm, ne // te
    assert nj % 2 == 0, "pre-activation buffers alternate with the expert-tile index"
    pre0 = pl.pallas_call(
        _peer_pre_body,
        grid=(1,),
        in_specs=[pl.BlockSpec((te, d), lambda i: (0, 0)), pl.BlockSpec((d, tm), lambda i: (0, 0))],
        out_specs=pl.BlockSpec((te, tm), lambda i: (0, 0)),
        out_shape=jax.ShapeDtypeStruct((te, tm), F32),
        compiler_params=_cparams(("arbitrary",)),
        name="peer_pre",
    )(u_f8, hnt)

    def next_i(i, j):
        return jnp.minimum(i + (j + 1) // nj, ni - 1)

    key_spec = pl.BlockSpec((PK_HEADS, N_KEYS, tm), lambda i, j: (0, 0, i))
    blk_spec = pl.BlockSpec((PK_HEADS, NI_PEER, tm), lambda i, j: (0, j, i))
    return pl.pallas_call(
        _peer_body,
        grid=(ni, nj),
        in_specs=[pl.BlockSpec((te, tm), lambda i, j: (0, 0)),
                  pl.BlockSpec((d, tm), lambda i, j: (0, next_i(i, j))),
                  pl.BlockSpec((te, d), lambda i, j: ((j + 1) % nj, 0)),
                  pl.BlockSpec((d, te), lambda i, j: (0, j)),
                  key_spec, key_spec, blk_spec, blk_spec,
                  pl.BlockSpec((1, tm), lambda i, j: (0, i)),
                  pl.BlockSpec((NI_PEER, 1, tm), lambda i, j: (j, 0, 0))],
        out_specs=pl.BlockSpec((tm, d), lambda i, j: (i, 0)),
        out_shape=jax.ShapeDtypeStruct((t, d), F32),
        scratch_shapes=[pltpu.VMEM((d, tm), F32), pltpu.VMEM((te, tm), BF16),
                        pltpu.VMEM((te, tm), F32), pltpu.VMEM((te, tm), F32)],
        compiler_params=_cparams(("arbitrary", "arbitrary")),
        name="peer",
    )(pre0, hnt, u_f8, vt_bf, rankb, eb, cnt, ea, xinv, uinv)


def _ple_body(h_ref, po_ref, p_ref, nple_ref, wg_ref, wp_ref, nfin_ref, o_ref):
    h2 = h_ref[...] + po_ref[...]
    gate = jax.nn.sigmoid(
        jnp.dot(_rms(h2, nple_ref[...]).astype(BF16), wg_ref[...], preferred_element_type=F32))
    ple = jnp.dot(p_ref[...].astype(BF16), wp_ref[...], preferred_element_type=F32)
    o_ref[...] = _rms(h2 + gate * ple, nfin_ref[...])


def _ple(h1, peer_out, p, nple, wg, wp, nfin):
    t, d = h1.shape
    tm = TM_DENSE
    full = lambda shape: pl.BlockSpec(shape, lambda i: (0, 0))
    return pl.pallas_call(
        _ple_body,
        grid=(t // tm,),
        in_specs=[pl.BlockSpec((tm, d), lambda i: (i, 0)),
                  pl.BlockSpec((tm, d), lambda i: (i, 0)),
                  pl.BlockSpec((tm, PLE_DIM), lambda i: (i, 0)),
                  full((1, d)), full((d, d)), full((PLE_DIM, d)), full((1, d))],
        out_specs=pl.BlockSpec((tm, d), lambda i: (i, 0)),
        out_shape=jax.ShapeDtypeStruct((t, d), F32),
        compiler_params=_cparams(("parallel",)),
        name="ple",
    )(h1, peer_out, p, nple, wg, wp, nfin)


def _lane_row(vec, offset):
    return jnp.zeros((1, LANE), F32).at[0, offset:offset + vec.shape[0]].set(vec.astype(F32))


def _head_selector(offset):
    sel = np.zeros((LANE, DN_V), np.float32)
    for h in range(DN_HEADS):
        sel[offset + h, h * DN_DV:(h + 1) * DN_DV] = 1.0
    return jnp.asarray(sel, dtype=BF16)


def kernel(x, p, positions, norm_mix, w_in, conv_w, a_log, dt_bias, dn_norm, ret_gn_w, ret_gn_b,
           w_out, norm_ffn, w_pq, sub_keys, expert_u, expert_v, norm_ple, w_ple_gate, w_ple_proj,
           norm_final):
    b, s, d = x.shape
    t = b * s
    depth = w_in.shape[0]
    assert depth == 1, "the final rms_norm is fused into the single layer's ple kernel"
    half = RET_DK // 2
    inv_freq = ROPE_BASE ** (-jnp.arange(half, dtype=F32) / half)
    freq_row = jnp.tile(inv_freq, LANE // half).reshape(1, LANE)
    pos = positions.reshape(t, 1)
    selg, selb = _head_selector(0), _head_selector(DN_HEADS)

    h = x.reshape(t, d)
    for i in range(depth):
        ab_end = AB_OFF + 2 * DN_HEADS
        w_ab = jnp.pad(w_in[i][:, AB_OFF:ab_end].astype(BF16), ((0, 0), (0, LANE - 2 * DN_HEADS)))
        proj, proj_ab = _inproj(h, norm_mix[i].reshape(1, d), w_in[i][:, :AB_OFF].astype(BF16),
                                w_in[i][:, ab_end:].astype(BF16), w_ab)

        qa, kd, u, wv, o0, gend, w_out16, w_pq16, w_gate16 = _dn_local(
            proj, proj_ab, conv_w[i].astype(F32).T, _lane_row(a_log[i], 0), _lane_row(dt_bias[i], 0),
            selg, selb, (w_out[i], w_pq[i], w_ple_gate[i]), b, s)
        o_dn = _dn_scan(qa, kd, u, wv, o0, gend, b, s)
        o_r = _ret(proj, pos, freq_row, b, s)
        h1 = _outproj(o_dn, proj, o_r, h, dn_norm[i].reshape(1, DN_DV), ret_gn_w[i].reshape(1, RET_V),
                      ret_gn_b[i].reshape(1, RET_V), w_out16)

        keys = sub_keys[i].reshape(2 * PK_HEADS, N_KEYS, PK_DQ // 2).astype(BF16)
        hnt, xinv, scores = _pq(h1, norm_ffn[i].reshape(1, d), w_pq16, keys)
        rankb, eb, cnt, ea, u_f8, uinv, vt_bf = _router(scores, expert_u[i], expert_v[i])
        peer_out = _peer(hnt, xinv, u_f8, uinv, vt_bf, rankb, eb, cnt, ea)

        h = _ple(h1, peer_out, p[i].reshape(t, PLE_DIM), norm_ple[i].reshape(1, d),
                 w_gate16, w_ple_proj[i].astype(BF16), norm_final.reshape(1, d))
    return h.reshape(b, s, d)
```

```python
import math

import numpy as np
import jax
import jax.numpy as jnp
from jax import lax
from jax.experimental import pallas as pl
from jax.experimental.pallas import tpu as pltpu

F32 = jnp.float32
BF16 = jnp.bfloat16
F8 = jnp.float8_e4m3fn
F8_TARGET = 224.0
F8_TINY = 1e-30
NEG_INF = float("-inf")

EPS = 1e-6
D_MODEL = 2048
DN_HEADS = 8
DN_DK = 128
DN_DV = 128
CONV_W = 4
RET_HEADS = 8
RET_DK = 64
RET_DV = 128
CHUNK = 64
ROPE_BASE = 10000.0
N_KEYS = 128
PK_HEADS = 8
PK_DQ = 256
PK_TOPK = 16
PLE_DIM = 256

DN_QK = DN_HEADS * DN_DK
DN_V = DN_HEADS * DN_DV
CONV_CH = 2 * DN_QK + DN_V
RET_QK = RET_HEADS * RET_DK
RET_V = RET_HEADS * RET_DV
AB_OFF = CONV_CH + DN_V
LANE = 128
MXU_DEPTH = 256
HALO = 8
CONV_HALO = 16

TM_INPROJ = 1024
TN_INPROJ = 1024
TT_SEQ = 256
TM_DENSE = 256
TL_ROUTER = 512
TM_PEER = 512
NI_PEER = 8
TE_PEER = NI_PEER * N_KEYS
VMEM_LIMIT = 56 * 1024 * 1024


def _cparams(sem, flags=None):
    return pltpu.CompilerParams(dimension_semantics=sem, vmem_limit_bytes=VMEM_LIMIT, flags=flags)


def _rms(x, w):
    return x * lax.rsqrt(jnp.mean(x * x, axis=-1, keepdims=True) + EPS) * w


def _silu(x):
    return x * jax.nn.sigmoid(x)


def _softplus(x):
    return jnp.maximum(x, 0.0) + jnp.log1p(jnp.exp(-jnp.abs(x)))


def _gelu(x):
    return 0.5 * x * (1.0 + lax.erf(x * (2.0 ** -0.5)))


def _mm(a, b):
    return jnp.dot(a.astype(BF16), b.astype(BF16), preferred_element_type=F32)


def _mm_nt(a, b):
    return lax.dot_general(a.astype(BF16), b.astype(BF16), (((1,), (1,)), ((), ())),
                           preferred_element_type=F32)


def _mm_tn(a, b):
    return lax.dot_general(a.astype(BF16), b.astype(BF16), (((0,), (0,)), ((), ())),
                           preferred_element_type=F32)


def _rms_matmul(x, norm_w, w_ref, k_chunk=MXU_DEPTH):
    acc = None
    sumsq = None
    for c in range(x.shape[1] // k_chunk):
        cs = slice(c * k_chunk, (c + 1) * k_chunk)
        xc = x[:, cs]
        part = jnp.sum(xc * xc, axis=-1, keepdims=True)
        sumsq = part if sumsq is None else sumsq + part
        prod = jnp.dot((xc * norm_w[:, cs]).astype(BF16), w_ref[cs, :], preferred_element_type=F32)
        acc = prod if acc is None else acc + prod
    inv_rms = lax.rsqrt(sumsq * (1.0 / x.shape[1]) + EPS)
    return acc * inv_rms, inv_rms


def _split3(x):
    hi = x.astype(BF16)
    rest = x - hi.astype(F32)
    mid = rest.astype(BF16)
    lo = (rest - mid.astype(F32)).astype(BF16)
    return hi, mid, lo


def _select_mm(a, b, dims=(((1,), (0,)), ((), ()))):
    if a.dtype == BF16:
        parts = [lax.dot_general(a, p, dims, preferred_element_type=F32) for p in _split3(b)]
    else:
        parts = [lax.dot_general(p, b, dims, preferred_element_type=F32) for p in _split3(a)]
    return parts[0] + parts[1] + parts[2]


def _inproj_body(x_ref, nw_ref, w_head_ref, w_tail_ref, wab_ref, o_ref, oab_ref, hn_ref):
    j = pl.program_id(1)
    n_head = AB_OFF // TN_INPROJ

    @pl.when(j == 0)
    def _():
        hn_ref[...] = _rms(x_ref[...], nw_ref[...]).astype(BF16)
        oab_ref[...] = jnp.dot(hn_ref[...], wab_ref[...], preferred_element_type=F32)

    @pl.when(j < n_head)
    def _():
        o_ref[...] = jnp.dot(hn_ref[...], w_head_ref[...], preferred_element_type=F32).astype(BF16)

    @pl.when(j >= n_head)
    def _():
        o_ref[...] = jnp.dot(hn_ref[...], w_tail_ref[...], preferred_element_type=F32).astype(BF16)


def _inproj(x, nw, w_all, w_tail, w_ab):
    t, d = x.shape
    n_head = AB_OFF // TN_INPROJ
    n = AB_OFF + w_tail.shape[1]
    return pl.pallas_call(
        _inproj_body,
        grid=(t // TM_INPROJ, n // TN_INPROJ),
        in_specs=[
            pl.BlockSpec((TM_INPROJ, d), lambda i, j: (i, 0)),
            pl.BlockSpec((1, d), lambda i, j: (0, 0)),
            pl.BlockSpec((d, TN_INPROJ), lambda i, j: (0, jnp.minimum(j, n_head - 1))),
            pl.BlockSpec((d, TN_INPROJ), lambda i, j: (0, jnp.maximum(j - n_head, 0))),
            pl.BlockSpec((d, LANE), lambda i, j: (0, 0)),
        ],
        out_specs=[
            pl.BlockSpec((TM_INPROJ, TN_INPROJ), lambda i, j: (i, j)),
            pl.BlockSpec((TM_INPROJ, LANE), lambda i, j: (i, 0)),
        ],
        out_shape=[jax.ShapeDtypeStruct((t, n), BF16), jax.ShapeDtypeStruct((t, LANE), F32)],
        scratch_shapes=[pltpu.VMEM((TM_INPROJ, d), BF16)],
        compiler_params=_cparams(("parallel", "arbitrary")),
        name="inproj",
    )(x, nw, w_all, w_tail, w_ab)


def _dn_local_body(q_ref, k_ref, v_ref, qh_ref, kh_ref, vh_ref, ab_ref, cw_ref, alog_ref, dtb_ref,
                   selg_ref, selb_ref, wa_ref, wb_ref, wc_ref, qa_ref, kd_ref, u_ref, w_ref, o0_ref,
                   gend_ref, wa16_ref, wb16_ref, wc16_ref, buf_ref):
    tt = TT_SEQ
    first_tile = pl.program_id(1) == 0
    for src, dst in ((wa_ref, wa16_ref), (wb_ref, wb16_ref), (wc_ref, wc16_ref)):
        dst[...] = src[...].astype(BF16)

    def conv_silu(main_ref, halo_ref, c0):
        halo = halo_ref[...].astype(F32)
        buf_ref[0:CONV_HALO, :] = jnp.where(first_tile, jnp.zeros_like(halo), halo)
        buf_ref[CONV_HALO:CONV_HALO + tt, :] = main_ref[...].astype(F32)
        y = None
        for j in range(CONV_W):
            off = CONV_HALO - (CONV_W - 1) + j
            term = buf_ref[off:off + tt, :] * cw_ref[j:j + 1, c0:c0 + DN_QK]
            y = term if y is None else y + term
        return _silu(y)

    q = conv_silu(q_ref, qh_ref, 0)
    k = conv_silu(k_ref, kh_ref, DN_QK)
    v = conv_silu(v_ref, vh_ref, 2 * DN_QK)

    row = lax.broadcasted_iota(jnp.int32, (tt, tt), 0)
    col = lax.broadcasted_iota(jnp.int32, (tt, tt), 1)

    def same_block(bits):
        return (row >> bits) == (col >> bits)

    blk8, blk16, blk32, blk64 = same_block(3), same_block(4), same_block(5), same_block(6)
    causal = blk64 & (col <= row)
    strict = blk64 & (col < row)
    eye = (row == col).astype(F32)

    ab = ab_ref[...]
    g = -jnp.exp(alog_ref[...]) * _softplus(ab + dtb_ref[...])
    beta = jax.nn.sigmoid(ab)
    def ones_where(mask):
        return jnp.where(mask, 1.0, 0.0).astype(BF16)

    gc = _select_mm(ones_where(causal), g)
    gtot = _select_mm(ones_where(blk64), g)
    e_gc = _select_mm(gc, selg_ref[...])
    e_gt = _select_mm(gtot, selg_ref[...])
    e_beta = _select_mm(beta, selb_ref[...])
    r128 = lax.broadcasted_iota(jnp.int32, (LANE, LANE), 0)
    c128 = lax.broadcasted_iota(jnp.int32, (LANE, LANE), 1)
    gc_t = _select_mm(ones_where(r128 == c128), gc, (((1,), (1,)), ((), ())))
    eg = jnp.exp(e_gc)
    kdec = jnp.exp(e_gt - e_gc)

    for c in range(tt // CHUNK):
        gend_ref[0, c:c + 1, :] = jnp.exp(e_gt[c * CHUNK:c * CHUNK + 1, :])
    gend_ref[0, tt // CHUNK:, :] = jnp.zeros((HALO - tt // CHUNK, DN_V), F32)

    heads = range(DN_HEADS)
    hsl = [slice(h * DN_DK, (h + 1) * DN_DK) for h in heads]

    def per_head(fn):
        return [fn(h) for h in heads]

    def l2n(x):
        return x * lax.rsqrt(jnp.sum(x * x, axis=-1, keepdims=True) + EPS)

    qh = per_head(lambda h: l2n(q[:, hsl[h]]) * (DN_DK ** -0.5))
    kh = per_head(lambda h: l2n(k[:, hsl[h]]))
    kb = per_head(lambda h: kh[h] * e_beta[:, hsl[h]])
    dmat = per_head(lambda h: jnp.exp(jnp.where(
        causal, jnp.concatenate([e_gc[:, hsl[h]]] * (tt // LANE), axis=1) - gc_t[h:h + 1, :], NEG_INF)))
    scores = per_head(lambda h: _mm_nt(jnp.concatenate([kb[h], qh[h]], axis=0), kh[h]))
    nmat = per_head(lambda h: jnp.where(strict, scores[h][:tt] * dmat[h], 0.0))
    attn = per_head(lambda h: scores[h][tt:] * dmat[h])

    n0 = per_head(lambda h: jnp.where(blk8, nmat[h], 0.0))
    n2 = per_head(lambda h: _mm(n0[h], n0[h]))
    n4 = per_head(lambda h: _mm(n2[h], n2[h]))
    inv = per_head(lambda h: eye - n0[h])
    inv = per_head(lambda h: inv[h] + _mm(inv[h], n2[h]))
    inv = per_head(lambda h: inv[h] + _mm(inv[h], n4[h]))
    for inner, outer in ((blk8, blk16), (blk16, blk32), (blk32, blk64)):
        level = outer & jnp.logical_not(inner)
        prod = per_head(lambda h: _mm(inv[h], jnp.where(level, nmat[h], 0.0)))
        inv = per_head(lambda h: inv[h] - _mm(prod[h], inv[h]))

    sol = per_head(lambda h: _mm(inv[h], jnp.concatenate(
        [v[:, hsl[h]] * e_beta[:, hsl[h]], kb[h] * eg[:, hsl[h]]], axis=1)))
    asol = per_head(lambda h: _mm(attn[h], sol[h]))
    for h in heads:
        hs = hsl[h]
        u_ref[:, hs] = sol[h][:, :DN_DV].astype(BF16)
        w_ref[:, hs] = sol[h][:, DN_DV:].astype(BF16)
        o0_ref[:, hs] = asol[h][:, :DN_DV].astype(BF16)
        qa_ref[:, hs] = (qh[h] * eg[:, hs] - asol[h][:, DN_DV:]).astype(BF16)
        kd_ref[:, hs] = (kh[h] * kdec[:, hs]).astype(BF16)


def _dn_local(proj, proj_ab, cw, alog_row, dtb_row, selg, selb, later_weights, b, s):
    t = b * s
    nt = s // TT_SEQ
    rows_per_halo = TT_SEQ // CONV_HALO
    w_rows, w_cols = later_weights[0].shape
    assert all(w.shape == (w_rows, w_cols) for w in later_weights) and w_rows % (b * nt) == 0
    slab = pl.BlockSpec((w_rows // (b * nt), w_cols), lambda bi, i: (bi * nt + i, 0))

    def main_spec(cb):
        return pl.BlockSpec((TT_SEQ, DN_QK), lambda bi, i: (bi * nt + i, cb))

    def halo_spec(cb):
        return pl.BlockSpec(
            (CONV_HALO, DN_QK), lambda bi, i: (jnp.maximum((bi * nt + i) * rows_per_halo - 1, 0), cb))

    def full(shape):
        return pl.BlockSpec(shape, lambda bi, i: (0,) * len(shape))

    tok_spec = pl.BlockSpec((TT_SEQ, DN_V), lambda bi, i: (bi * nt + i, 0))
    return pl.pallas_call(
        _dn_local_body,
        grid=(b, nt),
        in_specs=[main_spec(0), main_spec(1), main_spec(2), halo_spec(0), halo_spec(1), halo_spec(2),
                  pl.BlockSpec((TT_SEQ, LANE), lambda bi, i: (bi * nt + i, 0)),
                  full((CONV_W, CONV_CH)), full((1, LANE)), full((1, LANE)),
                  full((LANE, DN_V)), full((LANE, DN_V)), slab, slab, slab],
        out_specs=[tok_spec, tok_spec, tok_spec, tok_spec, tok_spec,
                   pl.BlockSpec((1, HALO, DN_V), lambda bi, i: (bi * nt + i, 0, 0)), slab, slab, slab],
        out_shape=[jax.ShapeDtypeStruct((t, DN_V), BF16),
                   jax.ShapeDtypeStruct((t, DN_V), BF16),
                   jax.ShapeDtypeStruct((t, DN_V), BF16),
                   jax.ShapeDtypeStruct((t, DN_V), BF16),
                   jax.ShapeDtypeStruct((t, DN_V), BF16),
                   jax.ShapeDtypeStruct((t // TT_SEQ, HALO, DN_V), F32)]
                  + [jax.ShapeDtypeStruct((w_rows, w_cols), BF16)] * 3,
        scratch_shapes=[pltpu.VMEM((CONV_HALO + TT_SEQ, DN_QK), F32)],
        compiler_params=_cparams(("parallel", "parallel")),
        name="dn_local",
    )(proj, proj, proj, proj, proj, proj, proj_ab, cw, alog_row, dtb_row, selg, selb, *later_weights)


def _dn_scan_body(qa_ref, kd_ref, u_ref, w_ref, o0_ref, gend_ref, o_ref, s_ref):
    nb = qa_ref.shape[0]

    @pl.when(pl.program_id(0) == 0)
    def _():
        s_ref[...] = jnp.zeros(s_ref.shape, F32)

    chains = [(b, h, slice(h * DN_DK, (h + 1) * DN_DK)) for b in range(nb) for h in range(DN_HEADS)]
    states = [s_ref[n] for n in range(len(chains))]
    for c in range(TT_SEQ // CHUNK):
        rs = slice(c * CHUNK, (c + 1) * CHUNK)
        both = [jnp.dot(jnp.concatenate([w_ref[b, rs, hs], qa_ref[b, rs, hs]], axis=0),
                        states[n].astype(BF16), preferred_element_type=F32)
                for n, (b, h, hs) in enumerate(chains)]
        for n, (b, h, hs) in enumerate(chains):
            o_ref[b, rs, hs] = (both[n][CHUNK:] + o0_ref[b, rs, hs]).astype(o_ref.dtype)
        states = [states[n] * gend_ref[b, 0, c:c + 1, hs]
                  + _mm_tn(kd_ref[b, rs, hs], u_ref[b, rs, hs] - both[n][:CHUNK])
                  for n, (b, h, hs) in enumerate(chains)]
    for n in range(len(chains)):
        s_ref[n] = states[n]


def _dn_scan(qa, kd, u, w, o0, gend, b, s):
    nt = s // TT_SEQ
    tok = lambda a: a.reshape(b, s, DN_V)
    tok_spec = pl.BlockSpec((b, TT_SEQ, DN_V), lambda i: (0, i, 0))
    out = pl.pallas_call(
        _dn_scan_body,
        grid=(nt,),
        in_specs=[tok_spec, tok_spec, tok_spec, tok_spec, tok_spec,
                  pl.BlockSpec((b, 1, HALO, DN_V), lambda i: (0, i, 0, 0))],
        out_specs=tok_spec,
        out_shape=jax.ShapeDtypeStruct((b, s, DN_V), BF16),
        scratch_shapes=[pltpu.VMEM((b * DN_HEADS, DN_DK, DN_DV), F32)],
        compiler_params=_cparams(("arbitrary",)),
        name="dn_scan",
    )(tok(qa), tok(kd), tok(u), tok(w), tok(o0), gend.reshape(b, nt, HALO, DN_V))
    return out.reshape(b * s, DN_V)


def _log_gamma(h):
    return math.log1p(-(2.0 ** (-5.0 - h)))


def _ret_body(qk_ref, v_ref, pos_ref, freq_ref, o_ref, s_ref):
    tt = TT_SEQ

    @pl.when(pl.program_id(1) == 0)
    def _():
        s_ref[...] = jnp.zeros(s_ref.shape, F32)

    ang = pos_ref[...].astype(F32) * freq_ref[...]
    lane = lax.broadcasted_iota(jnp.int32, (tt, LANE), 1)
    first_half = (lane & (RET_DK // 2)) == 0
    cos = jnp.cos(ang)
    sin = jnp.sin(ang)
    ssin = jnp.where(first_half, -sin, sin)

    def rotary(x):
        swapped = jnp.where(first_half, pltpu.roll(x, LANE - RET_DK // 2, axis=1),
                            pltpu.roll(x, RET_DK // 2, axis=1))
        return x * cos + swapped * ssin

    row = lax.broadcasted_iota(jnp.int32, (tt, tt), 0)
    col = lax.broadcasted_iota(jnp.int32, (tt, tt), 1)
    rel = (row - col).astype(F32)
    causal = row >= col
    trow = lax.broadcasted_iota(jnp.int32, (tt, LANE), 0).astype(F32)
    srow = lax.broadcasted_iota(jnp.int32, (LANE, LANE), 0)

    pairs = range(RET_HEADS // 2)
    heads = range(RET_HEADS)
    qp = [rotary(qk_ref[:, pr * LANE:(pr + 1) * LANE].astype(F32)) for pr in pairs]
    kp = [rotary(qk_ref[:, RET_QK + pr * LANE:RET_QK + (pr + 1) * LANE].astype(F32)) * (RET_DK ** -0.5)
          for pr in pairs]
    state = [s_ref[pr] for pr in pairs]
    mine = [(lane < RET_DK) if h % 2 == 0 else (lane >= RET_DK) for h in heads]
    lg = [_log_gamma(h) for h in heads]
    qm = [jnp.where(mine[h], qp[h // 2], 0.0).astype(BF16) for h in heads]
    vh = [v_ref[:, h * RET_DV:(h + 1) * RET_DV].astype(BF16) for h in heads]
    att = [_mm_nt(qm[h], kp[h // 2]) * jnp.exp(jnp.where(causal, rel * lg[h], NEG_INF)) for h in heads]
    inner = [jnp.dot(att[h].astype(BF16), vh[h], preferred_element_type=F32) for h in heads]
    cross = [jnp.dot(qm[h], state[h // 2].astype(BF16), preferred_element_type=F32)
             * jnp.exp((trow + 1.0) * lg[h]) for h in heads]
    for h in heads:
        o_ref[:, h * RET_DV:(h + 1) * RET_DV] = inner[h] + cross[h]
    update = [_mm_tn(jnp.where(mine[h], kp[h // 2], 0.0) * jnp.exp((tt - 1.0 - trow) * lg[h]), vh[h])
              for h in heads]
    for pr in pairs:
        decay = jnp.where(srow < RET_DK, math.exp(tt * lg[2 * pr]), math.exp(tt * lg[2 * pr + 1]))
        s_ref[pr] = state[pr] * decay + update[2 * pr] + update[2 * pr + 1]


def _ret(proj, pos, freq_row, b, s):
    t = b * s
    nt = s // TT_SEQ
    return pl.pallas_call(
        _ret_body,
        grid=(b, nt),
        in_specs=[pl.BlockSpec((TT_SEQ, 2 * RET_QK), lambda bi, i: (bi * nt + i, 4)),
                  pl.BlockSpec((TT_SEQ, RET_V), lambda bi, i: (bi * nt + i, 5)),
                  pl.BlockSpec((TT_SEQ, 1), lambda bi, i: (bi * nt + i, 0)),
                  pl.BlockSpec((1, LANE), lambda bi, i: (0, 0))],
        out_specs=pl.BlockSpec((TT_SEQ, RET_V), lambda bi, i: (bi * nt + i, 0)),
        out_shape=jax.ShapeDtypeStruct((t, RET_V), F32),
        scratch_shapes=[pltpu.VMEM((RET_HEADS // 2, LANE, RET_DV), F32)],
        compiler_params=_cparams(("parallel", "arbitrary")),
        name="ret",
    )(proj, proj, pos, freq_row)


def _outproj_body(odn_ref, z_ref, or_ref, rg_ref, x_ref, dnw_ref, gnw_ref, gnb_ref, w_ref, o_ref):
    def dn_head(h):
        hs = slice(h * DN_DV, (h + 1) * DN_DV)
        return _rms(odn_ref[:, hs].astype(F32), dnw_ref[...]) * _silu(z_ref[:, hs].astype(F32))

    def ret_head(h):
        hs = slice(h * RET_DV, (h + 1) * RET_DV)
        o = or_ref[:, hs]
        cen = o - jnp.mean(o, axis=-1, keepdims=True)
        y = cen * lax.rsqrt(jnp.mean(cen * cen, axis=-1, keepdims=True) + EPS)
        return (y * gnw_ref[:, hs] + gnb_ref[:, hs]) * _silu(rg_ref[:, hs].astype(F32))

    acc = x_ref[...]
    pair = 2 * DN_DV
    heads = [(dn_head, h) for h in range(DN_HEADS)] + [(ret_head, h) for h in range(RET_HEADS)]
    for c in range(len(heads) // 2):
        (f0, h0), (f1, h1) = heads[2 * c], heads[2 * c + 1]
        mix = jnp.concatenate([f0(h0), f1(h1)], axis=1).astype(BF16)
        acc = acc + jnp.dot(mix, w_ref[c * pair:(c + 1) * pair, :], preferred_element_type=F32)
    o_ref[...] = acc


def _pq_body(h_ref, nw_ref, w_ref, keys_ref, hnt_ref, inv_ref, sc_ref):
    x = h_ref[...]
    q, inv_rms = _rms_matmul(x, nw_ref[...], w_ref)
    hnt = (x * inv_rms * nw_ref[...]).T
    amax = jnp.maximum(jnp.max(jnp.abs(hnt), axis=0, keepdims=True), F8_TINY)
    hnt_ref[...] = (hnt * (F8_TARGET / amax)).astype(F8)
    inv_ref[...] = amax * (1.0 / F8_TARGET)
    half = PK_DQ // 2
    for hp in range(2 * PK_HEADS):
        sc_ref[hp] = _mm_nt(keys_ref[hp], q[:, hp * half:(hp + 1) * half])


def _outproj_pq_body(odn_ref, z_ref, or_ref, rg_ref, x_ref, dnw_ref, gnw_ref, gnb_ref, w_ref,
                     nw_ref, wpq_ref, keys_ref, o_ref, hnt_ref, inv_ref, sc_ref):
    _outproj_body(odn_ref, z_ref, or_ref, rg_ref, x_ref, dnw_ref, gnw_ref, gnb_ref, w_ref, o_ref)
    _pq_body(o_ref, nw_ref, wpq_ref, keys_ref, hnt_ref, inv_ref, sc_ref)


def _outproj_pq(o_dn, proj, o_r, x, dnw, gnw, gnb, w_out, nw, w_pq, keys):
    t, d = x.shape
    tm = TM_DENSE
    nk = 2 * PK_HEADS

    def full(shape):
        return pl.BlockSpec(shape, lambda i: (0,) * len(shape), pipeline_mode=pl.Buffered(1))

    return pl.pallas_call(
        _outproj_pq_body,
        grid=(t // tm,),
        in_specs=[pl.BlockSpec((tm, DN_V), lambda i: (i, 0)),
                  pl.BlockSpec((tm, DN_V), lambda i: (i, 3)),
                  pl.BlockSpec((tm, RET_V), lambda i: (i, 0)),
                  pl.BlockSpec((tm, RET_V), lambda i: (i, 6)),
                  pl.BlockSpec((tm, d), lambda i: (i, 0)),
                  full((1, DN_DV)), full((1, RET_V)), full((1, RET_V)), full((DN_V + RET_V, d)),
                  full((1, d)), full((d, PK_HEADS * PK_DQ)), full((nk, N_KEYS, PK_DQ // 2))],
        out_specs=[pl.BlockSpec((tm, d), lambda i: (i, 0)),
                   pl.BlockSpec((d, tm), lambda i: (0, i)),
                   pl.BlockSpec((1, tm), lambda i: (0, i)),
                   pl.BlockSpec((nk, N_KEYS, tm), lambda i: (0, 0, i))],
        out_shape=[jax.ShapeDtypeStruct((t, d), F32),
                   jax.ShapeDtypeStruct((d, t), F8),
                   jax.ShapeDtypeStruct((1, t), F32),
                   jax.ShapeDtypeStruct((nk, N_KEYS, t), F32)],
        compiler_params=_cparams(("parallel",)),
        name="outproj_pq",
    )(o_dn, proj, o_r, proj, x, dnw, gnw, gnb, w_out, nw, w_pq, keys)


def _route(a, bsc, exact):
    tl = a.shape[1]
    k = PK_TOPK
    keyid = lax.broadcasted_iota(jnp.int32, (N_KEYS, tl), 0)
    slot = lax.broadcasted_iota(jnp.int32, (k, tl), 0)

    def extract(s, ids, n_ids):
        m = jnp.max(s, axis=0, keepdims=True)
        hit = s == m
        if exact:
            hit = ids == jnp.min(jnp.where(hit, ids, n_ids), axis=0, keepdims=True)
        return m, hit

    def top_k(s):
        rank = jnp.full((N_KEYS, tl), float(k), F32)
        vals = jnp.zeros((k, tl), F32)
        for r in range(k):
            m, hit = extract(s, keyid, N_KEYS)
            rank = jnp.where(hit, float(r), rank)
            s = jnp.where(hit, NEG_INF, s)
            vals = jnp.where(slot == r, m, vals)
        return vals, rank

    av, rank_a = top_k(a)
    bv, rank_b = top_k(bsc)

    k2 = k // 2
    assert k2 & (k2 - 1) == 0
    n_cand = k + (k - 1) * k2
    cand = jnp.concatenate([av[0:1, :] + bv] + [av[r:r + 1, :] + bv[0:k2, :] for r in range(1, k)],
                           axis=0)
    row = lax.broadcasted_iota(jnp.int32, (n_cand, tl), 0)
    tail = row - k
    cid = jnp.where(row < k, row,
                    (1 + (tail >> (k2.bit_length() - 1))) * k + (tail & (k2 - 1)))
    work = cand
    for _ in range(k):
        _, hit = extract(work, cid, k * k)
        work = jnp.where(hit, NEG_INF, work)
    sel = (work == NEG_INF).astype(F32)
    zsum = jnp.sum(sel * jnp.exp(cand - cand[0:1, :]), axis=0, keepdims=True)

    cnt = jnp.zeros((N_KEYS, tl), F32)
    for r in range(k):
        lo, hi = (0, k) if r == 0 else (k + (r - 1) * k2, k + r * k2)
        cnt_r = jnp.sum(sel[lo:hi, :], axis=0, keepdims=True)
        cnt = jnp.where(rank_a == float(r), cnt_r, cnt)

    def full_count(x):
        return jnp.sum(x, axis=0, keepdims=True) == float(k)

    ok = (full_count((rank_a < float(k)).astype(F32)) & full_count((rank_b < float(k)).astype(F32))
          & full_count(sel))
    eb = jnp.exp(bsc - bv[0:1, :])
    ea = jnp.exp(a - av[0:1, :]) / zsum
    return rank_b, eb, cnt, ea, ok


def _router_body(sc_ref, u_ref, v_ref, rankb_ref, eb_ref, cnt_ref, ea_ref, u8_ref, uinv_ref, vt_ref):
    u = u_ref[...]
    amax = jnp.max(jnp.max(jnp.abs(u), axis=0, keepdims=True), axis=1, keepdims=True)
    amax = jnp.maximum(amax, F8_TINY)
    u8_ref[...] = (u * (F8_TARGET / amax)).astype(F8)
    uinv_ref[0] = jnp.broadcast_to(amax * (1.0 / F8_TARGET), uinv_ref.shape[1:])
    vt_ref[...] = v_ref[...].T.astype(BF16)

    def run(exact):
        rank_b, eb, cnt, ea, ok = _route(sc_ref[0], sc_ref[1], exact)
        rankb_ref[0] = rank_b.astype(BF16)
        eb_ref[0] = eb.astype(BF16)
        cnt_ref[0] = cnt
        ea_ref[0] = ea
        return ok

    ok = run(exact=False)
    n_bad = jnp.sum(jnp.where(ok, 0.0, 1.0), axis=1, keepdims=True)

    @pl.when(n_bad[0, 0] > 0.0)
    def _():
        run(exact=True)


def _router(scores, expert_u, expert_v):
    nk, n, t = scores.shape
    ne, d = expert_u.shape
    tl = TL_ROUTER
    nt = t // tl
    assert PK_HEADS * nt * N_KEYS == ne, "one 128-expert block is prepared per routing step"
    spec = pl.BlockSpec((1, n, tl), lambda h, j: (h, 0, j))
    shape = lambda dt: jax.ShapeDtypeStruct((PK_HEADS, n, t), dt)
    blk = lambda h, j: h * nt + j
    return pl.pallas_call(
        _router_body,
        grid=(PK_HEADS, nt),
        in_specs=[pl.BlockSpec((2, n, tl), lambda h, j: (h, 0, j)),
                  pl.BlockSpec((N_KEYS, d), lambda h, j: (blk(h, j), 0)),
                  pl.BlockSpec((N_KEYS, d), lambda h, j: (blk(h, j), 0))],
        out_specs=[spec, spec, spec, spec,
                   pl.BlockSpec((N_KEYS, d), lambda h, j: (blk(h, j), 0)),
                   pl.BlockSpec((1, 1, TM_PEER), lambda h, j: (blk(h, j), 0, 0)),
                   pl.BlockSpec((d, N_KEYS), lambda h, j: (0, blk(h, j)))],
        out_shape=[shape(BF16), shape(BF16), shape(F32), shape(F32),
                   jax.ShapeDtypeStruct((ne, d), F8),
                   jax.ShapeDtypeStruct((ne // N_KEYS, 1, TM_PEER), F32),
                   jax.ShapeDtypeStruct((d, ne), BF16)],
        compiler_params=_cparams(("parallel", "parallel")),
        name="router",
    )(scores, expert_u, expert_v)


def _peer_pre_body(u_ref, hnt_ref, o_ref):
    o_ref[...] = jnp.dot(u_ref[...], hnt_ref[...], preferred_element_type=F32)


def _peer_step(pre_ref, pre_next_ref, hnt_ref, u_ref, vt_ref, rankb_ref, eb_ref, cnt_ref, ea_ref,
               xinv_ref, uinv_ref, acc_ref, act_ref):
    k_piece = D_MODEL // NI_PEER
    pre_next = None
    for il in range(NI_PEER):
        rs = slice(il * N_KEYS, (il + 1) * N_KEYS)
        gate = jnp.zeros((N_KEYS, TM_PEER), BF16)
        for h in range(PK_HEADS):
            picked = rankb_ref[h] < cnt_ref[h, il:il + 1, :].astype(BF16)
            gate = gate + jnp.where(picked, eb_ref[h], 0.0) * ea_ref[h, il:il + 1, :].astype(BF16)
        unscale = xinv_ref[...] * uinv_ref[il]
        act_ref[rs, :] = _gelu(pre_ref[rs, :] * unscale).astype(BF16) * gate
        ks = slice(il * k_piece, (il + 1) * k_piece)
        zero = jnp.minimum(gate[:1, :], 0.0)
        x_piece = (hnt_ref[ks, :].astype(BF16) + zero).astype(F8)
        part = jnp.dot(u_ref[:, ks], x_piece, preferred_element_type=F32)
        pre_next = part if pre_next is None else pre_next + part
    pre_next_ref[...] = pre_next
    acc_ref[...] += jnp.dot(vt_ref[...], act_ref[...], preferred_element_type=F32)


def _peer_body(pre0_ref, hnt_ref, u_ref, vt_ref, rankb_ref, eb_ref, cnt_ref, ea_ref, xinv_ref,
               uinv_ref, o_ref, acc_ref, act_ref, pre_a_ref, pre_b_ref):
    i = pl.program_id(0)
    j = pl.program_id(1)
    args = (hnt_ref, u_ref, vt_ref, rankb_ref, eb_ref, cnt_ref, ea_ref, xinv_ref, uinv_ref, acc_ref,
            act_ref)

    @pl.when((i == 0) & (j == 0))
    def _():
        pre_a_ref[...] = pre0_ref[...]

    @pl.when(j == 0)
    def _():
        acc_ref[...] = jnp.zeros(acc_ref.shape, F32)

    @pl.when(lax.rem(j, 2) == 0)
    def _():
        _peer_step(pre_a_ref, pre_b_ref, *args)

    @pl.when(lax.rem(j, 2) == 1)
    def _():
        _peer_step(pre_b_ref, pre_a_ref, *args)

    @pl.when(j == pl.num_programs(1) - 1)
    def _():
        o_ref[...] = acc_ref[...].T


def _peer(hnt, xinv, u_f8, uinv, vt_bf, rankb, eb, cnt, ea):
    d, t = hnt.shape
    ne = u_f8.shape[0]
    tm, te = TM_PEER, TE_PEER
    ni, nj = t // tm, ne // te
    assert nj % 2 == 0, "pre-activation buffers alternate with the expert-tile index"
    pre0 = pl.pallas_call(
        _peer_pre_body,
        grid=(1,),
        in_specs=[pl.BlockSpec((te, d), lambda i: (0, 0)), pl.BlockSpec((d, tm), lambda i: (0, 0))],
        out_specs=pl.BlockSpec((te, tm), lambda i: (0, 0)),
        out_shape=jax.ShapeDtypeStruct((te, tm), F32),
        compiler_params=_cparams(("arbitrary",)),
        name="peer_pre",
    )(u_f8, hnt)

    def next_i(i, j):
        return jnp.minimum(i + (j + 1) // nj, ni - 1)

    key_spec = pl.BlockSpec((PK_HEADS, N_KEYS, tm), lambda i, j: (0, 0, i))
    blk_spec = pl.BlockSpec((PK_HEADS, NI_PEER, tm), lambda i, j: (0, j, i))
    return pl.pallas_call(
        _peer_body,
        grid=(ni, nj),
        in_specs=[pl.BlockSpec((te, tm), lambda i, j: (0, 0)),
                  pl.BlockSpec((d, tm), lambda i, j: (0, next_i(i, j))),
                  pl.BlockSpec((te, d), lambda i, j: ((j + 1) % nj, 0)),
                  pl.BlockSpec((d, te), lambda i, j: (0, j)),
                  key_spec, key_spec, blk_spec, blk_spec,
                  pl.BlockSpec((1, tm), lambda i, j: (0, i)),
                  pl.BlockSpec((NI_PEER, 1, tm), lambda i, j: (j, 0, 0))],
        out_specs=pl.BlockSpec((tm, d), lambda i, j: (i, 0)),
        out_shape=jax.ShapeDtypeStruct((t, d), F32),
        scratch_shapes=[pltpu.VMEM((d, tm), F32), pltpu.VMEM((te, tm), BF16),
                        pltpu.VMEM((te, tm), F32), pltpu.VMEM((te, tm), F32)],
        compiler_params=_cparams(("arbitrary", "arbitrary")),
        name="peer",
    )(pre0, hnt, u_f8, vt_bf, rankb, eb, cnt, ea, xinv, uinv)


def _ple_body(h_ref, po_ref, p_ref, nple_ref, wg_ref, wp_ref, nfin_ref, o_ref):
    h2 = h_ref[...] + po_ref[...]
    gate = jax.nn.sigmoid(
        jnp.dot(_rms(h2, nple_ref[...]).astype(BF16), wg_ref[...], preferred_element_type=F32))
    ple = jnp.dot(p_ref[...].astype(BF16), wp_ref[...], preferred_element_type=F32)
    o_ref[...] = _rms(h2 + gate * ple, nfin_ref[...])


def _ple(h1, peer_out, p, nple, wg, wp, nfin):
    t, d = h1.shape
    tm = TM_DENSE
    full = lambda shape: pl.BlockSpec(shape, lambda i: (0, 0))
    return pl.pallas_call(
        _ple_body,
        grid=(t // tm,),
        in_specs=[pl.BlockSpec((tm, d), lambda i: (i, 0)),
                  pl.BlockSpec((tm, d), lambda i: (i, 0)),
                  pl.BlockSpec((tm, PLE_DIM), lambda i: (i, 0)),
                  full((1, d)), full((d, d)), full((PLE_DIM, d)), full((1, d))],
        out_specs=pl.BlockSpec((tm, d), lambda i: (i, 0)),
        out_shape=jax.ShapeDtypeStruct((t, d), F32),
        compiler_params=_cparams(("parallel",)),
        name="ple",
    )(h1, peer_out, p, nple, wg, wp, nfin)


def _lane_row(vec, offset):
    return jnp.zeros((1, LANE), F32).at[0, offset:offset + vec.shape[0]].set(vec.astype(F32))


def _head_selector(offset):
    sel = np.zeros((LANE, DN_V), np.float32)
    for h in range(DN_HEADS):
        sel[offset + h, h * DN_DV:(h + 1) * DN_DV] = 1.0
    return jnp.asarray(sel, dtype=BF16)


def kernel(x, p, positions, norm_mix, w_in, conv_w, a_log, dt_bias, dn_norm, ret_gn_w, ret_gn_b,
           w_out, norm_ffn, w_pq, sub_keys, expert_u, expert_v, norm_ple, w_ple_gate, w_ple_proj,
           norm_final):
    b, s, d = x.shape
    t = b * s
    depth = w_in.shape[0]
    assert depth == 1, "the final rms_norm is fused into the single layer's ple kernel"
    half = RET_DK // 2
    inv_freq = ROPE_BASE ** (-jnp.arange(half, dtype=F32) / half)
    freq_row = jnp.tile(inv_freq, LANE // half).reshape(1, LANE)
    pos = positions.reshape(t, 1)
    selg, selb = _head_selector(0), _head_selector(DN_HEADS)

    h = x.reshape(t, d)
    for i in range(depth):
        w = w_in[i].astype(BF16)
        w_ab = jnp.pad(w[:, AB_OFF:AB_OFF + 2 * DN_HEADS], ((0, 0), (0, LANE - 2 * DN_HEADS)))
        proj, proj_ab = _inproj(h, norm_mix[i].reshape(1, d), w, w[:, AB_OFF + 2 * DN_HEADS:], w_ab)

        qa, kd, u, wv, o0, gend, w_out16, w_pq16, w_gate16 = _dn_local(
            proj, proj_ab, conv_w[i].astype(F32).T, _lane_row(a_log[i], 0), _lane_row(dt_bias[i], 0),
            selg, selb, (w_out[i], w_pq[i], w_ple_gate[i]), b, s)
        o_dn = _dn_scan(qa, kd, u, wv, o0, gend, b, s)
        o_r = _ret(proj, pos, freq_row, b, s)
        keys = sub_keys[i].reshape(2 * PK_HEADS, N_KEYS, PK_DQ // 2).astype(BF16)
        h1, hnt, xinv, scores = _outproj_pq(
            o_dn, proj, o_r, h, dn_norm[i].reshape(1, DN_DV), ret_gn_w[i].reshape(1, RET_V),
            ret_gn_b[i].reshape(1, RET_V), w_out16, norm_ffn[i].reshape(1, d), w_pq16, keys)
        rankb, eb, cnt, ea, u_f8, uinv, vt_bf = _router(scores, expert_u[i], expert_v[i])
        peer_out = _peer(hnt, xinv, u_f8, uinv, vt_bf, rankb, eb, cnt, ea)

        h = _ple(h1, peer_out, p[i].reshape(t, PLE_DIM), norm_ple[i].reshape(1, d),
                 w_gate16, w_ple_proj[i].astype(BF16), norm_final.reshape(1, d))
    return h.reshape(b, s, d)
```
